```python
import math
import jax, jax.numpy as jnp
from jax import lax
import numpy as np

D_MODEL = 1024
BATCH = 4
SEQ = 4096
DEPTH = 2

D_MIX = D_MODEL
D_A = 3 * D_MIX // 8
D_B = 3 * D_MIX // 8
D_C = D_MIX - D_A - D_B
HEAD_DIM = 64
N_Q_HEADS = D_B // HEAD_DIM
N_KV_HEADS = 2
GROUP = N_Q_HEADS // N_KV_HEADS
D_KV = N_KV_HEADS * HEAD_DIM
D_IN = 2 * D_A + D_B + 2 * D_KV + 3 * D_C
LRU_BW = 64
LRU_BLOCKS = D_A // LRU_BW
C_LRU = 8.0
CONV_A = 4
WINDOW = 128
BLOCK = 128
ROPE_THETA = 500000.0
ROT_DIM = HEAD_DIM // 4
CONV_C = 3
HY_EMB = 33
HY_BANDS = (HY_EMB - 1) // 2
HY_WIDTH = 64
HY_TARGET = 1e-2
HY_MAX_DECAY = math.log(HY_TARGET) / 0.3
HY_MIN_DECAY = math.log(HY_TARGET) / 1.5
D_FF = 4 * D_MODEL
EPS = 1e-6
NEG = -1e30

kernel_name = "hybrid_rglru_swa_hyena_encoder"

F32 = jnp.float32


def rmsnorm(x, g):
    xf = x.astype(F32)
    y = xf * lax.rsqrt(jnp.mean(xf * xf, axis=-1, keepdims=True) + EPS) * g.astype(F32)
    return y.astype(x.dtype)


def depthwise_conv(x, w, b, pad_left, pad_right):
    L = x.shape[1]
    xp = jnp.pad(x, ((0, 0), (pad_left, pad_right), (0, 0)))
    y = xp[:, 0:L] * w[0]
    for k in range(1, w.shape[0]):
        y = y + xp[:, k:k + L] * w[k]
    return y + b


def rope_tables(L):
    pos = jnp.arange(L, dtype=F32)
    inv_freq = ROPE_THETA ** (-jnp.arange(0, ROT_DIM, 2, dtype=F32) / ROT_DIM)
    ang = pos[:, None] * inv_freq[None, :]
    return jnp.cos(ang), jnp.sin(ang)


def rope_partial(t, cos, sin):
    half = ROT_DIM // 2
    tf = t.astype(F32)
    t1, t2, rest = tf[..., :half], tf[..., half:ROT_DIM], tf[..., ROT_DIM:]
    c = cos[None, :, None, :]
    s = sin[None, :, None, :]
    return jnp.concatenate([t1 * c - t2 * s, t2 * c + t1 * s, rest], axis=-1).astype(t.dtype)


def linear_scan(a, b, reverse):
    def combine(c1, c2):
        a1, b1 = c1
        a2, b2 = c2
        return a1 * a2, a2 * b1 + b2
    _, h = lax.associative_scan(combine, (a, b), axis=1, reverse=reverse)
    return h


def rglru_mixer(u, gate, conv_w, conv_b, wa, ba, wx, bx, lam):
    xc = depthwise_conv(u, conv_w, conv_b, 2, 1).astype(F32)
    Bn, L, _ = xc.shape
    xb = xc.reshape(Bn, L, LRU_BLOCKS, LRU_BW)
    r = jax.nn.sigmoid(jnp.einsum("blhi,nhij->nblhj", xb, wa.astype(F32)).reshape(2, Bn, L, D_A)
                       + ba.astype(F32)[:, None, None, :])
    i = jax.nn.sigmoid(jnp.einsum("blhi,nhij->nblhj", xb, wx.astype(F32)).reshape(2, Bn, L, D_A)
                       + bx.astype(F32)[:, None, None, :])
    log_a = -C_LRU * r * jax.nn.softplus(-lam.astype(F32))[:, None, None, :]
    a = jnp.exp(log_a)
    b = jnp.sqrt(-jnp.expm1(2.0 * log_a)) * (i * xc[None])
    h = linear_scan(a[0], b[0], reverse=False) + linear_scan(a[1], b[1], reverse=True)
    return (h * jax.nn.gelu(gate.astype(F32))).astype(u.dtype)


def window_attention(q, k, v, sink, cos, sin):
    Bn, L = q.shape[0], q.shape[1]
    nblk = L // BLOCK
    q = rope_partial(q.reshape(Bn, L, N_Q_HEADS, HEAD_DIM), cos, sin)
    k = rope_partial(k.reshape(Bn, L, N_KV_HEADS, HEAD_DIM), cos, sin)
    v = v.reshape(Bn, L, N_KV_HEADS, HEAD_DIM)
    qb = q.reshape(Bn, nblk, BLOCK, N_KV_HEADS, GROUP, HEAD_DIM)

    def band(t):
        tp = jnp.pad(t, ((0, 0), (BLOCK, BLOCK), (0, 0), (0, 0)))
        tp = tp.reshape(Bn, nblk + 2, BLOCK, N_KV_HEADS, HEAD_DIM)
        return jnp.concatenate([tp[:, :-2], tp[:, 1:-1], tp[:, 2:]], axis=2)

    kb, vb = band(k), band(v)
    s = jnp.einsum("bnqhgd,bnshd->bnhgqs", qb.astype(F32), kb.astype(F32)) * (HEAD_DIM ** -0.5)
    blk = jnp.arange(nblk)[:, None]
    qpos = blk * BLOCK + jnp.arange(BLOCK)[None, :]
    kpos = (blk - 1) * BLOCK + jnp.arange(3 * BLOCK)[None, :]
    diff = qpos[:, :, None] - kpos[:, None, :]
    valid = (jnp.abs(diff) <= WINDOW) & (kpos[:, None, :] >= 0) & (kpos[:, None, :] < L)
    s = jnp.where(valid[None, :, None, None], s, NEG)
    sk = sink.astype(F32).reshape(N_KV_HEADS, GROUP)[None, None, :, :, None, None]
    m = jnp.maximum(jnp.max(s, axis=-1, keepdims=True), sk)
    p = jnp.exp(s - m)
    denom = jnp.sum(p, axis=-1, keepdims=True) + jnp.exp(sk - m)
    o = jnp.einsum("bnhgqs,bnshd->bnqhgd", p / denom, vb.astype(F32))
    return o.reshape(Bn, L, D_B).astype(q.dtype)


def hyena_filters(L, w1, b1, freq, w2, b2, w3):
    t = jnp.linspace(0.0, 1.0, L, dtype=F32)[:, None]
    w = 2.0 * math.pi * jnp.arange(L, dtype=F32)[:, None] / L
    f = jnp.linspace(1e-4, HY_BANDS - 1, HY_BANDS, dtype=F32)[None, :]
    z = jnp.concatenate([t, jnp.cos(f * w), -jnp.sin(f * w)], axis=-1)
    fr = freq.astype(F32)
    hdn = jnp.sin(fr * (z @ w1.astype(F32) + b1.astype(F32)))
    hdn = jnp.sin(fr * (hdn @ w2.astype(F32) + b2.astype(F32)))
    filt = (hdn @ w3.astype(F32)).reshape(L, 2, D_C)
    deltas = jnp.abs(jnp.linspace(HY_MIN_DECAY, HY_MAX_DECAY, D_C, dtype=F32))
    decay = jnp.exp(-t * deltas[None, :])
    filt = filt * decay[:, None, :]
    return filt[:, 0], filt[:, 1]


def hyena_mixer(u, conv_w, conv_b, h_fwd, h_bwd, bias):
    uc = depthwise_conv(u, conv_w, conv_b, 1, 1).astype(F32)
    L = uc.shape[1]
    x0, x1, v = jnp.split(uc, 3, axis=-1)
    z = v * x1
    filt_circ = jnp.concatenate([h_fwd, jnp.zeros((1, D_C), F32), h_bwd[1:][::-1]], axis=0)
    zf = jnp.fft.rfft(z, n=2 * L, axis=1)
    hf = jnp.fft.rfft(filt_circ, n=2 * L, axis=0)
    y = jnp.fft.irfft(zf * hf[None], n=2 * L, axis=1)[:, :L] + z * bias.astype(F32)
    return (y * x0).astype(u.dtype)


def setup_inputs(seed: int = 0) -> dict:
    key = jax.random.key(seed)
    ks = jax.random.split(key, 32)

    def nrm(k, shape, scale):
        return jax.random.normal(k, shape, F32) * scale

    a0 = jax.random.uniform(ks[9], (DEPTH, 2, D_A), F32, 0.9, 0.999)
    return {
        "x": nrm(ks[0], (BATCH, SEQ, D_MODEL), 1.0),
        "norm_mix_g": 1.0 + nrm(ks[1], (DEPTH, D_MODEL), 0.02),
        "w_in": nrm(ks[2], (DEPTH, D_MODEL, D_IN), D_MODEL ** -0.5),
        "conv_a_w": nrm(ks[3], (DEPTH, CONV_A, D_A), CONV_A ** -0.5),
        "conv_a_b": nrm(ks[4], (DEPTH, D_A), 0.01),
        "lru_wa": nrm(ks[5], (DEPTH, 2, LRU_BLOCKS, LRU_BW, LRU_BW), LRU_BW ** -0.5),
        "lru_ba": nrm(ks[6], (DEPTH, 2, D_A), 0.01),
        "lru_wx": nrm(ks[7], (DEPTH, 2, LRU_BLOCKS, LRU_BW, LRU_BW), LRU_BW ** -0.5),
        "lru_bx": nrm(ks[8], (DEPTH, 2, D_A), 0.01),
        "lru_lambda": jnp.log(a0) - jnp.log1p(-a0),
        "attn_sink": nrm(ks[10], (DEPTH, N_Q_HEADS), 0.5),
        "hy_conv_w": nrm(ks[11], (DEPTH, CONV_C, 3 * D_C), CONV_C ** -0.5),
        "hy_conv_b": nrm(ks[12], (DEPTH, 3 * D_C), 0.01),
        "hy_w1": nrm(ks[13], (DEPTH, HY_EMB, HY_WIDTH), HY_EMB ** -0.5),
        "hy_b1": nrm(ks[14], (DEPTH, HY_WIDTH), 0.1),
        "hy_freq": 1.0 + nrm(ks[15], (DEPTH, HY_WIDTH), 0.05),
        "hy_w2": nrm(ks[16], (DEPTH, HY_WIDTH, HY_WIDTH), HY_WIDTH ** -0.5),
        "hy_b2": nrm(ks[17], (DEPTH, HY_WIDTH), 0.1),
        "hy_w3": nrm(ks[18], (DEPTH, HY_WIDTH, 2 * D_C), HY_WIDTH ** -0.5),
        "hy_bias": nrm(ks[19], (DEPTH, D_C), 0.1),
        "gnorm_a": 1.0 + nrm(ks[20], (DEPTH, D_A), 0.02),
        "gnorm_b": 1.0 + nrm(ks[21], (DEPTH, D_B), 0.02),
        "gnorm_c": 1.0 + nrm(ks[22], (DEPTH, D_C), 0.02),
        "w_out": nrm(ks[23], (DEPTH, D_MIX, D_MODEL), D_MIX ** -0.5),
        "norm_mlp_g": 1.0 + nrm(ks[24], (DEPTH, D_MODEL), 0.02),
        "w_up": nrm(ks[25], (DEPTH, D_MODEL, D_FF), D_MODEL ** -0.5),
        "w_down": nrm(ks[26], (DEPTH, D_FF, D_MODEL), D_FF ** -0.5),
        "final_norm_g": 1.0 + nrm(ks[27], (D_MODEL,), 0.02),
    }


def reference(x, norm_mix_g, w_in, conv_a_w, conv_a_b, lru_wa, lru_ba, lru_wx, lru_bx, lru_lambda,
              attn_sink, hy_conv_w, hy_conv_b, hy_w1, hy_b1, hy_freq, hy_w2, hy_b2, hy_w3, hy_bias,
              gnorm_a, gnorm_b, gnorm_c, w_out, norm_mlp_g, w_up, w_down, final_norm_g):
    L = x.shape[1]
    cos, sin = rope_tables(L)
    splits = [D_A, 2 * D_A, 2 * D_A + D_B, 2 * D_A + D_B + D_KV, 2 * D_A + D_B + 2 * D_KV]
    for i in range(DEPTH):
        h = rmsnorm(x, norm_mix_g[i])
        p = h @ w_in[i]
        a_x, a_g, q, k, v, c_u = jnp.split(p, splits, axis=-1)
        y_a = rglru_mixer(a_x, a_g, conv_a_w[i], conv_a_b[i], lru_wa[i], lru_ba[i],
                          lru_wx[i], lru_bx[i], lru_lambda[i])
        y_b = window_attention(q, k, v, attn_sink[i], cos, sin)
        h_fwd, h_bwd = hyena_filters(L, hy_w1[i], hy_b1[i], hy_freq[i], hy_w2[i], hy_b2[i], hy_w3[i])
        y_c = hyena_mixer(c_u, hy_conv_w[i], hy_conv_b[i], h_fwd, h_bwd, hy_bias[i])
        y = jnp.concatenate([rmsnorm(y_a, gnorm_a[i]), rmsnorm(y_b, gnorm_b[i]),
                             rmsnorm(y_c, gnorm_c[i])], axis=-1)
        x = x + (y @ w_out[i]).astype(x.dtype)
        h2 = rmsnorm(x, norm_mlp_g[i])
        x = x + (jnp.square(jax.nn.relu(h2 @ w_up[i])) @ w_down[i]).astype(x.dtype)
    return rmsnorm(x, final_norm_g)
```

```python
import functools
import math

import numpy as np
import jax
import jax.numpy as jnp
from jax import lax
from jax.experimental import pallas as pl
from jax.experimental.pallas import tpu as pltpu

F32 = jnp.float32
BF16 = jnp.bfloat16

D_MODEL = 1024
D_A = 384
D_B = 384
D_C = 256
HEAD_DIM = 64
N_Q_HEADS = 6
N_KV_HEADS = 2
GROUP = N_Q_HEADS // N_KV_HEADS
D_KV = N_KV_HEADS * HEAD_DIM
D_QKV = D_B + 2 * D_KV
D_IN = 2 * D_A + D_QKV + 3 * D_C
C_LRU = 8.0
WINDOW = 128
BLOCK = 128
ROPE_THETA = 500000.0
ROT_DIM = HEAD_DIM // 4
HY_EMB = 33
HY_BANDS = (HY_EMB - 1) // 2
HY_WIDTH = 64
HY_TARGET = 1e-2
HY_MAX_DECAY = math.log(HY_TARGET) / 0.3
HY_MIN_DECAY = math.log(HY_TARGET) / 1.5
D_FF = 4 * D_MODEL
EPS = 1e-6
NEG = -1e30

LANES = 128
SUBLANES = 8
VMEM_LIMIT = 56 * 1024 * 1024

SCAN_SEGS = SUBLANES
DFT_BLK = 128
DFT_NBLK = 64
DFT_SUB = 16


def _cparams(sem):
    return pltpu.CompilerParams(dimension_semantics=sem, vmem_limit_bytes=VMEM_LIMIT)


def _rms(x, g):
    return x * lax.rsqrt(jnp.mean(x * x, axis=-1, keepdims=True) + EPS) * g


def _in_proj_kernel(x_ref, g_ref, w_ref, rc_ref, rs_ref, oa_ref, oq_ref, oc_ref):
    h = _rms(x_ref[...], g_ref[...]).astype(BF16)
    oa_ref[...] = jnp.dot(h, w_ref[:, :2 * D_A], preferred_element_type=F32)
    oc_ref[...] = jnp.dot(h, w_ref[:, 2 * D_A + D_QKV:], preferred_element_type=F32)
    qkv = jnp.dot(h, w_ref[:, 2 * D_A:2 * D_A + D_QKV], preferred_element_type=F32)
    qk = qkv[:, :D_B + D_KV]
    n = D_B + D_KV
    half = ROT_DIM // 2
    lane = lax.broadcasted_iota(jnp.int32, qk.shape, 1) % HEAD_DIM
    swapped = jnp.where(lane < half, pltpu.roll(qk, n - half, axis=1), pltpu.roll(qk, half, axis=1))
    oq_ref[:, :n] = qk * rc_ref[...] + swapped * rs_ref[...]
    oq_ref[:, n:] = qkv[:, n:]


def _in_proj(x2, g, w_bf, rc, rs, seq, tm):
    n_tok = x2.shape[0]
    nrb = seq // tm
    return pl.pallas_call(
        _in_proj_kernel,
        grid=(n_tok // tm,),
        in_specs=[
            pl.BlockSpec((tm, D_MODEL), lambda i: (i, 0)),
            pl.BlockSpec((1, D_MODEL), lambda i: (0, 0)),
            pl.BlockSpec((D_MODEL, D_IN), lambda i: (0, 0)),
            pl.BlockSpec((tm, D_B + D_KV), lambda i: (i % nrb, 0)),
            pl.BlockSpec((tm, D_B + D_KV), lambda i: (i % nrb, 0)),
        ],
        out_specs=[
            pl.BlockSpec((tm, 2 * D_A), lambda i: (i, 0)),
            pl.BlockSpec((tm, D_QKV), lambda i: (i, 0)),
            pl.BlockSpec((tm, 3 * D_C), lambda i: (i, 0)),
        ],
        out_shape=[
            jax.ShapeDtypeStruct((n_tok, 2 * D_A), F32),
            jax.ShapeDtypeStruct((n_tok, D_QKV), F32),
            jax.ShapeDtypeStruct((n_tok, 3 * D_C), F32),
        ],
        compiler_params=_cparams(("parallel",)),
        name="in_proj",
    )(x2, g, w_bf, rc, rs)


def _dwconv(pad_ref, r0, rows, w, b, pad_left):
    win = pad_ref[pl.ds(r0, rows + 2 * SUBLANES), :]
    total = rows + 2 * SUBLANES
    acc = None
    for k in range(w.shape[0]):
        shift = (pad_left - k) % total
        tap = win if shift == 0 else pltpu.roll(win, shift, axis=0)
        term = tap[SUBLANES:SUBLANES + rows] * w[k:k + 1, :]
        acc = term if acc is None else acc + term
    return acc + b


def _fill_padded(pad_ref, src_ref, seq):
    zeros = jnp.zeros((SUBLANES, pad_ref.shape[1]), F32)
    pad_ref[0:SUBLANES, :] = zeros
    pad_ref[seq + SUBLANES:seq + 2 * SUBLANES, :] = zeros
    pad_ref[SUBLANES:seq + SUBLANES, :] = src_ref[...]


def _softplus(x):
    return jnp.maximum(x, 0.0) + jnp.log1p(jnp.exp(-jnp.abs(x)))


def _gelu_tanh(x):
    c = math.sqrt(2.0 / math.pi)
    return x * (0.5 * (1.0 + jnp.tanh(c * (x + 0.044715 * (x * x * x)))))


def _neg_expm1(x):
    series = -x * (1.0 + x * (1.0 / 2 + x * (1.0 / 6 + x * (1.0 / 24 + x * (1.0 / 120 + x * (
        1.0 / 720 + x * (1.0 / 5040 + x * (1.0 / 40320))))))))
    return jnp.where(x > -0.25, series, 1.0 - jnp.exp(x))


def _lru_kernel(u_ref, gate_ref, cw_ref, cb_ref, w_ref, bias_ref, lam_ref, o_ref,
                upad, af, bf, ar, br, *, seq, rows):
    seg = seq // SCAN_SEGS
    _fill_padded(upad, u_ref, seq)
    cw = cw_ref[...]
    cb = cb_ref[...]
    w = w_ref[0]
    bias = bias_ref[0]
    sp = _softplus(-lam_ref[0])

    def gates(ci, carry):
        r0 = pl.multiple_of(ci * rows, rows)
        xc = _dwconv(upad, r0, rows, cw, cb, 2)
        g = jax.nn.sigmoid(
            jnp.dot(xc, w, preferred_element_type=F32, precision=lax.Precision.HIGHEST) + bias)
        for d, (a_ref, b_ref) in enumerate(((af, bf), (ar, br))):
            r = g[:, (2 * d) * LANES:(2 * d + 1) * LANES]
            i = g[:, (2 * d + 1) * LANES:(2 * d + 2) * LANES]
            log_a = (-C_LRU * r) * sp[:, d * LANES:(d + 1) * LANES]
            a_ref[pl.ds(r0, rows), :] = jnp.exp(log_a)
            b_ref[pl.ds(r0, rows), :] = jnp.sqrt(_neg_expm1(2.0 * log_a)) * (i * xc)
        return carry

    lax.fori_loop(0, seq // rows, gates, 0)

    def scan(i, carry):
        pf, hf, pr, hr = carry
        a = af[pl.ds(i, SCAN_SEGS, stride=seg), :]
        b = bf[pl.ds(i, SCAN_SEGS, stride=seg), :]
        pf = a * pf
        hf = a * hf + b
        af[pl.ds(i, SCAN_SEGS, stride=seg), :] = pf
        bf[pl.ds(i, SCAN_SEGS, stride=seg), :] = hf
        j = seg - 1 - i
        a = ar[pl.ds(j, SCAN_SEGS, stride=seg), :]
        b = br[pl.ds(j, SCAN_SEGS, stride=seg), :]
        pr = a * pr
        hr = a * hr + b
        ar[pl.ds(j, SCAN_SEGS, stride=seg), :] = pr
        br[pl.ds(j, SCAN_SEGS, stride=seg), :] = hr
        return pf, hf, pr, hr

    one = jnp.ones((SCAN_SEGS, LANES), F32)
    zero = jnp.zeros((SCAN_SEGS, LANES), F32)
    pf, hf, pr, hr = lax.fori_loop(0, seg, scan, (one, zero, one, zero))

    sub = lax.broadcasted_iota(jnp.int32, (SCAN_SEGS, LANES), 0)
    cf = zero
    cr = zero
    for _ in range(SCAN_SEGS - 1):
        cf = jnp.where(sub == 0, 0.0, pltpu.roll(hf + pf * cf, 1, axis=0))
        cr = jnp.where(sub == SCAN_SEGS - 1, 0.0, pltpu.roll(hr + pr * cr, SCAN_SEGS - 1, axis=0))

    per_seg = seg // rows

    def combine(ci, carry):
        r0 = pl.multiple_of(ci * rows, rows)
        s = ci // per_seg
        pick = sub == s
        cfs = jnp.sum(jnp.where(pick, cf, 0.0), axis=0, keepdims=True)
        crs = jnp.sum(jnp.where(pick, cr, 0.0), axis=0, keepdims=True)
        sl = pl.ds(r0, rows)
        h = (bf[sl, :] + af[sl, :] * cfs) + (br[sl, :] + ar[sl, :] * crs)
        o_ref[sl, :] = h * _gelu_tanh(gate_ref[sl, :])
        return carry

    lax.fori_loop(0, seq // rows, combine, 0)


def _lru(pa, cw, cb, wg, bias, lam, batch, seq, rows=256):
    n_tok = pa.shape[0]
    ng = D_A // LANES
    kern = functools.partial(_lru_kernel, seq=seq, rows=rows)
    return pl.pallas_call(
        kern,
        grid=(batch, ng),
        in_specs=[
            pl.BlockSpec((seq, LANES), lambda b, g: (b, g)),
            pl.BlockSpec((seq, LANES), lambda b, g: (b, ng + g)),
            pl.BlockSpec((cw.shape[0], LANES), lambda b, g: (0, g)),
            pl.BlockSpec((1, LANES), lambda b, g: (0, g)),
            pl.BlockSpec((1, LANES, 4 * LANES), lambda b, g: (g, 0, 0)),
            pl.BlockSpec((1, 1, 4 * LANES), lambda b, g: (g, 0, 0)),
            pl.BlockSpec((1, 1, 2 * LANES), lambda b, g: (g, 0, 0)),
        ],
        out_specs=pl.BlockSpec((seq, LANES), lambda b, g: (b, g)),
        out_shape=jax.ShapeDtypeStruct((n_tok, D_A), F32),
        scratch_shapes=[pltpu.VMEM((seq + 2 * SUBLANES, LANES), F32)]
        + [pltpu.VMEM((seq, LANES), F32) for _ in range(4)],
        compiler_params=_cparams(("parallel", "parallel")),
        name="rglru",
    )(pa, pa, cw, cb, wg, bias, lam)


def _attn_kernel(sink_ref, q_ref, k_ref, v_ref, o_ref, *, seq, tq):
    band = 3 * BLOCK
    nsub = tq // BLOCK
    i = pl.program_id(1)
    qi = lax.broadcasted_iota(jnp.int32, (BLOCK, band), 0)
    ki = lax.broadcasted_iota(jnp.int32, (BLOCK, band), 1)
    for jb in range(nsub):
        q0 = (i * nsub + jb) * BLOCK
        k0 = pl.multiple_of(jnp.clip(q0 - BLOCK, 0, seq - band), BLOCK)
        valid = jnp.abs((q0 + qi) - (k0 + ki)) <= WINDOW
        qb = q_ref[jb * BLOCK:(jb + 1) * BLOCK, :]
        kb = k_ref[pl.ds(k0, band), :].astype(BF16)
        vb = v_ref[pl.ds(k0, band), :].astype(BF16)
        outs = []
        for h in range(N_Q_HEADS):
            kv = h // GROUP
            qh = qb[:, h * HEAD_DIM:(h + 1) * HEAD_DIM].astype(BF16)
            kh = kb[:, kv * HEAD_DIM:(kv + 1) * HEAD_DIM]
            vh = vb[:, kv * HEAD_DIM:(kv + 1) * HEAD_DIM]
            s = lax.dot_general(qh, kh, (((1,), (1,)), ((), ())), preferred_element_type=F32)
            s = jnp.where(valid, s, NEG)
            sk = sink_ref[h]
            m = jnp.maximum(jnp.max(s, axis=-1, keepdims=True), sk)
            p = jnp.exp(s - m)
            denom = jnp.sum(p, axis=-1, keepdims=True) + jnp.exp(sk - m)
            o = jnp.dot(p.astype(BF16), vh, preferred_element_type=F32)
            outs.append(o / denom)
        o_ref[jb * BLOCK:(jb + 1) * BLOCK, :] = jnp.concatenate(outs, axis=-1)


def _attention(pq, sink, batch, seq, tq=512):
    n_tok = pq.shape[0]
    nq = seq // tq
    kcol = D_B // D_KV
    kern = functools.partial(_attn_kernel, seq=seq, tq=tq)
    return pl.pallas_call(
        kern,
        grid=(batch, nq),
        in_specs=[
            pl.BlockSpec(memory_space=pltpu.SMEM),
            pl.BlockSpec((tq, D_B), lambda b, i: (b * nq + i, 0)),
            pl.BlockSpec((seq, D_KV), lambda b, i: (b, kcol)),
            pl.BlockSpec((seq, D_KV), lambda b, i: (b, kcol + 1)),
        ],
        out_specs=pl.BlockSpec((tq, D_B), lambda b, i: (b * nq + i, 0)),
        out_shape=jax.ShapeDtypeStruct((n_tok, D_B), F32),
        compiler_params=_cparams(("parallel", "arbitrary")),
        name="win_attn",
    )(sink, pq, pq, pq)


def _hy_filter_kernel(z_ref, w1_ref, b1_ref, fr_ref, w2_ref, b2_ref, w3_ref, dec_ref, o_ref, *, seq):
    hi = lax.Precision.HIGHEST
    fr = fr_ref[...]
    h = jnp.sin(fr * (jnp.dot(z_ref[...], w1_ref[...], preferred_element_type=F32, precision=hi)
                      + b1_ref[...]))
    h = jnp.sin(fr * (jnp.dot(h, w2_ref[...], preferred_element_type=F32, precision=hi) + b2_ref[...]))
    f = jnp.dot(h, w3_ref[...], preferred_element_type=F32, precision=hi)
    rows = z_ref.shape[0]
    fwd = pl.program_id(0) * rows < seq
    o_ref[...] = jnp.where(fwd, f[:, :D_C], f[:, D_C:]) * dec_ref[...]


def _hy_filter(zemb, w1p, b1, fr, w2, b2, w3, dec2, seq, rows=1024):
    n = zemb.shape[0]
    kern = functools.partial(_hy_filter_kernel, seq=seq)
    full = lambda a: pl.BlockSpec(a.shape, lambda i: (0,) * a.ndim)
    return pl.pallas_call(
        kern,
        grid=(n // rows,),
        in_specs=[pl.BlockSpec((rows, zemb.shape[1]), lambda i: (i, 0)),
                  full(w1p), full(b1), full(fr), full(w2), full(b2), full(w3),
                  pl.BlockSpec((rows, D_C), lambda i: (i, 0))],
        out_specs=pl.BlockSpec((rows, D_C), lambda i: (i, 0)),
        out_shape=jax.ShapeDtypeStruct((n, D_C), F32),
        compiler_params=_cparams(("parallel",)),
        name="hyena_filter",
    )(zemb, w1p, b1, fr, w2, b2, w3, dec2)


def _dft_outer_fwd(zs, ka_ref, o_ref):
    nj = zs.shape[0]
    for p in range(DFT_BLK // DFT_SUB):
        sl = slice(p * DFT_SUB, (p + 1) * DFT_SUB)
        xg = zs[:, sl, :].reshape(nj * DFT_SUB, LANES).astype(BF16)
        c = jnp.dot(ka_ref[...], xg, preferred_element_type=F32)
        o_ref[0, :, :, sl, :] = c.reshape(2, DFT_NBLK, DFT_SUB, LANES).astype(BF16)


def _hy_fwd_data_kernel(x1_ref, v_ref, cw_ref, cb_ref, ka_ref, o_ref, x1pad, vpad, zs, *, seq):
    _fill_padded(x1pad, x1_ref, seq)
    _fill_padded(vpad, v_ref, seq)
    cw = cw_ref[0]
    cb = cb_ref[0]
    for j in range(seq // DFT_BLK):
        r0 = j * DFT_BLK
        x1 = _dwconv(x1pad, r0, DFT_BLK, cw[:, :LANES], cb[:, :LANES], 1)
        v = _dwconv(vpad, r0, DFT_BLK, cw[:, LANES:], cb[:, LANES:], 1)
        zs[j] = v * x1
    _dft_outer_fwd(zs, ka_ref, o_ref)


def _hy_fwd_filt_kernel(f_ref, ka_ref, o_ref):
    _dft_outer_fwd(f_ref.at[0], ka_ref, o_ref)


def _hy_fwd_data(pc, cw2, cb2, ka, batch, seq):
    nh = D_C // LANES
    nj = seq // DFT_BLK
    kern = functools.partial(_hy_fwd_data_kernel, seq=seq)
    return pl.pallas_call(
        kern,
        grid=(batch, nh),
        in_specs=[
            pl.BlockSpec((seq, LANES), lambda b, c: (b, nh + c)),
            pl.BlockSpec((seq, LANES), lambda b, c: (b, 2 * nh + c)),
            pl.BlockSpec((1, 3, 2 * LANES), lambda b, c: (c, 0, 0)),
            pl.BlockSpec((1, 1, 2 * LANES), lambda b, c: (c, 0, 0)),
            pl.BlockSpec(ka.shape, lambda b, c: (0, 0)),
        ],
        out_specs=pl.BlockSpec((1, 2, DFT_NBLK, DFT_BLK, LANES), lambda b, c: (b, 0, 0, 0, c)),
        out_shape=jax.ShapeDtypeStruct((batch, 2, DFT_NBLK, DFT_BLK, D_C), BF16),
        scratch_shapes=[pltpu.VMEM((seq + 2 * SUBLANES, LANES), F32),
                        pltpu.VMEM((seq + 2 * SUBLANES, LANES), F32),
                        pltpu.VMEM((nj, DFT_BLK, LANES), F32)],
        compiler_params=_cparams(("parallel", "parallel")),
        name="hyena_dft_outer",
    )(pc, pc, cw2, cb2, ka)


def _hy_fwd_filt(filt4, ka):
    nhalf, nj = filt4.shape[0], filt4.shape[1]
    nh = D_C // LANES
    return pl.pallas_call(
        _hy_fwd_filt_kernel,
        grid=(nhalf, nh),
        in_specs=[
            pl.BlockSpec((1, nj, DFT_BLK, LANES), lambda b, c: (b, 0, 0, c)),
            pl.BlockSpec(ka.shape, lambda b, c: (0, 0)),
        ],
        out_specs=pl.BlockSpec((1, 2, DFT_NBLK, DFT_BLK, LANES), lambda b, c: (b, 0, 0, 0, c)),
        out_shape=jax.ShapeDtypeStruct((nhalf, 2, DFT_NBLK, DFT_BLK, D_C), BF16),
        compiler_params=_cparams(("parallel", "parallel")),
        name="hyena_dft_outer_filter",
    )(filt4, ka)


def _hy_inner_kernel(c_ref, f_ref, g_ref, gi_ref, o_ref, *, batch):
    kk = pl.program_id(0)
    g = g_ref[0]
    gi = gi_ref[0]
    sign = jnp.where(kk % 2 == 0, 1.0, -1.0).astype(F32)
    fsum = f_ref[0].astype(F32) + sign * f_ref[1].astype(F32)
    fsum = fsum.reshape(2 * DFT_BLK, D_C)
    hs = jnp.dot(g, fsum.astype(BF16), preferred_element_type=F32)
    hre, him = hs[:DFT_BLK], hs[DFT_BLK:]
    for b in range(batch):
        x = jnp.dot(g, c_ref[b].reshape(2 * DFT_BLK, D_C), preferred_element_type=F32)
        xre, xim = x[:DFT_BLK], x[DFT_BLK:]
        y = jnp.concatenate([xre * hre - xim * him, xre * him + xim * hre], axis=0)
        d = jnp.dot(gi, y.astype(BF16), preferred_element_type=F32)
        o_ref[b] = d.reshape(2, 1, DFT_BLK, D_C).astype(BF16)


def _hy_inner(cdata, cfilt, g, gi):
    batch = cdata.shape[0]
    kern = functools.partial(_hy_inner_kernel, batch=batch)
    blk = lambda nb: pl.BlockSpec((nb, 2, 1, DFT_BLK, D_C), lambda k: (0, 0, k, 0, 0))
    return pl.pallas_call(
        kern,
        grid=(DFT_NBLK,),
        in_specs=[blk(batch), blk(cfilt.shape[0]),
                  pl.BlockSpec((1, 2 * DFT_BLK, 2 * DFT_BLK), lambda k: (k, 0, 0)),
                  pl.BlockSpec((1, 2 * DFT_BLK, 2 * DFT_BLK), lambda k: (k, 0, 0))],
        out_specs=blk(batch),
        out_shape=jax.ShapeDtypeStruct(cdata.shape, BF16),
        compiler_params=_cparams(("parallel",)),
        name="hyena_dft_inner",
    )(cdata, cfilt, g, gi)


def _hy_out_kernel(d_ref, x0_ref, x1_ref, v_ref, cw_ref, cb_ref, hb_ref, kai_ref, o_ref,
                   x0pad, x1pad, vpad, ys, *, seq):
    nj = seq // DFT_BLK
    for p in range(DFT_BLK // DFT_SUB):
        sl = slice(p * DFT_SUB, (p + 1) * DFT_SUB)
        rhs = d_ref[0, :, :, sl, :].reshape(2 * DFT_NBLK * DFT_SUB, LANES)
        y = jnp.dot(kai_ref[...], rhs, preferred_element_type=F32)
        ys[:, sl, :] = y.reshape(nj, DFT_SUB, LANES)
    _fill_padded(x0pad, x0_ref, seq)
    _fill_padded(x1pad, x1_ref, seq)
    _fill_padded(vpad, v_ref, seq)
    cw = cw_ref[0]
    cb = cb_ref[0]
    hb = hb_ref[0]
    for j in range(nj):
        r0 = j * DFT_BLK
        x0 = _dwconv(x0pad, r0, DFT_BLK, cw[:, :LANES], cb[:, :LANES], 1)
        x1 = _dwconv(x1pad, r0, DFT_BLK, cw[:, LANES:2 * LANES], cb[:, LANES:2 * LANES], 1)
        v = _dwconv(vpad, r0, DFT_BLK, cw[:, 2 * LANES:], cb[:, 2 * LANES:], 1)
        z = v * x1
        o_ref[r0:r0 + DFT_BLK, :] = (ys[j] + z * hb) * x0


def _hy_out(dd, pc, cw3, cb3, hb, kai, batch, seq):
    n_tok = pc.shape[0]
    nh = D_C // LANES
    nj = seq // DFT_BLK
    kern = functools.partial(_hy_out_kernel, seq=seq)
    return pl.pallas_call(
        kern,
        grid=(batch, nh),
        in_specs=[
            pl.BlockSpec((1, 2, DFT_NBLK, DFT_BLK, LANES), lambda b, c: (b, 0, 0, 0, c)),
            pl.BlockSpec((seq, LANES), lambda b, c: (b, c)),
            pl.BlockSpec((seq, LANES), lambda b, c: (b, nh + c)),
            pl.BlockSpec((seq, LANES), lambda b, c: (b, 2 * nh + c)),
            pl.BlockSpec((1, 3, 3 * LANES), lambda b, c: (c, 0, 0)),
            pl.BlockSpec((1, 1, 3 * LANES), lambda b, c: (c, 0, 0)),
            pl.BlockSpec((1, 1, LANES), lambda b, c: (c, 0, 0)),
            pl.BlockSpec(kai.shape, lambda b, c: (0, 0)),
        ],
        out_specs=pl.BlockSpec((seq, LANES), lambda b, c: (b, c)),
        out_shape=jax.ShapeDtypeStruct((n_tok, D_C), F32),
        scratch_shapes=[pltpu.VMEM((seq + 2 * SUBLANES, LANES), F32) for _ in range(3)]
        + [pltpu.VMEM((nj, DFT_BLK, LANES), F32)],
        compiler_params=_cparams(("parallel", "parallel")),
        name="hyena_out",
    )(dd, pc, pc, pc, cw3, cb3, hb, kai)


def _out_proj_kernel(ya_ref, yb_ref, yc_ref, x_ref, ga_ref, gb_ref, gc_ref, w_ref, o_ref):
    y = jnp.concatenate([_rms(ya_ref[...], ga_ref[...]), _rms(yb_ref[...], gb_ref[...]),
                         _rms(yc_ref[...], gc_ref[...])], axis=-1).astype(BF16)
    o_ref[...] = x_ref[...] + jnp.dot(y, w_ref[...], preferred_element_type=F32)


def _out_proj(ya, yb, yc, x2, ga, gb, gc, w_bf, tm):
    n_tok = x2.shape[0]
    row = lambda w: pl.BlockSpec((tm, w), lambda i: (i, 0))
    vec = lambda w: pl.BlockSpec((1, w), lambda i: (0, 0))
    return pl.pallas_call(
        _out_proj_kernel,
        grid=(n_tok // tm,),
        in_specs=[row(D_A), row(D_B), row(D_C), row(D_MODEL), vec(D_A), vec(D_B), vec(D_C),
                  pl.BlockSpec((D_MODEL, D_MODEL), lambda i: (0, 0))],
        out_specs=row(D_MODEL),
        out_shape=jax.ShapeDtypeStruct((n_tok, D_MODEL), F32),
        compiler_params=_cparams(("parallel",)),
        name="out_proj",
    )(ya, yb, yc, x2, ga, gb, gc, w_bf)


def _mlp_kernel(x_ref, g_ref, wu_ref, wd_ref, gf_ref, o_ref, *, ff_chunk, final_norm):
    x = x_ref[...]
    h = _rms(x, g_ref[...]).astype(BF16)
    acc = x
    for c in range(D_FF // ff_chunk):
        sl = slice(c * ff_chunk, (c + 1) * ff_chunk)
        u = jnp.maximum(jnp.dot(h, wu_ref[:, sl], preferred_element_type=F32), 0.0)
        acc = acc + jnp.dot((u * u).astype(BF16), wd_ref[sl, :], preferred_element_type=F32)
    if final_norm:
        acc = _rms(acc, gf_ref[...])
    o_ref[...] = acc


def _mlp(x2, g, wu_bf, wd_bf, gf, tm, final_norm, ff_chunk=1024):
    n_tok = x2.shape[0]
    kern = functools.partial(_mlp_kernel, ff_chunk=ff_chunk, final_norm=final_norm)
    return pl.pallas_call(
        kern,
        grid=(n_tok // tm,),
        in_specs=[pl.BlockSpec((tm, D_MODEL), lambda i: (i, 0)),
                  pl.BlockSpec((1, D_MODEL), lambda i: (0, 0)),
                  pl.BlockSpec((D_MODEL, D_FF), lambda i: (0, 0)),
                  pl.BlockSpec((D_FF, D_MODEL), lambda i: (0, 0)),
                  pl.BlockSpec((1, D_MODEL), lambda i: (0, 0))],
        out_specs=pl.BlockSpec((tm, D_MODEL), lambda i: (i, 0)),
        out_shape=jax.ShapeDtypeStruct((n_tok, D_MODEL), F32),
        compiler_params=_cparams(("parallel",)),
        name="mlp",
    )(x2, g, wu_bf, wd_bf, gf)


def _rope_tables(seq):
    pos = np.arange(seq, dtype=np.float32)
    inv_freq = (np.float32(ROPE_THETA) ** (-np.arange(0, ROT_DIM, 2, dtype=np.float32) / ROT_DIM))
    ang = (pos[:, None] * inv_freq[None, :]).astype(np.float32)
    cos, sin = np.cos(ang).astype(np.float32), np.sin(ang).astype(np.float32)
    rest = HEAD_DIM - ROT_DIM
    c_head = np.concatenate([cos, cos, np.ones((seq, rest), np.float32)], axis=1)
    s_head = np.concatenate([-sin, sin, np.zeros((seq, rest), np.float32)], axis=1)
    scale = np.float32(HEAD_DIM ** -0.5)
    c = np.concatenate([np.tile(c_head, (1, N_Q_HEADS)) * scale, np.tile(c_head, (1, N_KV_HEADS))], axis=1)
    s = np.concatenate([np.tile(s_head, (1, N_Q_HEADS)) * scale, np.tile(s_head, (1, N_KV_HEADS))], axis=1)
    return jnp.asarray(c), jnp.asarray(s)


def _hyena_position_tables(seq):
    t = np.linspace(0.0, 1.0, seq, dtype=np.float32)[:, None]
    w = (2.0 * math.pi * np.arange(seq, dtype=np.float32)[:, None] / seq).astype(np.float32)
    f = np.linspace(1e-4, HY_BANDS - 1, HY_BANDS, dtype=np.float32)[None, :]
    fw = (f * w).astype(np.float32)
    z = np.concatenate([t, np.cos(fw), -np.sin(fw)], axis=-1).astype(np.float32)
    deltas = np.abs(np.linspace(HY_MIN_DECAY, HY_MAX_DECAY, D_C, dtype=np.float32))
    decay = np.exp(-t * deltas[None, :]).astype(np.float32)
    rev = np.concatenate([np.zeros(1, np.int64), np.arange(seq - 1, 0, -1)])
    z2 = np.concatenate([z, z[rev]], axis=0)
    d2 = np.concatenate([decay, decay[rev]], axis=0)
    d2[seq] = 0.0
    zp = np.zeros((2 * seq, LANES), np.float32)
    zp[:, :HY_EMB] = z2
    return jnp.asarray(zp), jnp.asarray(d2)


def _dft_tables(seq):
    n = 2 * seq
    nj = seq // DFT_BLK
    kk = np.arange(DFT_NBLK)
    ang = 2.0 * np.pi * np.outer(kk, np.arange(nj)) / DFT_NBLK
    eye = np.eye(DFT_SUB)
    ka = np.concatenate([np.kron(np.cos(ang), eye), np.kron(-np.sin(ang), eye)], axis=0)
    kai = np.concatenate([np.kron(np.cos(ang).T, eye), np.kron(-np.sin(ang).T, eye)], axis=1) / n
    m = np.arange(DFT_BLK)
    k = kk[:, None, None] + DFT_NBLK * np.arange(DFT_BLK)[None, :, None]
    ph = 2.0 * np.pi * ((k * m[None, None, :]) % n) / n
    gre, gim = np.cos(ph), -np.sin(ph)
    g = np.concatenate([np.concatenate([gre, -gim], axis=2), np.concatenate([gim, gre], axis=2)], axis=1)
    gi = np.transpose(g, (0, 2, 1))
    as_bf = lambda a: jnp.asarray(a.astype(np.float32)).astype(BF16)
    return as_bf(ka), as_bf(kai), as_bf(g), as_bf(gi)


def _lru_blockdiag(w):
    nb = w.shape[1] // 2
    w = w.reshape(2, nb, 2, w.shape[2], w.shape[3])
    z = jnp.zeros_like(w[:, :, 0])
    top = jnp.concatenate([w[:, :, 0], z], axis=-1)
    bot = jnp.concatenate([z, w[:, :, 1]], axis=-1)
    return jnp.concatenate([top, bot], axis=-2)


def kernel(x, norm_mix_g, w_in, conv_a_w, conv_a_b, lru_wa, lru_ba, lru_wx, lru_bx, lru_lambda,
           attn_sink, hy_conv_w, hy_conv_b, hy_w1, hy_b1, hy_freq, hy_w2, hy_b2, hy_w3, hy_bias,
           gnorm_a, gnorm_b, gnorm_c, w_out, norm_mlp_g, w_up, w_down, final_norm_g):
    batch, seq, _ = x.shape
    depth = w_in.shape[0]
    n_tok = batch * seq
    tm = 512
    ng = D_A // LANES
    nh = D_C // LANES

    rc, rs = _rope_tables(seq)
    zemb, dec2 = _hyena_position_tables(seq)
    ka, kai, gtab, gitab = _dft_tables(seq)

    xs = x.reshape(n_tok, D_MODEL)
    for i in range(depth):
        pa, pq, pc = _in_proj(xs, norm_mix_g[i][None], w_in[i].astype(BF16), rc, rs, seq, tm)

        wa, wx = _lru_blockdiag(lru_wa[i]), _lru_blockdiag(lru_wx[i])
        wg = jnp.concatenate([wa[0], wx[0], wa[1], wx[1]], axis=-1)
        tile = lambda v: v.reshape(ng, 1, LANES)
        bias = jnp.concatenate([tile(lru_ba[i][0]), tile(lru_bx[i][0]),
                                tile(lru_ba[i][1]), tile(lru_bx[i][1])], axis=-1)
        lam = jnp.concatenate([tile(lru_lambda[i][0]), tile(lru_lambda[i][1])], axis=-1)
        y_a = _lru(pa, conv_a_w[i], conv_a_b[i][None], wg, bias, lam, batch, seq)

        y_b = _attention(pq, attn_sink[i], batch, seq)

        w1p = jnp.zeros((LANES, HY_WIDTH), F32).at[:HY_EMB].set(hy_w1[i])
        filt = _hy_filter(zemb, w1p, hy_b1[i][None], hy_freq[i][None], hy_w2[i], hy_b2[i][None],
                          hy_w3[i], dec2, seq)
        cw = hy_conv_w[i].reshape(3, 3, nh, LANES)
        cw3 = jnp.transpose(cw, (2, 0, 1, 3)).reshape(nh, 3, 3 * LANES)
        cb3 = jnp.transpose(hy_conv_b[i].reshape(3, nh, LANES), (1, 0, 2)).reshape(nh, 1, 3 * LANES)
        cdata = _hy_fwd_data(pc, cw3[:, :, LANES:], cb3[:, :, LANES:], ka, batch, seq)
        cfilt = _hy_fwd_filt(filt.reshape(2, seq // DFT_BLK, DFT_BLK, D_C), ka)
        dd = _hy_inner(cdata, cfilt, gtab, gitab)
        y_c = _hy_out(dd, pc, cw3, cb3, hy_bias[i].reshape(nh, 1, LANES), kai, batch, seq)

        xs = _out_proj(y_a, y_b, y_c, xs, gnorm_a[i][None], gnorm_b[i][None], gnorm_c[i][None],
                       w_out[i].astype(BF16), tm)
        xs = _mlp(xs, norm_mlp_g[i][None], w_up[i].astype(BF16), w_down[i].astype(BF16),
                  final_norm_g[None], tm, final_norm=(i == depth - 1))
    return xs.reshape(batch, seq, D_MODEL)
```

```python
import functools
import math

import numpy as np
import jax
import jax.numpy as jnp
from jax import lax
from jax.experimental import pallas as pl
from jax.experimental.pallas import tpu as pltpu

F32 = jnp.float32
BF16 = jnp.bfloat16

D_MODEL = 1024
D_A = 384
D_B = 384
D_C = 256
HEAD_DIM = 64
N_Q_HEADS = 6
N_KV_HEADS = 2
GROUP = N_Q_HEADS // N_KV_HEADS
D_KV = N_KV_HEADS * HEAD_DIM
D_QKV = D_B + 2 * D_KV
D_IN = 2 * D_A + D_QKV + 3 * D_C
C_LRU = 8.0
WINDOW = 128
BLOCK = 128
ROPE_THETA = 500000.0
ROT_DIM = HEAD_DIM // 4
HY_EMB = 33
HY_BANDS = (HY_EMB - 1) // 2
HY_WIDTH = 64
HY_TARGET = 1e-2
HY_MAX_DECAY = math.log(HY_TARGET) / 0.3
HY_MIN_DECAY = math.log(HY_TARGET) / 1.5
D_FF = 4 * D_MODEL
EPS = 1e-6
NEG = -1e30

LANES = 128
SUBLANES = 8
VMEM_LIMIT = 56 * 1024 * 1024

SCAN_SEGS = SUBLANES
DFT_BLK = 128
DFT_NBLK = 64
DFT_SUB = 16


def _cparams(sem):
    return pltpu.CompilerParams(dimension_semantics=sem, vmem_limit_bytes=VMEM_LIMIT)


def _rms(x, g):
    return x * lax.rsqrt(jnp.mean(x * x, axis=-1, keepdims=True) + EPS) * g


def _in_proj_kernel(x_ref, g_ref, w_ref, rc_ref, rs_ref, oa_ref, oq_ref, oc_ref):
    h = _rms(x_ref[...], g_ref[...]).astype(BF16)
    oa_ref[...] = jnp.dot(h, w_ref[:, :2 * D_A], preferred_element_type=F32)
    oc_ref[...] = jnp.dot(h, w_ref[:, 2 * D_A + D_QKV:], preferred_element_type=F32)
    qkv = jnp.dot(h, w_ref[:, 2 * D_A:2 * D_A + D_QKV], preferred_element_type=F32)
    qk = qkv[:, :D_B + D_KV]
    n = D_B + D_KV
    half = ROT_DIM // 2
    lane = lax.broadcasted_iota(jnp.int32, qk.shape, 1) % HEAD_DIM
    swapped = jnp.where(lane < half, pltpu.roll(qk, n - half, axis=1), pltpu.roll(qk, half, axis=1))
    oq_ref[:, :n] = qk * rc_ref[...] + swapped * rs_ref[...]
    oq_ref[:, n:] = qkv[:, n:]


def _in_proj(x2, g, w_bf, rc, rs, seq, tm):
    n_tok = x2.shape[0]
    nrb = seq // tm
    return pl.pallas_call(
        _in_proj_kernel,
        grid=(n_tok // tm,),
        in_specs=[
            pl.BlockSpec((tm, D_MODEL), lambda i: (i, 0)),
            pl.BlockSpec((1, D_MODEL), lambda i: (0, 0)),
            pl.BlockSpec((D_MODEL, D_IN), lambda i: (0, 0)),
            pl.BlockSpec((tm, D_B + D_KV), lambda i: (i % nrb, 0)),
            pl.BlockSpec((tm, D_B + D_KV), lambda i: (i % nrb, 0)),
        ],
        out_specs=[
            pl.BlockSpec((tm, 2 * D_A), lambda i: (i, 0)),
            pl.BlockSpec((tm, D_QKV), lambda i: (i, 0)),
            pl.BlockSpec((tm, 3 * D_C), lambda i: (i, 0)),
        ],
        out_shape=[
            jax.ShapeDtypeStruct((n_tok, 2 * D_A), F32),
            jax.ShapeDtypeStruct((n_tok, D_QKV), F32),
            jax.ShapeDtypeStruct((n_tok, 3 * D_C), F32),
        ],
        compiler_params=_cparams(("parallel",)),
        name="in_proj",
    )(x2, g, w_bf, rc, rs)


def _dwconv(pad_ref, r0, rows, w, b, pad_left):
    win = pad_ref[pl.ds(r0, rows + 2 * SUBLANES), :]
    total = rows + 2 * SUBLANES
    acc = None
    for k in range(w.shape[0]):
        shift = (pad_left - k) % total
        tap = win if shift == 0 else pltpu.roll(win, shift, axis=0)
        term = tap[SUBLANES:SUBLANES + rows] * w[k:k + 1, :]
        acc = term if acc is None else acc + term
    return acc + b


def _fill_padded(pad_ref, src_ref, seq):
    zeros = jnp.zeros((SUBLANES, pad_ref.shape[1]), F32)
    pad_ref[0:SUBLANES, :] = zeros
    pad_ref[seq + SUBLANES:seq + 2 * SUBLANES, :] = zeros
    pad_ref[SUBLANES:seq + SUBLANES, :] = src_ref[...]


def _softplus(x):
    return jnp.maximum(x, 0.0) + jnp.log1p(jnp.exp(-jnp.abs(x)))


def _gelu_tanh(x):
    c = math.sqrt(2.0 / math.pi)
    return x * (0.5 * (1.0 + jnp.tanh(c * (x + 0.044715 * (x * x * x)))))


def _lru_kernel(u_ref, gate_ref, cw_ref, cb_ref, w_ref, bias_ref, lam_ref, o_ref,
                upad, xci, gg, af, bf, ar, br, pfs, hfs, prs, hrs, *, seq, rows, unroll):
    seg = seq // SCAN_SEGS
    tiles = rows // SCAN_SEGS
    nchunk = seq // rows
    per_seg = seg // rows
    _fill_padded(upad, u_ref, seq)
    cw = cw_ref[...]
    cb = cb_ref[...]
    w = w_ref[0] * 0.5
    bias = bias_ref[0] * 0.5
    hnsp = (-0.5 * C_LRU) * _softplus(-lam_ref[0])

    def conv(ci, carry):
        r0 = pl.multiple_of(ci * rows, rows)
        dst = pl.ds((ci % per_seg) * (rows * SCAN_SEGS) + ci // per_seg, rows, stride=SCAN_SEGS)
        xci[dst, :] = _dwconv(upad, r0, rows, cw, cb, 2)
        gg[dst, :] = _gelu_tanh(gate_ref[pl.ds(r0, rows), :])
        return carry

    lax.fori_loop(0, nchunk, conv, 0)

    def gates(ci, carry):
        r0 = pl.multiple_of(ci * rows, rows)
        xc = xci[pl.ds(r0, rows), :]
        t = jnp.tanh(jnp.dot(xc.astype(BF16), w, preferred_element_type=F32) + bias)
        hxc = 0.5 * xc
        for d, (a_ref, b_ref) in enumerate(((af, bf), (ar, br))):
            tr = t[:, (2 * d) * LANES:(2 * d + 1) * LANES]
            ti = t[:, (2 * d + 1) * LANES:(2 * d + 2) * LANES]
            hn = hnsp[:, d * LANES:(d + 1) * LANES]
            log_a = tr * hn + hn
            a = jnp.exp(log_a)
            nem = (-1.0 - a * a) * jnp.tanh(log_a)
            a_ref[pl.ds(r0, rows), :] = a
            b_ref[pl.ds(r0, rows), :] = jnp.sqrt(nem) * (ti * hxc + hxc)
        return carry

    lax.fori_loop(0, nchunk, gates, 0, unroll=2)

    def scan(it, carry):
        pf, hf, pr, hr = carry
        for u in range(unroll):
            i = it * unroll + u
            rf = pl.multiple_of(i * SCAN_SEGS, SCAN_SEGS)
            a = af[pl.ds(rf, SCAN_SEGS), :]
            pf = a * pf
            hf = a * hf + bf[pl.ds(rf, SCAN_SEGS), :]
            pfs[pl.ds(rf, SCAN_SEGS), :] = pf
            hfs[pl.ds(rf, SCAN_SEGS), :] = hf
            rr = pl.multiple_of((seg - 1 - i) * SCAN_SEGS, SCAN_SEGS)
            a = ar[pl.ds(rr, SCAN_SEGS), :]
            pr = a * pr
            hr = a * hr + br[pl.ds(rr, SCAN_SEGS), :]
            prs[pl.ds(rr, SCAN_SEGS), :] = pr
            hrs[pl.ds(rr, SCAN_SEGS), :] = hr
        return pf, hf, pr, hr

    one = jnp.ones((SCAN_SEGS, LANES), F32)
    zero = jnp.zeros((SCAN_SEGS, LANES), F32)
    pf, hf, pr, hr = lax.fori_loop(0, seg // unroll, scan, (one, zero, one, zero))

    sub = lax.broadcasted_iota(jnp.int32, (SCAN_SEGS, LANES), 0)
    cf = zero
    cr = zero
    for _ in range(SCAN_SEGS - 1):
        cf = jnp.where(sub == 0, 0.0, pltpu.roll(hf + pf * cf, 1, axis=0))
        cr = jnp.where(sub == SCAN_SEGS - 1, 0.0, pltpu.roll(hr + pr * cr, SCAN_SEGS - 1, axis=0))

    def combine(ci, carry):
        r0 = pl.multiple_of(ci * rows, rows)
        sl = pl.ds(r0, rows)
        tile3 = lambda ref: ref[sl, :].reshape(tiles, SCAN_SEGS, LANES)
        h = (tile3(hfs) + tile3(pfs) * cf[None]) + (tile3(hrs) + tile3(prs) * cr[None])
        out = h.reshape(rows, LANES) * gg[sl, :]
        for k in range(tiles):
            o_ref[pl.ds(ci * tiles + k, SCAN_SEGS, stride=seg), :] = (
                out[k * SCAN_SEGS:(k + 1) * SCAN_SEGS])
        return carry

    lax.fori_loop(0, nchunk, combine, 0)


def _lru(pa, cw, cb, wg, bias, lam, batch, seq, rows=128, unroll=8):
    n_tok = pa.shape[0]
    ng = D_A // LANES
    kern = functools.partial(_lru_kernel, seq=seq, rows=rows, unroll=unroll)
    return pl.pallas_call(
        kern,
        grid=(batch, ng),
        in_specs=[
            pl.BlockSpec((seq, LANES), lambda b, g: (b, g)),
            pl.BlockSpec((seq, LANES), lambda b, g: (b, ng + g)),
            pl.BlockSpec((cw.shape[0], LANES), lambda b, g: (0, g)),
            pl.BlockSpec((1, LANES), lambda b, g: (0, g)),
            pl.BlockSpec((1, LANES, 4 * LANES), lambda b, g: (g, 0, 0)),
            pl.BlockSpec((1, 1, 4 * LANES), lambda b, g: (g, 0, 0)),
            pl.BlockSpec((1, 1, 2 * LANES), lambda b, g: (g, 0, 0)),
        ],
        out_specs=pl.BlockSpec((seq, LANES), lambda b, g: (b, g)),
        out_shape=jax.ShapeDtypeStruct((n_tok, D_A), F32),
        scratch_shapes=[pltpu.VMEM((seq + 2 * SUBLANES, LANES), F32)]
        + [pltpu.VMEM((seq, LANES), F32) for _ in range(10)],
        compiler_params=_cparams(("parallel", "parallel")),
        name="rglru",
    )(pa, pa, cw, cb, wg, bias, lam)


def _attn_kernel(sink_ref, q_ref, k_ref, v_ref, o_ref, *, seq, tq):
    band = 3 * BLOCK
    nsub = tq // BLOCK
    i = pl.program_id(1)
    qi = lax.broadcasted_iota(jnp.int32, (BLOCK, band), 0)
    ki = lax.broadcasted_iota(jnp.int32, (BLOCK, band), 1)
    for jb in range(nsub):
        q0 = (i * nsub + jb) * BLOCK
        k0 = pl.multiple_of(jnp.clip(q0 - BLOCK, 0, seq - band), BLOCK)
        valid = jnp.abs((q0 + qi) - (k0 + ki)) <= WINDOW
        qb = q_ref[jb * BLOCK:(jb + 1) * BLOCK, :]
        kb = k_ref[pl.ds(k0, band), :].astype(BF16)
        vb = v_ref[pl.ds(k0, band), :].astype(BF16)
        outs = []
        for h in range(N_Q_HEADS):
            kv = h // GROUP
            qh = qb[:, h * HEAD_DIM:(h + 1) * HEAD_DIM].astype(BF16)
            kh = kb[:, kv * HEAD_DIM:(kv + 1) * HEAD_DIM]
            vh = vb[:, kv * HEAD_DIM:(kv + 1) * HEAD_DIM]
            s = lax.dot_general(qh, kh, (((1,), (1,)), ((), ())), preferred_element_type=F32)
            s = jnp.where(valid, s, NEG)
            sk = sink_ref[h]
            m = jnp.maximum(jnp.max(s, axis=-1, keepdims=True), sk)
            p = jnp.exp(s - m)
            denom = jnp.sum(p, axis=-1, keepdims=True) + jnp.exp(sk - m)
            o = jnp.dot(p.astype(BF16), vh, preferred_element_type=F32)
            outs.append(o / denom)
        o_ref[jb * BLOCK:(jb + 1) * BLOCK, :] = jnp.concatenate(outs, axis=-1)


def _attention(pq, sink, batch, seq, tq=512):
    n_tok = pq.shape[0]
    nq = seq // tq
    kcol = D_B // D_KV
    kern = functools.partial(_attn_kernel, seq=seq, tq=tq)
    return pl.pallas_call(
        kern,
        grid=(batch, nq),
        in_specs=[
            pl.BlockSpec(memory_space=pltpu.SMEM),
            pl.BlockSpec((tq, D_B), lambda b, i: (b * nq + i, 0)),
            pl.BlockSpec((seq, D_KV), lambda b, i: (b, kcol)),
            pl.BlockSpec((seq, D_KV), lambda b, i: (b, kcol + 1)),
        ],
        out_specs=pl.BlockSpec((tq, D_B), lambda b, i: (b * nq + i, 0)),
        out_shape=jax.ShapeDtypeStruct((n_tok, D_B), F32),
        compiler_params=_cparams(("parallel", "arbitrary")),
        name="win_attn",
    )(sink, pq, pq, pq)


def _hy_filter_kernel(z_ref, w1_ref, b1_ref, fr_ref, w2_ref, b2_ref, w3_ref, dec_ref, o_ref, *, seq):
    hi = lax.Precision.HIGHEST
    fr = fr_ref[...]
    h = jnp.sin(fr * (jnp.dot(z_ref[...], w1_ref[...], preferred_element_type=F32, precision=hi)
                      + b1_ref[...]))
    h = jnp.sin(fr * (jnp.dot(h, w2_ref[...], preferred_element_type=F32, precision=hi) + b2_ref[...]))
    f = jnp.dot(h, w3_ref[...], preferred_element_type=F32, precision=hi)
    rows = z_ref.shape[0]
    fwd = pl.program_id(0) * rows < seq
    o_ref[...] = jnp.where(fwd, f[:, :D_C], f[:, D_C:]) * dec_ref[...]


def _hy_filter(zemb, w1p, b1, fr, w2, b2, w3, dec2, seq, rows=1024):
    n = zemb.shape[0]
    kern = functools.partial(_hy_filter_kernel, seq=seq)
    full = lambda a: pl.BlockSpec(a.shape, lambda i: (0,) * a.ndim)
    return pl.pallas_call(
        kern,
        grid=(n // rows,),
        in_specs=[pl.BlockSpec((rows, zemb.shape[1]), lambda i: (i, 0)),
                  full(w1p), full(b1), full(fr), full(w2), full(b2), full(w3),
                  pl.BlockSpec((rows, D_C), lambda i: (i, 0))],
        out_specs=pl.BlockSpec((rows, D_C), lambda i: (i, 0)),
        out_shape=jax.ShapeDtypeStruct((n, D_C), F32),
        compiler_params=_cparams(("parallel",)),
        name="hyena_filter",
    )(zemb, w1p, b1, fr, w2, b2, w3, dec2)


def _dft_outer_fwd(zs, ka_ref, o_ref):
    nj = zs.shape[0]
    for p in range(DFT_BLK // DFT_SUB):
        sl = slice(p * DFT_SUB, (p + 1) * DFT_SUB)
        xg = zs[:, sl, :].reshape(nj * DFT_SUB, LANES).astype(BF16)
        c = jnp.dot(ka_ref[...], xg, preferred_element_type=F32)
        o_ref[0, :, :, sl, :] = c.reshape(2, DFT_NBLK, DFT_SUB, LANES).astype(BF16)


def _hy_fwd_data_kernel(x1_ref, v_ref, cw_ref, cb_ref, ka_ref, o_ref, x1pad, vpad, zs, *, seq):
    _fill_padded(x1pad, x1_ref, seq)
    _fill_padded(vpad, v_ref, seq)
    cw = cw_ref[0]
    cb = cb_ref[0]
    for j in range(seq // DFT_BLK):
        r0 = j * DFT_BLK
        x1 = _dwconv(x1pad, r0, DFT_BLK, cw[:, :LANES], cb[:, :LANES], 1)
        v = _dwconv(vpad, r0, DFT_BLK, cw[:, LANES:], cb[:, LANES:], 1)
        zs[j] = v * x1
    _dft_outer_fwd(zs, ka_ref, o_ref)


def _hy_fwd_filt_kernel(f_ref, ka_ref, o_ref):
    _dft_outer_fwd(f_ref.at[0], ka_ref, o_ref)


def _hy_fwd_data(pc, cw2, cb2, ka, batch, seq):
    nh = D_C // LANES
    nj = seq // DFT_BLK
    kern = functools.partial(_hy_fwd_data_kernel, seq=seq)
    return pl.pallas_call(
        kern,
        grid=(batch, nh),
        in_specs=[
            pl.BlockSpec((seq, LANES), lambda b, c: (b, nh + c)),
            pl.BlockSpec((seq, LANES), lambda b, c: (b, 2 * nh + c)),
            pl.BlockSpec((1, 3, 2 * LANES), lambda b, c: (c, 0, 0)),
            pl.BlockSpec((1, 1, 2 * LANES), lambda b, c: (c, 0, 0)),
            pl.BlockSpec(ka.shape, lambda b, c: (0, 0)),
        ],
        out_specs=pl.BlockSpec((1, 2, DFT_NBLK, DFT_BLK, LANES), lambda b, c: (b, 0, 0, 0, c)),
        out_shape=jax.ShapeDtypeStruct((batch, 2, DFT_NBLK, DFT_BLK, D_C), BF16),
        scratch_shapes=[pltpu.VMEM((seq + 2 * SUBLANES, LANES), F32),
                        pltpu.VMEM((seq + 2 * SUBLANES, LANES), F32),
                        pltpu.VMEM((nj, DFT_BLK, LANES), F32)],
        compiler_params=_cparams(("parallel", "parallel")),
        name="hyena_dft_outer",
    )(pc, pc, cw2, cb2, ka)


def _hy_fwd_filt(filt4, ka):
    nhalf, nj = filt4.shape[0], filt4.shape[1]
    nh = D_C // LANES
    return pl.pallas_call(
        _hy_fwd_filt_kernel,
        grid=(nhalf, nh),
        in_specs=[
            pl.BlockSpec((1, nj, DFT_BLK, LANES), lambda b, c: (b, 0, 0, c)),
            pl.BlockSpec(ka.shape, lambda b, c: (0, 0)),
        ],
        out_specs=pl.BlockSpec((1, 2, DFT_NBLK, DFT_BLK, LANES), lambda b, c: (b, 0, 0, 0, c)),
        out_shape=jax.ShapeDtypeStruct((nhalf, 2, DFT_NBLK, DFT_BLK, D_C), BF16),
        compiler_params=_cparams(("parallel", "parallel")),
        name="hyena_dft_outer_filter",
    )(filt4, ka)


def _hy_inner_kernel(c_ref, f_ref, g_ref, gi_ref, o_ref, *, batch):
    kk = pl.program_id(0)
    g = g_ref[0]
    gi = gi_ref[0]
    sign = jnp.where(kk % 2 == 0, 1.0, -1.0).astype(F32)
    fsum = f_ref[0].astype(F32) + sign * f_ref[1].astype(F32)
    fsum = fsum.reshape(2 * DFT_BLK, D_C)
    hs = jnp.dot(g, fsum.astype(BF16), preferred_element_type=F32)
    hre, him = hs[:DFT_BLK], hs[DFT_BLK:]
    for b in range(batch):
        x = jnp.dot(g, c_ref[b].reshape(2 * DFT_BLK, D_C), preferred_element_type=F32)
        xre, xim = x[:DFT_BLK], x[DFT_BLK:]
        y = jnp.concatenate([xre * hre - xim * him, xre * him + xim * hre], axis=0)
        d = jnp.dot(gi, y.astype(BF16), preferred_element_type=F32)
        o_ref[b] = d.reshape(2, 1, DFT_BLK, D_C).astype(BF16)


def _hy_inner(cdata, cfilt, g, gi):
    batch = cdata.shape[0]
    kern = functools.partial(_hy_inner_kernel, batch=batch)
    blk = lambda nb: pl.BlockSpec((nb, 2, 1, DFT_BLK, D_C), lambda k: (0, 0, k, 0, 0))
    return pl.pallas_call(
        kern,
        grid=(DFT_NBLK,),
        in_specs=[blk(batch), blk(cfilt.shape[0]),
                  pl.BlockSpec((1, 2 * DFT_BLK, 2 * DFT_BLK), lambda k: (k, 0, 0)),
                  pl.BlockSpec((1, 2 * DFT_BLK, 2 * DFT_BLK), lambda k: (k, 0, 0))],
        out_specs=blk(batch),
        out_shape=jax.ShapeDtypeStruct(cdata.shape, BF16),
        compiler_params=_cparams(("parallel",)),
        name="hyena_dft_inner",
    )(cdata, cfilt, g, gi)


def _hy_out_kernel(d_ref, x0_ref, x1_ref, v_ref, cw_ref, cb_ref, hb_ref, kai_ref, o_ref,
                   x0pad, x1pad, vpad, ys, *, seq):
    nj = seq // DFT_BLK
    for p in range(DFT_BLK // DFT_SUB):
        sl = slice(p * DFT_SUB, (p + 1) * DFT_SUB)
        rhs = d_ref[0, :, :, sl, :].reshape(2 * DFT_NBLK * DFT_SUB, LANES)
        y = jnp.dot(kai_ref[...], rhs, preferred_element_type=F32)
        ys[:, sl, :] = y.reshape(nj, DFT_SUB, LANES)
    _fill_padded(x0pad, x0_ref, seq)
    _fill_padded(x1pad, x1_ref, seq)
    _fill_padded(vpad, v_ref, seq)
    cw = cw_ref[0]
    cb = cb_ref[0]
    hb = hb_ref[0]
    for j in range(nj):
        r0 = j * DFT_BLK
        x0 = _dwconv(x0pad, r0, DFT_BLK, cw[:, :LANES], cb[:, :LANES], 1)
        x1 = _dwconv(x1pad, r0, DFT_BLK, cw[:, LANES:2 * LANES], cb[:, LANES:2 * LANES], 1)
        v = _dwconv(vpad, r0, DFT_BLK, cw[:, 2 * LANES:], cb[:, 2 * LANES:], 1)
        z = v * x1
        o_ref[r0:r0 + DFT_BLK, :] = (ys[j] + z * hb) * x0


def _hy_out(dd, pc, cw3, cb3, hb, kai, batch, seq):
    n_tok = pc.shape[0]
    nh = D_C // LANES
    nj = seq // DFT_BLK
    kern = functools.partial(_hy_out_kernel, seq=seq)
    return pl.pallas_call(
        kern,
        grid=(batch, nh),
        in_specs=[
            pl.BlockSpec((1, 2, DFT_NBLK, DFT_BLK, LANES), lambda b, c: (b, 0, 0, 0, c)),
            pl.BlockSpec((seq, LANES), lambda b, c: (b, c)),
            pl.BlockSpec((seq, LANES), lambda b, c: (b, nh + c)),
            pl.BlockSpec((seq, LANES), lambda b, c: (b, 2 * nh + c)),
            pl.BlockSpec((1, 3, 3 * LANES), lambda b, c: (c, 0, 0)),
            pl.BlockSpec((1, 1, 3 * LANES), lambda b, c: (c, 0, 0)),
            pl.BlockSpec((1, 1, LANES), lambda b, c: (c, 0, 0)),
            pl.BlockSpec(kai.shape, lambda b, c: (0, 0)),
        ],
        out_specs=pl.BlockSpec((seq, LANES), lambda b, c: (b, c)),
        out_shape=jax.ShapeDtypeStruct((n_tok, D_C), F32),
        scratch_shapes=[pltpu.VMEM((seq + 2 * SUBLANES, LANES), F32) for _ in range(3)]
        + [pltpu.VMEM((nj, DFT_BLK, LANES), F32)],
        compiler_params=_cparams(("parallel", "parallel")),
        name="hyena_out",
    )(dd, pc, pc, pc, cw3, cb3, hb, kai)


def _out_proj_kernel(ya_ref, yb_ref, yc_ref, x_ref, ga_ref, gb_ref, gc_ref, w_ref, o_ref):
    y = jnp.concatenate([_rms(ya_ref[...], ga_ref[...]), _rms(yb_ref[...], gb_ref[...]),
                         _rms(yc_ref[...], gc_ref[...])], axis=-1).astype(BF16)
    o_ref[...] = x_ref[...] + jnp.dot(y, w_ref[...], preferred_element_type=F32)


def _out_proj(ya, yb, yc, x2, ga, gb, gc, w_bf, tm):
    n_tok = x2.shape[0]
    row = lambda w: pl.BlockSpec((tm, w), lambda i: (i, 0))
    vec = lambda w: pl.BlockSpec((1, w), lambda i: (0, 0))
    return pl.pallas_call(
        _out_proj_kernel,
        grid=(n_tok // tm,),
        in_specs=[row(D_A), row(D_B), row(D_C), row(D_MODEL), vec(D_A), vec(D_B), vec(D_C),
                  pl.BlockSpec((D_MODEL, D_MODEL), lambda i: (0, 0))],
        out_specs=row(D_MODEL),
        out_shape=jax.ShapeDtypeStruct((n_tok, D_MODEL), F32),
        compiler_params=_cparams(("parallel",)),
        name="out_proj",
    )(ya, yb, yc, x2, ga, gb, gc, w_bf)


def _mlp_kernel(x_ref, g_ref, wu_ref, wd_ref, gf_ref, o_ref, *, ff_chunk, final_norm):
    x = x_ref[...]
    h = _rms(x, g_ref[...]).astype(BF16)
    acc = x
    for c in range(D_FF // ff_chunk):
        sl = slice(c * ff_chunk, (c + 1) * ff_chunk)
        u = jnp.maximum(jnp.dot(h, wu_ref[:, sl], preferred_element_type=F32), 0.0)
        acc = acc + jnp.dot((u * u).astype(BF16), wd_ref[sl, :], preferred_element_type=F32)
    if final_norm:
        acc = _rms(acc, gf_ref[...])
    o_ref[...] = acc


def _mlp(x2, g, wu_bf, wd_bf, gf, tm, final_norm, ff_chunk=1024):
    n_tok = x2.shape[0]
    kern = functools.partial(_mlp_kernel, ff_chunk=ff_chunk, final_norm=final_norm)
    return pl.pallas_call(
        kern,
        grid=(n_tok // tm,),
        in_specs=[pl.BlockSpec((tm, D_MODEL), lambda i: (i, 0)),
                  pl.BlockSpec((1, D_MODEL), lambda i: (0, 0)),
                  pl.BlockSpec((D_MODEL, D_FF), lambda i: (0, 0)),
                  pl.BlockSpec((D_FF, D_MODEL), lambda i: (0, 0)),
                  pl.BlockSpec((1, D_MODEL), lambda i: (0, 0))],
        out_specs=pl.BlockSpec((tm, D_MODEL), lambda i: (i, 0)),
        out_shape=jax.ShapeDtypeStruct((n_tok, D_MODEL), F32),
        compiler_params=_cparams(("parallel",)),
        name="mlp",
    )(x2, g, wu_bf, wd_bf, gf)


def _rope_tables(seq):
    pos = np.arange(seq, dtype=np.float32)
    inv_freq = (np.float32(ROPE_THETA) ** (-np.arange(0, ROT_DIM, 2, dtype=np.float32) / ROT_DIM))
    ang = (pos[:, None] * inv_freq[None, :]).astype(np.float32)
    cos, sin = np.cos(ang).astype(np.float32), np.sin(ang).astype(np.float32)
    rest = HEAD_DIM - ROT_DIM
    c_head = np.concatenate([cos, cos, np.ones((seq, rest), np.float32)], axis=1)
    s_head = np.concatenate([-sin, sin, np.zeros((seq, rest), np.float32)], axis=1)
    scale = np.float32(HEAD_DIM ** -0.5)
    c = np.concatenate([np.tile(c_head, (1, N_Q_HEADS)) * scale, np.tile(c_head, (1, N_KV_HEADS))], axis=1)
    s = np.concatenate([np.tile(s_head, (1, N_Q_HEADS)) * scale, np.tile(s_head, (1, N_KV_HEADS))], axis=1)
    return jnp.asarray(c), jnp.asarray(s)


def _hyena_position_tables(seq):
    t = np.linspace(0.0, 1.0, seq, dtype=np.float32)[:, None]
    w = (2.0 * math.pi * np.arange(seq, dtype=np.float32)[:, None] / seq).astype(np.float32)
    f = np.linspace(1e-4, HY_BANDS - 1, HY_BANDS, dtype=np.float32)[None, :]
    fw = (f * w).astype(np.float32)
    z = np.concatenate([t, np.cos(fw), -np.sin(fw)], axis=-1).astype(np.float32)
    deltas = np.abs(np.linspace(HY_MIN_DECAY, HY_MAX_DECAY, D_C, dtype=np.float32))
    decay = np.exp(-t * deltas[None, :]).astype(np.float32)
    rev = np.concatenate([np.zeros(1, np.int64), np.arange(seq - 1, 0, -1)])
    z2 = np.concatenate([z, z[rev]], axis=0)
    d2 = np.concatenate([decay, decay[rev]], axis=0)
    d2[seq] = 0.0
    zp = np.zeros((2 * seq, LANES), np.float32)
    zp[:, :HY_EMB] = z2
    return jnp.asarray(zp), jnp.asarray(d2)


def _dft_tables(seq):
    n = 2 * seq
    nj = seq // DFT_BLK
    kk = np.arange(DFT_NBLK)
    ang = 2.0 * np.pi * np.outer(kk, np.arange(nj)) / DFT_NBLK
    eye = np.eye(DFT_SUB)
    ka = np.concatenate([np.kron(np.cos(ang), eye), np.kron(-np.sin(ang), eye)], axis=0)
    kai = np.concatenate([np.kron(np.cos(ang).T, eye), np.kron(-np.sin(ang).T, eye)], axis=1) / n
    m = np.arange(DFT_BLK)
    k = kk[:, None, None] + DFT_NBLK * np.arange(DFT_BLK)[None, :, None]
    ph = 2.0 * np.pi * ((k * m[None, None, :]) % n) / n
    gre, gim = np.cos(ph), -np.sin(ph)
    g = np.concatenate([np.concatenate([gre, -gim], axis=2), np.concatenate([gim, gre], axis=2)], axis=1)
    gi = np.transpose(g, (0, 2, 1))
    as_bf = lambda a: jnp.asarray(a.astype(np.float32)).astype(BF16)
    return as_bf(ka), as_bf(kai), as_bf(g), as_bf(gi)


def _lru_blockdiag(w):
    nb = w.shape[1] // 2
    w = w.reshape(2, nb, 2, w.shape[2], w.shape[3])
    z = jnp.zeros_like(w[:, :, 0])
    top = jnp.concatenate([w[:, :, 0], z], axis=-1)
    bot = jnp.concatenate([z, w[:, :, 1]], axis=-1)
    return jnp.concatenate([top, bot], axis=-2)


def kernel(x, norm_mix_g, w_in, conv_a_w, conv_a_b, lru_wa, lru_ba, lru_wx, lru_bx, lru_lambda,
           attn_sink, hy_conv_w, hy_conv_b, hy_w1, hy_b1, hy_freq, hy_w2, hy_b2, hy_w3, hy_bias,
           gnorm_a, gnorm_b, gnorm_c, w_out, norm_mlp_g, w_up, w_down, final_norm_g):
    batch, seq, _ = x.shape
    depth = w_in.shape[0]
    n_tok = batch * seq
    tm = 512
    ng = D_A // LANES
    nh = D_C // LANES

    rc, rs = _rope_tables(seq)
    zemb, dec2 = _hyena_position_tables(seq)
    ka, kai, gtab, gitab = _dft_tables(seq)

    xs = x.reshape(n_tok, D_MODEL)
    for i in range(depth):
        pa, pq, pc = _in_proj(xs, norm_mix_g[i][None], w_in[i].astype(BF16), rc, rs, seq, tm)

        wa, wx = _lru_blockdiag(lru_wa[i]), _lru_blockdiag(lru_wx[i])
        wg = jnp.concatenate([wa[0], wx[0], wa[1], wx[1]], axis=-1).astype(BF16)
        tile = lambda v: v.reshape(ng, 1, LANES)
        bias = jnp.concatenate([tile(lru_ba[i][0]), tile(lru_bx[i][0]),
                                tile(lru_ba[i][1]), tile(lru_bx[i][1])], axis=-1)
        lam = jnp.concatenate([tile(lru_lambda[i][0]), tile(lru_lambda[i][1])], axis=-1)
        y_a = _lru(pa, conv_a_w[i], conv_a_b[i][None], wg, bias, lam, batch, seq)

        y_b = _attention(pq, attn_sink[i], batch, seq)

        w1p = jnp.zeros((LANES, HY_WIDTH), F32).at[:HY_EMB].set(hy_w1[i])
        filt = _hy_filter(zemb, w1p, hy_b1[i][None], hy_freq[i][None], hy_w2[i], hy_b2[i][None],
                          hy_w3[i], dec2, seq)
        cw = hy_conv_w[i].reshape(3, 3, nh, LANES)
        cw3 = jnp.transpose(cw, (2, 0, 1, 3)).reshape(nh, 3, 3 * LANES)
        cb3 = jnp.transpose(hy_conv_b[i].reshape(3, nh, LANES), (1, 0, 2)).reshape(nh, 1, 3 * LANES)
        cdata = _hy_fwd_data(pc, cw3[:, :, LANES:], cb3[:, :, LANES:], ka, batch, seq)
        cfilt = _hy_fwd_filt(filt.reshape(2, seq // DFT_BLK, DFT_BLK, D_C), ka)
        dd = _hy_inner(cdata, cfilt, gtab, gitab)
        y_c = _hy_out(dd, pc, cw3, cb3, hy_bias[i].reshape(nh, 1, LANES), kai, batch, seq)

        xs = _out_proj(y_a, y_b, y_c, xs, gnorm_a[i][None], gnorm_b[i][None], gnorm_c[i][None],
                       w_out[i].astype(BF16), tm)
        xs = _mlp(xs, norm_mlp_g[i][None], w_up[i].astype(BF16), w_down[i].astype(BF16),
                  final_norm_g[None], tm, final_norm=(i == depth - 1))
    return xs.reshape(batch, seq, D_MODEL)
```

```python
import functools
import math

import numpy as np
import jax
import jax.numpy as jnp
from jax import lax
from jax.experimental import pallas as pl
from jax.experimental.pallas import tpu as pltpu

F32 = jnp.float32
BF16 = jnp.bfloat16

D_MODEL = 1024
D_A = 384
D_B = 384
D_C = 256
HEAD_DIM = 64
N_Q_HEADS = 6
N_KV_HEADS = 2
GROUP = N_Q_HEADS // N_KV_HEADS
D_KV = N_KV_HEADS * HEAD_DIM
D_QKV = D_B + 2 * D_KV
D_IN = 2 * D_A + D_QKV + 3 * D_C
C_LRU = 8.0
WINDOW = 128
BLOCK = 128
ROPE_THETA = 500000.0
ROT_DIM = HEAD_DIM // 4
HY_EMB = 33
HY_BANDS = (HY_EMB - 1) // 2
HY_WIDTH = 64
HY_TARGET = 1e-2
HY_MAX_DECAY = math.log(HY_TARGET) / 0.3
HY_MIN_DECAY = math.log(HY_TARGET) / 1.5
D_FF = 4 * D_MODEL
EPS = 1e-6
NEG = -1e30

LANES = 128
SUBLANES = 8
VMEM_LIMIT = 56 * 1024 * 1024

SCAN_SEGS = SUBLANES
DFT_BLK = 128
DFT_NBLK = 64
DFT_SUB = 16


def _cparams(sem):
    return pltpu.CompilerParams(dimension_semantics=sem, vmem_limit_bytes=VMEM_LIMIT)


def _rms(x, g):
    return x * lax.rsqrt(jnp.mean(x * x, axis=-1, keepdims=True) + EPS) * g


def _in_proj_kernel(x_ref, g_ref, w_ref, rc_ref, rs_ref, oa_ref, oq_ref, oc_ref):
    h = _rms(x_ref[...], g_ref[...]).astype(BF16)
    oa_ref[...] = jnp.dot(h, w_ref[:, :2 * D_A], preferred_element_type=F32)
    oc_ref[...] = jnp.dot(h, w_ref[:, 2 * D_A + D_QKV:], preferred_element_type=F32)
    qkv = jnp.dot(h, w_ref[:, 2 * D_A:2 * D_A + D_QKV], preferred_element_type=F32)
    qk = qkv[:, :D_B + D_KV]
    n = D_B + D_KV
    half = ROT_DIM // 2
    lane = lax.broadcasted_iota(jnp.int32, qk.shape, 1) % HEAD_DIM
    swapped = jnp.where(lane < half, pltpu.roll(qk, n - half, axis=1), pltpu.roll(qk, half, axis=1))
    oq_ref[:, :n] = qk * rc_ref[...] + swapped * rs_ref[...]
    oq_ref[:, n:] = qkv[:, n:]


def _in_proj(x2, g, w_bf, rc, rs, seq, tm):
    n_tok = x2.shape[0]
    nrb = seq // tm
    return pl.pallas_call(
        _in_proj_kernel,
        grid=(n_tok // tm,),
        in_specs=[
            pl.BlockSpec((tm, D_MODEL), lambda i: (i, 0)),
            pl.BlockSpec((1, D_MODEL), lambda i: (0, 0)),
            pl.BlockSpec((D_MODEL, D_IN), lambda i: (0, 0)),
            pl.BlockSpec((tm, D_B + D_KV), lambda i: (i % nrb, 0)),
            pl.BlockSpec((tm, D_B + D_KV), lambda i: (i % nrb, 0)),
        ],
        out_specs=[
            pl.BlockSpec((tm, 2 * D_A), lambda i: (i, 0)),
            pl.BlockSpec((tm, D_QKV), lambda i: (i, 0)),
            pl.BlockSpec((tm, 3 * D_C), lambda i: (i, 0)),
        ],
        out_shape=[
            jax.ShapeDtypeStruct((n_tok, 2 * D_A), F32),
            jax.ShapeDtypeStruct((n_tok, D_QKV), F32),
            jax.ShapeDtypeStruct((n_tok, 3 * D_C), F32),
        ],
        compiler_params=_cparams(("parallel",)),
        name="in_proj",
    )(x2, g, w_bf, rc, rs)


def _dwconv(pad_ref, r0, rows, w, b, pad_left):
    return _conv_taps(pad_ref[pl.ds(r0, rows + 2 * SUBLANES), :], rows, w, b, pad_left)


def _conv_taps(win, rows, w, b, pad_left):
    total = rows + 2 * SUBLANES
    acc = None
    for k in range(w.shape[0]):
        shift = (pad_left - k) % total
        tap = win if shift == 0 else pltpu.roll(win, shift, axis=0)
        term = tap[SUBLANES:SUBLANES + rows] * w[k:k + 1, :]
        acc = term if acc is None else acc + term
    return acc + b


def _dwconv_block(src_ref, j, nblk, w, b, pad_left):
    r0 = j * DFT_BLK
    zeros = jnp.zeros((SUBLANES, src_ref.shape[1]), F32)
    top = zeros if j == 0 else src_ref[r0 - SUBLANES:r0, :]
    bot = zeros if j == nblk - 1 else src_ref[r0 + DFT_BLK:r0 + DFT_BLK + SUBLANES, :]
    win = jnp.concatenate([top, src_ref[r0:r0 + DFT_BLK, :], bot], axis=0)
    return _conv_taps(win, DFT_BLK, w, b, pad_left)


def _fill_padded(pad_ref, src_ref, seq):
    zeros = jnp.zeros((SUBLANES, pad_ref.shape[1]), F32)
    pad_ref[0:SUBLANES, :] = zeros
    pad_ref[seq + SUBLANES:seq + 2 * SUBLANES, :] = zeros
    pad_ref[SUBLANES:seq + SUBLANES, :] = src_ref[...]


def _softplus(x):
    return jnp.maximum(x, 0.0) + jnp.log1p(jnp.exp(-jnp.abs(x)))


def _gelu_tanh(x):
    c = math.sqrt(2.0 / math.pi)
    return x * (0.5 * (1.0 + jnp.tanh(c * (x + 0.044715 * (x * x * x)))))


def _lru_kernel(u_ref, gate_ref, cw_ref, cb_ref, w_ref, bias_ref, lam_ref, o_ref,
                upad, xci, gg, af, bf, ar, br, pfs, hfs, prs, hrs, *, seq, rows, unroll):
    seg = seq // SCAN_SEGS
    tiles = rows // SCAN_SEGS
    nchunk = seq // rows
    per_seg = seg // rows
    _fill_padded(upad, u_ref, seq)
    cw = cw_ref[...]
    cb = cb_ref[...]
    w = w_ref[0] * 0.5
    bias = bias_ref[0] * 0.5
    hnsp = (-0.5 * C_LRU) * _softplus(-lam_ref[0])

    def conv(ci, carry):
        r0 = pl.multiple_of(ci * rows, rows)
        dst = pl.ds((ci % per_seg) * (rows * SCAN_SEGS) + ci // per_seg, rows, stride=SCAN_SEGS)
        xci[dst, :] = _dwconv(upad, r0, rows, cw, cb, 2)
        gg[dst, :] = _gelu_tanh(gate_ref[pl.ds(r0, rows), :])
        return carry

    lax.fori_loop(0, nchunk, conv, 0)

    def gates(ci, carry):
        r0 = pl.multiple_of(ci * rows, rows)
        xc = xci[pl.ds(r0, rows), :]
        t = jnp.tanh(jnp.dot(xc.astype(BF16), w, preferred_element_type=F32) + bias)
        hxc = 0.5 * xc
        for d, (a_ref, b_ref) in enumerate(((af, bf), (ar, br))):
            tr = t[:, (2 * d) * LANES:(2 * d + 1) * LANES]
            ti = t[:, (2 * d + 1) * LANES:(2 * d + 2) * LANES]
            hn = hnsp[:, d * LANES:(d + 1) * LANES]
            log_a = tr * hn + hn
            a = jnp.exp(log_a)
            nem = (-1.0 - a * a) * jnp.tanh(log_a)
            a_ref[pl.ds(r0, rows), :] = a
            b_ref[pl.ds(r0, rows), :] = jnp.sqrt(nem) * (ti * hxc + hxc)
        return carry

    lax.fori_loop(0, nchunk, gates, 0, unroll=2)

    def scan(it, carry):
        pf, hf, pr, hr = carry
        for u in range(unroll):
            i = it * unroll + u
            rf = pl.multiple_of(i * SCAN_SEGS, SCAN_SEGS)
            a = af[pl.ds(rf, SCAN_SEGS), :]
            pf = a * pf
            hf = a * hf + bf[pl.ds(rf, SCAN_SEGS), :]
            pfs[pl.ds(rf, SCAN_SEGS), :] = pf
            hfs[pl.ds(rf, SCAN_SEGS), :] = hf
            rr = pl.multiple_of((seg - 1 - i) * SCAN_SEGS, SCAN_SEGS)
            a = ar[pl.ds(rr, SCAN_SEGS), :]
            pr = a * pr
            hr = a * hr + br[pl.ds(rr, SCAN_SEGS), :]
            prs[pl.ds(rr, SCAN_SEGS), :] = pr
            hrs[pl.ds(rr, SCAN_SEGS), :] = hr
        return pf, hf, pr, hr

    one = jnp.ones((SCAN_SEGS, LANES), F32)
    zero = jnp.zeros((SCAN_SEGS, LANES), F32)
    pf, hf, pr, hr = lax.fori_loop(0, seg // unroll, scan, (one, zero, one, zero))

    sub = lax.broadcasted_iota(jnp.int32, (SCAN_SEGS, LANES), 0)
    cf = zero
    cr = zero
    for _ in range(SCAN_SEGS - 1):
        cf = jnp.where(sub == 0, 0.0, pltpu.roll(hf + pf * cf, 1, axis=0))
        cr = jnp.where(sub == SCAN_SEGS - 1, 0.0, pltpu.roll(hr + pr * cr, SCAN_SEGS - 1, axis=0))

    def combine(ci, carry):
        r0 = pl.multiple_of(ci * rows, rows)
        sl = pl.ds(r0, rows)
        tile3 = lambda ref: ref[sl, :].reshape(tiles, SCAN_SEGS, LANES)
        h = (tile3(hfs) + tile3(pfs) * cf[None]) + (tile3(hrs) + tile3(prs) * cr[None])
        out = h.reshape(rows, LANES) * gg[sl, :]
        for k in range(tiles):
            o_ref[pl.ds(ci * tiles + k, SCAN_SEGS, stride=seg), :] = (
                out[k * SCAN_SEGS:(k + 1) * SCAN_SEGS])
        return carry

    lax.fori_loop(0, nchunk, combine, 0)


def _lru(pa, cw, cb, wg, bias, lam, batch, seq, rows=128, unroll=8):
    n_tok = pa.shape[0]
    ng = D_A // LANES
    kern = functools.partial(_lru_kernel, seq=seq, rows=rows, unroll=unroll)
    return pl.pallas_call(
        kern,
        grid=(batch, ng),
        in_specs=[
            pl.BlockSpec((seq, LANES), lambda b, g: (b, g)),
            pl.BlockSpec((seq, LANES), lambda b, g: (b, ng + g)),
            pl.BlockSpec((cw.shape[0], LANES), lambda b, g: (0, g)),
            pl.BlockSpec((1, LANES), lambda b, g: (0, g)),
            pl.BlockSpec((1, LANES, 4 * LANES), lambda b, g: (g, 0, 0)),
            pl.BlockSpec((1, 1, 4 * LANES), lambda b, g: (g, 0, 0)),
            pl.BlockSpec((1, 1, 2 * LANES), lambda b, g: (g, 0, 0)),
        ],
        out_specs=pl.BlockSpec((seq, LANES), lambda b, g: (b, g)),
        out_shape=jax.ShapeDtypeStruct((n_tok, D_A), F32),
        scratch_shapes=[pltpu.VMEM((seq + 2 * SUBLANES, LANES), F32)]
        + [pltpu.VMEM((seq, LANES), F32) for _ in range(10)],
        compiler_params=_cparams(("parallel", "parallel")),
        name="rglru",
    )(pa, pa, cw, cb, wg, bias, lam)


_ATTN_STACK = (0, 2, 3, 5, 1, 4)
_ATTN_STRAIGHT = 4


def _attn_kernel(sink_ref, q_ref, k_ref, v_ref, o_ref, *, seq, tq):
    band = 3 * BLOCK
    nsub = tq // BLOCK
    nh = N_Q_HEADS
    ns = _ATTN_STRAIGHT * BLOCK
    i = pl.program_id(1)
    qi = lax.broadcasted_iota(jnp.int32, (BLOCK, band), 0)
    ki = lax.broadcasted_iota(jnp.int32, (BLOCK, band), 1)
    lo_kv = lax.broadcasted_iota(jnp.int32, (band, D_KV), 1) < HEAD_DIM
    lo_q = lax.broadcasted_iota(jnp.int32, (BLOCK, 2 * HEAD_DIM), 1) < HEAD_DIM
    sink = jnp.concatenate([jnp.full((BLOCK, 1), sink_ref[h], F32) for h in _ATTN_STACK], axis=0)
    nt = (((1,), (1,)), ((), ()))
    for jb in range(nsub):
        rows = slice(jb * BLOCK, (jb + 1) * BLOCK)
        q0 = (i * nsub + jb) * BLOCK
        k0 = pl.multiple_of(jnp.clip(q0 - BLOCK, 0, seq - band), BLOCK)
        bias = jnp.where(jnp.abs((q0 + qi) - (k0 + ki)) <= WINDOW, 0.0, NEG)
        kb = k_ref[pl.ds(k0, band), :]
        vb = v_ref[pl.ds(k0, band), :]
        kbs = pltpu.roll(kb, HEAD_DIM, axis=1)
        vbs = pltpu.roll(vb, HEAD_DIM, axis=1)
        qs = []
        for h in _ATTN_STACK:
            qt = q_ref[rows, (h // 2) * 2 * HEAD_DIM:(h // 2 + 1) * 2 * HEAD_DIM]
            qs.append(jnp.where(lo_q, qt, 0.0) if h % 2 == 0 else jnp.where(lo_q, 0.0, qt))
        qs = jnp.concatenate(qs, axis=0).astype(BF16)
        s = jnp.concatenate([
            lax.dot_general(qs[:ns], kb.astype(BF16), nt, preferred_element_type=F32),
            lax.dot_general(qs[ns:], kbs.astype(BF16), nt, preferred_element_type=F32)], axis=0)
        s = (s.reshape(nh, BLOCK, band) + bias[None]).reshape(nh * BLOCK, band)
        m = jnp.maximum(jnp.max(s, axis=-1, keepdims=True), sink)
        p = jnp.exp(s - m).astype(BF16)
        esk = jnp.exp(sink - m)
        res = {}
        vvar = {}
        for pos, h in enumerate(_ATTN_STACK):
            key = (pos < _ATTN_STRAIGHT, h % 2)
            if key not in vvar:
                src = vb if key[0] else vbs
                vv = jnp.where(lo_kv, src, 1.0) if h % 2 == 0 else jnp.where(lo_kv, 1.0, src)
                vvar[key] = vv.astype(BF16)
            pr = slice(pos * BLOCK, (pos + 1) * BLOCK)
            ov = jnp.dot(p[pr], vvar[key], preferred_element_type=F32)
            res[h] = ov / (pltpu.roll(ov, HEAD_DIM, axis=1) + esk[pr])
        for t in range(nh // 2):
            o_ref[rows, t * 2 * HEAD_DIM:(t + 1) * 2 * HEAD_DIM] = (
                jnp.where(lo_q, res[2 * t], res[2 * t + 1]))


def _attention(pq, sink, batch, seq, tq=512):
    n_tok = pq.shape[0]
    nq = seq // tq
    kcol = D_B // D_KV
    kern = functools.partial(_attn_kernel, seq=seq, tq=tq)
    return pl.pallas_call(
        kern,
        grid=(batch, nq),
        in_specs=[
            pl.BlockSpec(memory_space=pltpu.SMEM),
            pl.BlockSpec((tq, D_B), lambda b, i: (b * nq + i, 0)),
            pl.BlockSpec((seq, D_KV), lambda b, i: (b, kcol)),
            pl.BlockSpec((seq, D_KV), lambda b, i: (b, kcol + 1)),
        ],
        out_specs=pl.BlockSpec((tq, D_B), lambda b, i: (b * nq + i, 0)),
        out_shape=jax.ShapeDtypeStruct((n_tok, D_B), F32),
        compiler_params=_cparams(("parallel", "arbitrary")),
        name="win_attn",
    )(sink, pq, pq, pq)


def _hy_filter_kernel(z_ref, w1_ref, b1_ref, fr_ref, w2_ref, b2_ref, w3_ref, dec_ref, o_ref):
    hi = lax.Precision.HIGHEST
    fr = fr_ref[...]
    h = jnp.sin(fr * (jnp.dot(z_ref[...], w1_ref[...], preferred_element_type=F32, precision=hi)
                      + b1_ref[...]))
    h = jnp.sin(fr * (jnp.dot(h, w2_ref[...], preferred_element_type=F32, precision=hi) + b2_ref[...]))
    f = jnp.dot(h, w3_ref[...], preferred_element_type=F32, precision=hi) * dec_ref[...]
    for half in range(2):
        for d in range(2):
            c0 = (2 * half + d) * D_C
            o_ref[d, half] = f[:, c0:c0 + D_C]


def _hy_filter(zemb2, w1, b1, fr, w2, b2, w3, dec4, rows=512):
    n = zemb2.shape[0]
    full = lambda a: pl.BlockSpec(a.shape, lambda i: (0,) * a.ndim)
    return pl.pallas_call(
        _hy_filter_kernel,
        grid=(n // rows,),
        in_specs=[pl.BlockSpec((rows, zemb2.shape[1]), lambda i: (i, 0)),
                  full(w1), full(b1), full(fr), full(w2), full(b2), full(w3),
                  pl.BlockSpec((rows, 4 * D_C), lambda i: (i, 0))],
        out_specs=pl.BlockSpec((2, 2, rows, D_C), lambda i: (0, 0, i, 0)),
        out_shape=jax.ShapeDtypeStruct((2, 2, n, D_C), F32),
        compiler_params=_cparams(("parallel",)),
        name="hyena_filter",
    )(zemb2, w1, b1, fr, w2, b2, w3, dec4)


def _dft_outer_fwd(zs, ka_ref, o_ref):
    nj, _, lanes = zs.shape
    for p in range(DFT_BLK // DFT_SUB):
        sl = slice(p * DFT_SUB, (p + 1) * DFT_SUB)
        xg = zs[:, sl, :].reshape(nj * DFT_SUB, lanes).astype(BF16)
        c = jnp.dot(ka_ref[...], xg, preferred_element_type=F32)
        o_ref[0, :, :, sl, :] = c.reshape(2, DFT_NBLK, DFT_SUB, lanes).astype(BF16)


def _hy_fwd_data_kernel(x1_ref, v_ref, cw_ref, cb_ref, ka_ref, o_ref, zs, *, seq):
    nj = seq // DFT_BLK
    cw = cw_ref[...]
    cb = cb_ref[...]
    for j in range(nj):
        x1 = _dwconv_block(x1_ref, j, nj, cw[:, D_C:2 * D_C], cb[:, D_C:2 * D_C], 1)
        v = _dwconv_block(v_ref, j, nj, cw[:, 2 * D_C:], cb[:, 2 * D_C:], 1)
        zs[j] = v * x1
    _dft_outer_fwd(zs, ka_ref, o_ref)


def _hy_fwd_filt_kernel(f_ref, ka_ref, o_ref):
    _dft_outer_fwd(f_ref.at[0], ka_ref, o_ref)


def _hy_fwd_data(pc, cw, cb, ka, batch, seq):
    nj = seq // DFT_BLK
    kern = functools.partial(_hy_fwd_data_kernel, seq=seq)
    return pl.pallas_call(
        kern,
        grid=(batch,),
        in_specs=[
            pl.BlockSpec((seq, D_C), lambda b: (b, 1)),
            pl.BlockSpec((seq, D_C), lambda b: (b, 2)),
            pl.BlockSpec(cw.shape, lambda b: (0, 0)),
            pl.BlockSpec(cb.shape, lambda b: (0, 0)),
            pl.BlockSpec(ka.shape, lambda b: (0, 0)),
        ],
        out_specs=pl.BlockSpec((1, 2, DFT_NBLK, DFT_BLK, D_C), lambda b: (b, 0, 0, 0, 0)),
        out_shape=jax.ShapeDtypeStruct((batch, 2, DFT_NBLK, DFT_BLK, D_C), BF16),
        scratch_shapes=[pltpu.VMEM((nj, DFT_BLK, D_C), F32)],
        compiler_params=_cparams(("parallel",)),
        name="hyena_dft_outer",
    )(pc, pc, cw, cb, ka)


def _hy_fwd_filt(filt4, ka):
    ndir, nj = filt4.shape[0], filt4.shape[1]
    return pl.pallas_call(
        _hy_fwd_filt_kernel,
        grid=(ndir,),
        in_specs=[
            pl.BlockSpec((1, nj, DFT_BLK, D_C), lambda b: (b, 0, 0, 0)),
            pl.BlockSpec(ka.shape, lambda b: (0, 0)),
        ],
        out_specs=pl.BlockSpec((1, 2, DFT_NBLK, DFT_BLK, D_C), lambda b: (b, 0, 0, 0, 0)),
        out_shape=jax.ShapeDtypeStruct((ndir, 2, DFT_NBLK, DFT_BLK, D_C), BF16),
        compiler_params=_cparams(("parallel",)),
        name="hyena_dft_outer_filter",
    )(filt4, ka)


def _hy_inner_kernel(c_ref, f_ref, g_ref, gi_ref, o_ref, *, batch, kper):
    for q in range(kper):
        g = g_ref[q]
        gi = gi_ref[q]
        hf = jnp.dot(g, f_ref[0, :, q].reshape(2 * DFT_BLK, D_C), preferred_element_type=F32)
        hb = jnp.dot(g, f_ref[1, :, q].reshape(2 * DFT_BLK, D_C), preferred_element_type=F32)
        hre = hf[:DFT_BLK] + hb[:DFT_BLK]
        him = hf[DFT_BLK:] - hb[DFT_BLK:]
        for b in range(batch):
            x = jnp.dot(g, c_ref[b, :, q].reshape(2 * DFT_BLK, D_C), preferred_element_type=F32)
            xre, xim = x[:DFT_BLK], x[DFT_BLK:]
            y = jnp.concatenate([xre * hre - xim * him, xre * him + xim * hre], axis=0)
            d = jnp.dot(gi, y.astype(BF16), preferred_element_type=F32)
            o_ref[b, :, q] = d.reshape(2, DFT_BLK, D_C).astype(BF16)


def _hy_inner(cdata, cfilt, g, gi, kper=2):
    batch = cdata.shape[0]
    kern = functools.partial(_hy_inner_kernel, batch=batch, kper=kper)
    blk = lambda nb: pl.BlockSpec((nb, 2, kper, DFT_BLK, D_C), lambda k: (0, 0, k, 0, 0))
    return pl.pallas_call(
        kern,
        grid=(DFT_NBLK // kper,),
        in_specs=[blk(batch), blk(cfilt.shape[0]),
                  pl.BlockSpec((kper, 2 * DFT_BLK, 2 * DFT_BLK), lambda k: (k, 0, 0)),
                  pl.BlockSpec((kper, 2 * DFT_BLK, 2 * DFT_BLK), lambda k: (k, 0, 0))],
        out_specs=blk(batch),
        out_shape=jax.ShapeDtypeStruct(cdata.shape, BF16),
        compiler_params=_cparams(("parallel",)),
        name="hyena_dft_inner",
    )(cdata, cfilt, g, gi)


def _hy_out_kernel(d_ref, x0_ref, x1_ref, v_ref, w0_ref, w1_ref, wv_ref, b0_ref, b1_ref, bv_ref,
                   hb_ref, kai_ref, o_ref, ys, *, seq):
    nj = seq // DFT_BLK
    for p in range(DFT_BLK // DFT_SUB):
        sl = slice(p * DFT_SUB, (p + 1) * DFT_SUB)
        rhs = d_ref[0, :, :, sl, :].reshape(2 * DFT_NBLK * DFT_SUB, LANES)
        y = jnp.dot(kai_ref[...], rhs, preferred_element_type=F32)
        ys[:, sl, :] = y.reshape(nj, DFT_SUB, LANES)
    hb = hb_ref[...]
    for j in range(nj):
        x0 = _dwconv_block(x0_ref, j, nj, w0_ref[...], b0_ref[...], 1)
        x1 = _dwconv_block(x1_ref, j, nj, w1_ref[...], b1_ref[...], 1)
        v = _dwconv_block(v_ref, j, nj, wv_ref[...], bv_ref[...], 1)
        z = v * x1
        o_ref[j * DFT_BLK:(j + 1) * DFT_BLK, :] = (ys[j] + z * hb) * x0


def _hy_out(dd, pc, cw, cb, hb, kai, batch, seq):
    n_tok = pc.shape[0]
    nh = D_C // LANES
    nj = seq // DFT_BLK
    kern = functools.partial(_hy_out_kernel, seq=seq)
    part = lambda p, r: pl.BlockSpec((r, LANES), lambda b, c: (0, p * nh + c))
    return pl.pallas_call(
        kern,
        grid=(batch, nh),
        in_specs=[
            pl.BlockSpec((1, 2, DFT_NBLK, DFT_BLK, LANES), lambda b, c: (b, 0, 0, 0, c)),
            pl.BlockSpec((seq, LANES), lambda b, c: (b, c)),
            pl.BlockSpec((seq, LANES), lambda b, c: (b, nh + c)),
            pl.BlockSpec((seq, LANES), lambda b, c: (b, 2 * nh + c)),
            part(0, cw.shape[0]), part(1, cw.shape[0]), part(2, cw.shape[0]),
            part(0, 1), part(1, 1), part(2, 1),
            pl.BlockSpec((1, LANES), lambda b, c: (0, c)),
            pl.BlockSpec(kai.shape, lambda b, c: (0, 0)),
        ],
        out_specs=pl.BlockSpec((seq, LANES), lambda b, c: (b, c)),
        out_shape=jax.ShapeDtypeStruct((n_tok, D_C), F32),
        scratch_shapes=[pltpu.VMEM((nj, DFT_BLK, LANES), F32)],
        compiler_params=_cparams(("parallel", "parallel")),
        name="hyena_out",
    )(dd, pc, pc, pc, cw, cw, cw, cb, cb, cb, hb, kai)


def _out_proj_kernel(ya_ref, yb_ref, yc_ref, x_ref, ga_ref, gb_ref, gc_ref, w_ref, o_ref):
    y = jnp.concatenate([_rms(ya_ref[...], ga_ref[...]), _rms(yb_ref[...], gb_ref[...]),
                         _rms(yc_ref[...], gc_ref[...])], axis=-1).astype(BF16)
    o_ref[...] = x_ref[...] + jnp.dot(y, w_ref[...], preferred_element_type=F32)


def _out_proj(ya, yb, yc, x2, ga, gb, gc, w_bf, tm):
    n_tok = x2.shape[0]
    row = lambda w: pl.BlockSpec((tm, w), lambda i: (i, 0))
    vec = lambda w: pl.BlockSpec((1, w), lambda i: (0, 0))
    return pl.pallas_call(
        _out_proj_kernel,
        grid=(n_tok // tm,),
        in_specs=[row(D_A), row(D_B), row(D_C), row(D_MODEL), vec(D_A), vec(D_B), vec(D_C),
                  pl.BlockSpec((D_MODEL, D_MODEL), lambda i: (0, 0))],
        out_specs=row(D_MODEL),
        out_shape=jax.ShapeDtypeStruct((n_tok, D_MODEL), F32),
        compiler_params=_cparams(("parallel",)),
        name="out_proj",
    )(ya, yb, yc, x2, ga, gb, gc, w_bf)


def _mlp_kernel(x_ref, g_ref, wu_ref, wd_ref, gf_ref, o_ref, *, ff_chunk, final_norm):
    x = x_ref[...]
    h = _rms(x, g_ref[...]).astype(BF16)
    acc = x
    for c in range(D_FF // ff_chunk):
        sl = slice(c * ff_chunk, (c + 1) * ff_chunk)
        u = jnp.maximum(jnp.dot(h, wu_ref[:, sl], preferred_element_type=F32), 0.0)
        acc = acc + jnp.dot((u * u).astype(BF16), wd_ref[sl, :], preferred_element_type=F32)
    if final_norm:
        acc = _rms(acc, gf_ref[...])
    o_ref[...] = acc


def _mlp(x2, g, wu_bf, wd_bf, gf, tm, final_norm, ff_chunk=1024):
    n_tok = x2.shape[0]
    kern = functools.partial(_mlp_kernel, ff_chunk=ff_chunk, final_norm=final_norm)
    return pl.pallas_call(
        kern,
        grid=(n_tok // tm,),
        in_specs=[pl.BlockSpec((tm, D_MODEL), lambda i: (i, 0)),
                  pl.BlockSpec((1, D_MODEL), lambda i: (0, 0)),
                  pl.BlockSpec((D_MODEL, D_FF), lambda i: (0, 0)),
                  pl.BlockSpec((D_FF, D_MODEL), lambda i: (0, 0)),
                  pl.BlockSpec((1, D_MODEL), lambda i: (0, 0))],
        out_specs=pl.BlockSpec((tm, D_MODEL), lambda i: (i, 0)),
        out_shape=jax.ShapeDtypeStruct((n_tok, D_MODEL), F32),
        compiler_params=_cparams(("parallel",)),
        name="mlp",
    )(x2, g, wu_bf, wd_bf, gf)


def _rope_tables(seq):
    pos = np.arange(seq, dtype=np.float32)
    inv_freq = (np.float32(ROPE_THETA) ** (-np.arange(0, ROT_DIM, 2, dtype=np.float32) / ROT_DIM))
    ang = (pos[:, None] * inv_freq[None, :]).astype(np.float32)
    cos, sin = np.cos(ang).astype(np.float32), np.sin(ang).astype(np.float32)
    rest = HEAD_DIM - ROT_DIM
    c_head = np.concatenate([cos, cos, np.ones((seq, rest), np.float32)], axis=1)
    s_head = np.concatenate([-sin, sin, np.zeros((seq, rest), np.float32)], axis=1)
    scale = np.float32(HEAD_DIM ** -0.5)
    c = np.concatenate([np.tile(c_head, (1, N_Q_HEADS)) * scale, np.tile(c_head, (1, N_KV_HEADS))], axis=1)
    s = np.concatenate([np.tile(s_head, (1, N_Q_HEADS)) * scale, np.tile(s_head, (1, N_KV_HEADS))], axis=1)
    return jnp.asarray(c), jnp.asarray(s)


def _hyena_position_tables(seq):
    t = np.linspace(0.0, 1.0, seq, dtype=np.float32)[:, None]
    w = (2.0 * math.pi * np.arange(seq, dtype=np.float32)[:, None] / seq).astype(np.float32)
    f = np.linspace(1e-4, HY_BANDS - 1, HY_BANDS, dtype=np.float32)[None, :]
    fw = (f * w).astype(np.float32)
    z = np.concatenate([t, np.cos(fw), -np.sin(fw)], axis=-1).astype(np.float32)
    deltas = np.abs(np.linspace(HY_MIN_DECAY, HY_MAX_DECAY, D_C, dtype=np.float32))
    decay = np.exp(-t * deltas[None, :]).astype(np.float32)
    decay_b = decay.copy()
    decay_b[0] = 0.0
    half = seq // 2
    zp = np.zeros((half, 2, HY_WIDTH), np.float32)
    zp[:, 0, :HY_EMB] = z[:half]
    zp[:, 1, :HY_EMB] = z[half:]
    dec4 = np.concatenate([decay[:half], decay_b[:half], decay[half:], decay_b[half:]], axis=1)
    return jnp.asarray(zp.reshape(half, 2 * HY_WIDTH)), jnp.asarray(dec4)


def _blockdiag2(a):
    z = jnp.zeros_like(a)
    return jnp.concatenate([jnp.concatenate([a, z], axis=1), jnp.concatenate([z, a], axis=1)], axis=0)


def _dft_tables(seq):
    n = 2 * seq
    nj = seq // DFT_BLK
    kk = np.arange(DFT_NBLK)
    ang = 2.0 * np.pi * np.outer(kk, np.arange(nj)) / DFT_NBLK
    eye = np.eye(DFT_SUB)
    ka = np.concatenate([np.kron(np.cos(ang), eye), np.kron(-np.sin(ang), eye)], axis=0)
    kai = np.concatenate([np.kron(np.cos(ang).T, eye), np.kron(-np.sin(ang).T, eye)], axis=1) / n
    m = np.arange(DFT_BLK)
    k = kk[:, None, None] + DFT_NBLK * np.arange(DFT_BLK)[None, :, None]
    ph = 2.0 * np.pi * ((k * m[None, None, :]) % n) / n
    gre, gim = np.cos(ph), -np.sin(ph)
    g = np.concatenate([np.concatenate([gre, -gim], axis=2), np.concatenate([gim, gre], axis=2)], axis=1)
    gi = np.transpose(g, (0, 2, 1))
    as_bf = lambda a: jnp.asarray(a.astype(np.float32)).astype(BF16)
    return as_bf(ka), as_bf(kai), as_bf(g), as_bf(gi)


def _lru_blockdiag(w):
    nb = w.shape[1] // 2
    w = w.reshape(2, nb, 2, w.shape[2], w.shape[3])
    z = jnp.zeros_like(w[:, :, 0])
    top = jnp.concatenate([w[:, :, 0], z], axis=-1)
    bot = jnp.concatenate([z, w[:, :, 1]], axis=-1)
    return jnp.concatenate([top, bot], axis=-2)


def kernel(x, norm_mix_g, w_in, conv_a_w, conv_a_b, lru_wa, lru_ba, lru_wx, lru_bx, lru_lambda,
           attn_sink, hy_conv_w, hy_conv_b, hy_w1, hy_b1, hy_freq, hy_w2, hy_b2, hy_w3, hy_bias,
           gnorm_a, gnorm_b, gnorm_c, w_out, norm_mlp_g, w_up, w_down, final_norm_g):
    batch, seq, _ = x.shape
    depth = w_in.shape[0]
    n_tok = batch * seq
    tm = 1024
    tm_mlp = 512
    ng = D_A // LANES

    rc, rs = _rope_tables(seq)
    zemb, dec4 = _hyena_position_tables(seq)
    ka, kai, gtab, gitab = _dft_tables(seq)

    xs = x.reshape(n_tok, D_MODEL)
    for i in range(depth):
        pa, pq, pc = _in_proj(xs, norm_mix_g[i][None], w_in[i].astype(BF16), rc, rs, seq, tm)

        wa, wx = _lru_blockdiag(lru_wa[i]), _lru_blockdiag(lru_wx[i])
        wg = jnp.concatenate([wa[0], wx[0], wa[1], wx[1]], axis=-1).astype(BF16)
        tile = lambda v: v.reshape(ng, 1, LANES)
        bias = jnp.concatenate([tile(lru_ba[i][0]), tile(lru_bx[i][0]),
                                tile(lru_ba[i][1]), tile(lru_bx[i][1])], axis=-1)
        lam = jnp.concatenate([tile(lru_lambda[i][0]), tile(lru_lambda[i][1])], axis=-1)
        y_a = _lru(pa, conv_a_w[i], conv_a_b[i][None], wg, bias, lam, batch, seq)

        y_b = _attention(pq, attn_sink[i], batch, seq)

        w1p = jnp.zeros((HY_WIDTH, HY_WIDTH), F32).at[:HY_EMB].set(hy_w1[i])
        pair = lambda v: jnp.concatenate([v, v])[None]
        filt = _hy_filter(zemb, _blockdiag2(w1p), pair(hy_b1[i]), pair(hy_freq[i]),
                          _blockdiag2(hy_w2[i]), pair(hy_b2[i]), _blockdiag2(hy_w3[i]), dec4)
        cw, cb = hy_conv_w[i], hy_conv_b[i][None]
        cdata = _hy_fwd_data(pc, cw, cb, ka, batch, seq)
        cfilt = _hy_fwd_filt(filt.reshape(2, seq // DFT_BLK, DFT_BLK, D_C), ka)
        dd = _hy_inner(cdata, cfilt, gtab, gitab)
        y_c = _hy_out(dd, pc, cw, cb, hy_bias[i][None], kai, batch, seq)

        xs = _out_proj(y_a, y_b, y_c, xs, gnorm_a[i][None], gnorm_b[i][None], gnorm_c[i][None],
                       w_out[i].astype(BF16), tm)
        xs = _mlp(xs, norm_mlp_g[i][None], w_up[i].astype(BF16), w_down[i].astype(BF16),
                  final_norm_g[None], tm_mlp, final_norm=(i == depth - 1))
    return xs.reshape(batch, seq, D_MODEL)
```

```python
import functools
import math

import numpy as np
import jax
import jax.numpy as jnp
from jax import lax
from jax.experimental import pallas as pl
from jax.experimental.pallas import tpu as pltpu

F32 = jnp.float32
BF16 = jnp.bfloat16

D_MODEL = 1024
D_A = 384
D_B = 384
D_C = 256
HEAD_DIM = 64
N_Q_HEADS = 6
N_KV_HEADS = 2
GROUP = N_Q_HEADS // N_KV_HEADS
D_KV = N_KV_HEADS * HEAD_DIM
D_QKV = D_B + 2 * D_KV
D_IN = 2 * D_A + D_QKV + 3 * D_C
C_LRU = 8.0
WINDOW = 128
BLOCK = 128
ROPE_THETA = 500000.0
ROT_DIM = HEAD_DIM // 4
HY_EMB = 33
HY_BANDS = (HY_EMB - 1) // 2
HY_WIDTH = 64
HY_TARGET = 1e-2
HY_MAX_DECAY = math.log(HY_TARGET) / 0.3
HY_MIN_DECAY = math.log(HY_TARGET) / 1.5
D_FF = 4 * D_MODEL
EPS = 1e-6
NEG = -1e30

LANES = 128
SUBLANES = 8
VMEM_LIMIT = 56 * 1024 * 1024

SCAN_SEGS = SUBLANES
DFT_BLK = 128
DFT_NBLK = 64
DFT_SUB = 16


def _cparams(sem):
    return pltpu.CompilerParams(dimension_semantics=sem, vmem_limit_bytes=VMEM_LIMIT)


def _rms(x, g):
    return x * lax.rsqrt(jnp.mean(x * x, axis=-1, keepdims=True) + EPS) * g


def _in_proj_kernel(x_ref, g_ref, w_ref, rc_ref, rs_ref, oa_ref, oq_ref, oc_ref):
    h = _rms(x_ref[...], g_ref[...]).astype(BF16)
    oa_ref[...] = jnp.dot(h, w_ref[:, :2 * D_A], preferred_element_type=F32)
    oc_ref[...] = jnp.dot(h, w_ref[:, 2 * D_A + D_QKV:], preferred_element_type=F32)
    qkv = jnp.dot(h, w_ref[:, 2 * D_A:2 * D_A + D_QKV], preferred_element_type=F32)
    qk = qkv[:, :D_B + D_KV]
    n = D_B + D_KV
    half = ROT_DIM // 2
    lane = lax.broadcasted_iota(jnp.int32, qk.shape, 1) % HEAD_DIM
    swapped = jnp.where(lane < half, pltpu.roll(qk, n - half, axis=1), pltpu.roll(qk, half, axis=1))
    oq_ref[:, :n] = (qk * rc_ref[...] + swapped * rs_ref[...]).astype(BF16)
    oq_ref[:, n:] = qkv[:, n:].astype(BF16)


def _in_proj(x2, g, w_bf, rc, rs, seq, tm):
    n_tok = x2.shape[0]
    nrb = seq // tm
    return pl.pallas_call(
        _in_proj_kernel,
        grid=(n_tok // tm,),
        in_specs=[
            pl.BlockSpec((tm, D_MODEL), lambda i: (i, 0)),
            pl.BlockSpec((1, D_MODEL), lambda i: (0, 0)),
            pl.BlockSpec((D_MODEL, D_IN), lambda i: (0, 0)),
            pl.BlockSpec((tm, D_B + D_KV), lambda i: (i % nrb, 0)),
            pl.BlockSpec((tm, D_B + D_KV), lambda i: (i % nrb, 0)),
        ],
        out_specs=[
            pl.BlockSpec((tm, 2 * D_A), lambda i: (i, 0)),
            pl.BlockSpec((tm, D_QKV), lambda i: (i, 0)),
            pl.BlockSpec((tm, 3 * D_C), lambda i: (i, 0)),
        ],
        out_shape=[
            jax.ShapeDtypeStruct((n_tok, 2 * D_A), F32),
            jax.ShapeDtypeStruct((n_tok, D_QKV), BF16),
            jax.ShapeDtypeStruct((n_tok, 3 * D_C), F32),
        ],
        compiler_params=_cparams(("parallel",)),
        name="in_proj",
    )(x2, g, w_bf, rc, rs)


def _dwconv(pad_ref, r0, rows, w, b, pad_left):
    return _conv_taps(pad_ref[pl.ds(r0, rows + 2 * SUBLANES), :], rows, w, b, pad_left)


def _conv_taps(win, rows, w, b, pad_left):
    total = rows + 2 * SUBLANES
    acc = None
    for k in range(w.shape[0]):
        shift = (pad_left - k) % total
        tap = win if shift == 0 else pltpu.roll(win, shift, axis=0)
        term = tap[SUBLANES:SUBLANES + rows] * w[k:k + 1, :]
        acc = term if acc is None else acc + term
    return acc + b


def _dwconv_block(src_ref, j, nblk, w, b, pad_left):
    r0 = j * DFT_BLK
    zeros = jnp.zeros((SUBLANES, src_ref.shape[1]), F32)
    top = zeros if j == 0 else src_ref[r0 - SUBLANES:r0, :]
    bot = zeros if j == nblk - 1 else src_ref[r0 + DFT_BLK:r0 + DFT_BLK + SUBLANES, :]
    win = jnp.concatenate([top, src_ref[r0:r0 + DFT_BLK, :], bot], axis=0)
    return _conv_taps(win, DFT_BLK, w, b, pad_left)


def _fill_padded(pad_ref, src_ref, seq):
    zeros = jnp.zeros((SUBLANES, pad_ref.shape[1]), F32)
    pad_ref[0:SUBLANES, :] = zeros
    pad_ref[seq + SUBLANES:seq + 2 * SUBLANES, :] = zeros
    pad_ref[SUBLANES:seq + SUBLANES, :] = src_ref[...]


def _softplus(x):
    return jnp.maximum(x, 0.0) + jnp.log1p(jnp.exp(-jnp.abs(x)))


def _gelu_tanh(x):
    c = math.sqrt(2.0 / math.pi)
    return x * (0.5 * (1.0 + jnp.tanh(c * (x + 0.044715 * (x * x * x)))))


def _lru_kernel(u_ref, gate_ref, cw_ref, cb_ref, w_ref, bias_ref, lam_ref, o_ref,
                upad, xci, gg, af, bf, ar, br, pfs, hfs, prs, hrs, *, seq, rows, unroll):
    seg = seq // SCAN_SEGS
    tiles = rows // SCAN_SEGS
    nchunk = seq // rows
    per_seg = seg // rows
    _fill_padded(upad, u_ref, seq)
    cw = cw_ref[...]
    cb = cb_ref[...]
    w = w_ref[0] * 0.5
    bias = bias_ref[0] * 0.5
    hnsp = (-0.5 * C_LRU) * _softplus(-lam_ref[0])

    def conv(ci, carry):
        r0 = pl.multiple_of(ci * rows, rows)
        dst = pl.ds((ci % per_seg) * (rows * SCAN_SEGS) + ci // per_seg, rows, stride=SCAN_SEGS)
        xci[dst, :] = _dwconv(upad, r0, rows, cw, cb, 2)
        gg[dst, :] = _gelu_tanh(gate_ref[pl.ds(r0, rows), :])
        return carry

    lax.fori_loop(0, nchunk, conv, 0)

    def gates(ci, carry):
        r0 = pl.multiple_of(ci * rows, rows)
        xc = xci[pl.ds(r0, rows), :]
        t = jnp.tanh(jnp.dot(xc.astype(BF16), w, preferred_element_type=F32) + bias)
        hxc = 0.5 * xc
        for d, (a_ref, b_ref) in enumerate(((af, bf), (ar, br))):
            tr = t[:, (2 * d) * LANES:(2 * d + 1) * LANES]
            ti = t[:, (2 * d + 1) * LANES:(2 * d + 2) * LANES]
            hn = hnsp[:, d * LANES:(d + 1) * LANES]
            log_a = tr * hn + hn
            a = jnp.exp(log_a)
            nem = (-1.0 - a * a) * jnp.tanh(log_a)
            a_ref[pl.ds(r0, rows), :] = a
            b_ref[pl.ds(r0, rows), :] = jnp.sqrt(nem) * (ti * hxc + hxc)
        return carry

    lax.fori_loop(0, nchunk, gates, 0, unroll=2)

    def scan(it, carry):
        pf, hf, pr, hr = carry
        for u in range(unroll):
            i = it * unroll + u
            rf = pl.multiple_of(i * SCAN_SEGS, SCAN_SEGS)
            a = af[pl.ds(rf, SCAN_SEGS), :]
            pf = a * pf
            hf = a * hf + bf[pl.ds(rf, SCAN_SEGS), :]
            pfs[pl.ds(rf, SCAN_SEGS), :] = pf
            hfs[pl.ds(rf, SCAN_SEGS), :] = hf
            rr = pl.multiple_of((seg - 1 - i) * SCAN_SEGS, SCAN_SEGS)
            a = ar[pl.ds(rr, SCAN_SEGS), :]
            pr = a * pr
            hr = a * hr + br[pl.ds(rr, SCAN_SEGS), :]
            prs[pl.ds(rr, SCAN_SEGS), :] = pr
            hrs[pl.ds(rr, SCAN_SEGS), :] = hr
        return pf, hf, pr, hr

    one = jnp.ones((SCAN_SEGS, LANES), F32)
    zero = jnp.zeros((SCAN_SEGS, LANES), F32)
    pf, hf, pr, hr = lax.fori_loop(0, seg // unroll, scan, (one, zero, one, zero))

    sub = lax.broadcasted_iota(jnp.int32, (SCAN_SEGS, LANES), 0)
    cf = zero
    cr = zero
    for _ in range(SCAN_SEGS - 1):
        cf = jnp.where(sub == 0, 0.0, pltpu.roll(hf + pf * cf, 1, axis=0))
        cr = jnp.where(sub == SCAN_SEGS - 1, 0.0, pltpu.roll(hr + pr * cr, SCAN_SEGS - 1, axis=0))

    def combine(ci, carry):
        r0 = pl.multiple_of(ci * rows, rows)
        sl = pl.ds(r0, rows)
        tile3 = lambda ref: ref[sl, :].reshape(tiles, SCAN_SEGS, LANES)
        h = (tile3(hfs) + tile3(pfs) * cf[None]) + (tile3(hrs) + tile3(prs) * cr[None])
        out = h.reshape(rows, LANES) * gg[sl, :]
        for k in range(tiles):
            o_ref[pl.ds(ci * tiles + k, SCAN_SEGS, stride=seg), :] = (
                out[k * SCAN_SEGS:(k + 1) * SCAN_SEGS])
        return carry

    lax.fori_loop(0, nchunk, combine, 0)


def _lru(pa, cw, cb, wg, bias, lam, batch, seq, rows=128, unroll=8):
    n_tok = pa.shape[0]
    ng = D_A // LANES
    kern = functools.partial(_lru_kernel, seq=seq, rows=rows, unroll=unroll)
    return pl.pallas_call(
        kern,
        grid=(batch, ng),
        in_specs=[
            pl.BlockSpec((seq, LANES), lambda b, g: (b, g)),
            pl.BlockSpec((seq, LANES), lambda b, g: (b, ng + g)),
            pl.BlockSpec((cw.shape[0], LANES), lambda b, g: (0, g)),
            pl.BlockSpec((1, LANES), lambda b, g: (0, g)),
            pl.BlockSpec((1, LANES, 4 * LANES), lambda b, g: (g, 0, 0)),
            pl.BlockSpec((1, 1, 4 * LANES), lambda b, g: (g, 0, 0)),
            pl.BlockSpec((1, 1, 2 * LANES), lambda b, g: (g, 0, 0)),
        ],
        out_specs=pl.BlockSpec((seq, LANES), lambda b, g: (b, g)),
        out_shape=jax.ShapeDtypeStruct((n_tok, D_A), F32),
        scratch_shapes=[pltpu.VMEM((seq + 2 * SUBLANES, LANES), F32)]
        + [pltpu.VMEM((seq, LANES), F32) for _ in range(10)],
        compiler_params=_cparams(("parallel", "parallel")),
        name="rglru",
    )(pa, pa, cw, cb, wg, bias, lam)


_ATTN_STACK = (0, 2, 3, 5, 1, 4)
_ATTN_STRAIGHT = 4


def _attn_kernel(sink_ref, q_ref, k_ref, v_ref, o_ref, *, seq, tq):
    band = 3 * BLOCK
    nsub = tq // BLOCK
    nh = N_Q_HEADS
    ns = _ATTN_STRAIGHT * BLOCK
    i = pl.program_id(1)
    qi = lax.broadcasted_iota(jnp.int32, (BLOCK, band), 0)
    ki = lax.broadcasted_iota(jnp.int32, (BLOCK, band), 1)
    lo_kv = lax.broadcasted_iota(jnp.int32, (band, D_KV), 1) < HEAD_DIM
    lo_q = lax.broadcasted_iota(jnp.int32, (BLOCK, 2 * HEAD_DIM), 1) < HEAD_DIM
    sink = jnp.concatenate([jnp.full((BLOCK, 1), sink_ref[h], F32) for h in _ATTN_STACK], axis=0)
    nt = (((1,), (1,)), ((), ()))
    for jb in range(nsub):
        rows = slice(jb * BLOCK, (jb + 1) * BLOCK)
        q0 = (i * nsub + jb) * BLOCK
        k0 = pl.multiple_of(jnp.clip(q0 - BLOCK, 0, seq - band), BLOCK)
        bias = jnp.where(jnp.abs((q0 + qi) - (k0 + ki)) <= WINDOW, 0.0, NEG)
        kb = k_ref[pl.ds(k0, band), :]
        vb = v_ref[pl.ds(k0, band), :]
        kbs = pltpu.roll(kb, HEAD_DIM, axis=1)
        vbs = pltpu.roll(vb, HEAD_DIM, axis=1)
        qs = []
        for h in _ATTN_STACK:
            qt = q_ref[rows, (h // 2) * 2 * HEAD_DIM:(h // 2 + 1) * 2 * HEAD_DIM]
            qs.append(jnp.where(lo_q, qt, 0.0) if h % 2 == 0 else jnp.where(lo_q, 0.0, qt))
        qs = jnp.concatenate(qs, axis=0)
        s = jnp.concatenate([
            lax.dot_general(qs[:ns], kb, nt, preferred_element_type=F32),
            lax.dot_general(qs[ns:], kbs, nt, preferred_element_type=F32)], axis=0)
        s = (s.reshape(nh, BLOCK, band) + bias[None]).reshape(nh * BLOCK, band)
        m = jnp.maximum(jnp.max(s, axis=-1, keepdims=True), sink)
        p = jnp.exp(s - m).astype(BF16)
        esk = jnp.exp(sink - m)
        res = {}
        vvar = {}
        for pos, h in enumerate(_ATTN_STACK):
            key = (pos < _ATTN_STRAIGHT, h % 2)
            if key not in vvar:
                src = vb if key[0] else vbs
                vv = jnp.where(lo_kv, src, 1.0) if h % 2 == 0 else jnp.where(lo_kv, 1.0, src)
                vvar[key] = vv
            pr = slice(pos * BLOCK, (pos + 1) * BLOCK)
            ov = jnp.dot(p[pr], vvar[key], preferred_element_type=F32)
            res[h] = ov / (pltpu.roll(ov, HEAD_DIM, axis=1) + esk[pr])
        for t in range(nh // 2):
            o_ref[rows, t * 2 * HEAD_DIM:(t + 1) * 2 * HEAD_DIM] = (
                jnp.where(lo_q, res[2 * t], res[2 * t + 1]))


def _attention(pq, sink, batch, seq, tq=512):
    n_tok = pq.shape[0]
    nq = seq // tq
    kcol = D_B // D_KV
    kern = functools.partial(_attn_kernel, seq=seq, tq=tq)
    return pl.pallas_call(
        kern,
        grid=(batch, nq),
        in_specs=[
            pl.BlockSpec(memory_space=pltpu.SMEM),
            pl.BlockSpec((tq, D_B), lambda b, i: (b * nq + i, 0)),
            pl.BlockSpec((seq, D_KV), lambda b, i: (b, kcol)),
            pl.BlockSpec((seq, D_KV), lambda b, i: (b, kcol + 1)),
        ],
        out_specs=pl.BlockSpec((tq, D_B), lambda b, i: (b * nq + i, 0)),
        out_shape=jax.ShapeDtypeStruct((n_tok, D_B), F32),
        compiler_params=_cparams(("parallel", "arbitrary")),
        name="win_attn",
    )(sink, pq, pq, pq)


def _hy_filter_kernel(z_ref, w1_ref, b1_ref, fr_ref, w2_ref, b2_ref, w3_ref, dec_ref, o_ref):
    hi = lax.Precision.HIGHEST
    fr = fr_ref[...]
    h = jnp.sin(fr * (jnp.dot(z_ref[...], w1_ref[...], preferred_element_type=F32, precision=hi)
                      + b1_ref[...]))
    h = jnp.sin(fr * (jnp.dot(h, w2_ref[...], preferred_element_type=F32, precision=hi) + b2_ref[...]))
    f = jnp.dot(h, w3_ref[...], preferred_element_type=F32, precision=hi) * dec_ref[...]
    for half in range(2):
        for d in range(2):
            c0 = (2 * half + d) * D_C
            o_ref[d, half] = f[:, c0:c0 + D_C]


def _hy_filter(zemb2, w1, b1, fr, w2, b2, w3, dec4, rows=512):
    n = zemb2.shape[0]
    full = lambda a: pl.BlockSpec(a.shape, lambda i: (0,) * a.ndim)
    return pl.pallas_call(
        _hy_filter_kernel,
        grid=(n // rows,),
        in_specs=[pl.BlockSpec((rows, zemb2.shape[1]), lambda i: (i, 0)),
                  full(w1), full(b1), full(fr), full(w2), full(b2), full(w3),
                  pl.BlockSpec((rows, 4 * D_C), lambda i: (i, 0))],
        out_specs=pl.BlockSpec((2, 2, rows, D_C), lambda i: (0, 0, i, 0)),
        out_shape=jax.ShapeDtypeStruct((2, 2, n, D_C), F32),
        compiler_params=_cparams(("parallel",)),
        name="hyena_filter",
    )(zemb2, w1, b1, fr, w2, b2, w3, dec4)


def _dft_outer_fwd(zs, ka_ref, o_ref):
    nj, _, lanes = zs.shape
    for p in range(DFT_BLK // DFT_SUB):
        sl = slice(p * DFT_SUB, (p + 1) * DFT_SUB)
        xg = zs[:, sl, :].reshape(nj * DFT_SUB, lanes).astype(BF16)
        c = jnp.dot(ka_ref[...], xg, preferred_element_type=F32)
        o_ref[0, :, :, sl, :] = c.reshape(2, DFT_NBLK, DFT_SUB, lanes).astype(BF16)


def _hy_fwd_data_kernel(x1_ref, v_ref, cw_ref, cb_ref, ka_ref, o_ref, zs, *, seq):
    nj = seq // DFT_BLK
    cw = cw_ref[...]
    cb = cb_ref[...]
    for j in range(nj):
        x1 = _dwconv_block(x1_ref, j, nj, cw[:, D_C:2 * D_C], cb[:, D_C:2 * D_C], 1)
        v = _dwconv_block(v_ref, j, nj, cw[:, 2 * D_C:], cb[:, 2 * D_C:], 1)
        zs[j] = v * x1
    _dft_outer_fwd(zs, ka_ref, o_ref)


def _hy_fwd_filt_kernel(f_ref, ka_ref, o_ref):
    _dft_outer_fwd(f_ref.at[0], ka_ref, o_ref)


def _hy_fwd_data(pc, cw, cb, ka, batch, seq):
    nj = seq // DFT_BLK
    kern = functools.partial(_hy_fwd_data_kernel, seq=seq)
    return pl.pallas_call(
        kern,
        grid=(batch,),
        in_specs=[
            pl.BlockSpec((seq, D_C), lambda b: (b, 1)),
            pl.BlockSpec((seq, D_C), lambda b: (b, 2)),
            pl.BlockSpec(cw.shape, lambda b: (0, 0)),
            pl.BlockSpec(cb.shape, lambda b: (0, 0)),
            pl.BlockSpec(ka.shape, lambda b: (0, 0)),
        ],
        out_specs=pl.BlockSpec((1, 2, DFT_NBLK, DFT_BLK, D_C), lambda b: (b, 0, 0, 0, 0)),
        out_shape=jax.ShapeDtypeStruct((batch, 2, DFT_NBLK, DFT_BLK, D_C), BF16),
        scratch_shapes=[pltpu.VMEM((nj, DFT_BLK, D_C), F32)],
        compiler_params=_cparams(("parallel",)),
        name="hyena_dft_outer",
    )(pc, pc, cw, cb, ka)


def _hy_fwd_filt(filt4, ka):
    ndir, nj = filt4.shape[0], filt4.shape[1]
    return pl.pallas_call(
        _hy_fwd_filt_kernel,
        grid=(ndir,),
        in_specs=[
            pl.BlockSpec((1, nj, DFT_BLK, D_C), lambda b: (b, 0, 0, 0)),
            pl.BlockSpec(ka.shape, lambda b: (0, 0)),
        ],
        out_specs=pl.BlockSpec((1, 2, DFT_NBLK, DFT_BLK, D_C), lambda b: (b, 0, 0, 0, 0)),
        out_shape=jax.ShapeDtypeStruct((ndir, 2, DFT_NBLK, DFT_BLK, D_C), BF16),
        compiler_params=_cparams(("parallel",)),
        name="hyena_dft_outer_filter",
    )(filt4, ka)


def _hy_inner_kernel(c_ref, f_ref, g_ref, gi_ref, o_ref, *, batch, kper):
    for q in range(kper):
        g = g_ref[q]
        gi = gi_ref[q]
        hf = jnp.dot(g, f_ref[0, :, q].reshape(2 * DFT_BLK, D_C), preferred_element_type=F32)
        hb = jnp.dot(g, f_ref[1, :, q].reshape(2 * DFT_BLK, D_C), preferred_element_type=F32)
        hre = hf[:DFT_BLK] + hb[:DFT_BLK]
        him = hf[DFT_BLK:] - hb[DFT_BLK:]
        for b in range(batch):
            x = jnp.dot(g, c_ref[b, :, q].reshape(2 * DFT_BLK, D_C), preferred_element_type=F32)
            xre, xim = x[:DFT_BLK], x[DFT_BLK:]
            y = jnp.concatenate([xre * hre - xim * him, xre * him + xim * hre], axis=0)
            d = jnp.dot(gi, y.astype(BF16), preferred_element_type=F32)
            o_ref[b, :, q] = d.reshape(2, DFT_BLK, D_C).astype(BF16)


def _hy_inner(cdata, cfilt, g, gi, kper=2):
    batch = cdata.shape[0]
    kern = functools.partial(_hy_inner_kernel, batch=batch, kper=kper)
    blk = lambda nb: pl.BlockSpec((nb, 2, kper, DFT_BLK, D_C), lambda k: (0, 0, k, 0, 0))
    return pl.pallas_call(
        kern,
        grid=(DFT_NBLK // kper,),
        in_specs=[blk(batch), blk(cfilt.shape[0]),
                  pl.BlockSpec((kper, 2 * DFT_BLK, 2 * DFT_BLK), lambda k: (k, 0, 0)),
                  pl.BlockSpec((kper, 2 * DFT_BLK, 2 * DFT_BLK), lambda k: (k, 0, 0))],
        out_specs=blk(batch),
        out_shape=jax.ShapeDtypeStruct(cdata.shape, BF16),
        compiler_params=_cparams(("parallel",)),
        name="hyena_dft_inner",
    )(cdata, cfilt, g, gi)


def _hy_out_kernel(d_ref, x0_ref, x1_ref, v_ref, w0_ref, w1_ref, wv_ref, b0_ref, b1_ref, bv_ref,
                   hb_ref, kai_ref, o_ref, ys, *, seq):
    nj = seq // DFT_BLK
    for p in range(DFT_BLK // DFT_SUB):
        sl = slice(p * DFT_SUB, (p + 1) * DFT_SUB)
        rhs = d_ref[0, :, :, sl, :].reshape(2 * DFT_NBLK * DFT_SUB, LANES)
        y = jnp.dot(kai_ref[...], rhs, preferred_element_type=F32)
        ys[:, sl, :] = y.reshape(nj, DFT_SUB, LANES)
    hb = hb_ref[...]
    for j in range(nj):
        x0 = _dwconv_block(x0_ref, j, nj, w0_ref[...], b0_ref[...], 1)
        x1 = _dwconv_block(x1_ref, j, nj, w1_ref[...], b1_ref[...], 1)
        v = _dwconv_block(v_ref, j, nj, wv_ref[...], bv_ref[...], 1)
        z = v * x1
        o_ref[j * DFT_BLK:(j + 1) * DFT_BLK, :] = (ys[j] + z * hb) * x0


def _hy_out(dd, pc, cw, cb, hb, kai, batch, seq):
    n_tok = pc.shape[0]
    nh = D_C // LANES
    nj = seq // DFT_BLK
    kern = functools.partial(_hy_out_kernel, seq=seq)
    part = lambda p, r: pl.BlockSpec((r, LANES), lambda b, c: (0, p * nh + c))
    return pl.pallas_call(
        kern,
        grid=(batch, nh),
        in_specs=[
            pl.BlockSpec((1, 2, DFT_NBLK, DFT_BLK, LANES), lambda b, c: (b, 0, 0, 0, c)),
            pl.BlockSpec((seq, LANES), lambda b, c: (b, c)),
            pl.BlockSpec((seq, LANES), lambda b, c: (b, nh + c)),
            pl.BlockSpec((seq, LANES), lambda b, c: (b, 2 * nh + c)),
            part(0, cw.shape[0]), part(1, cw.shape[0]), part(2, cw.shape[0]),
            part(0, 1), part(1, 1), part(2, 1),
            pl.BlockSpec((1, LANES), lambda b, c: (0, c)),
            pl.BlockSpec(kai.shape, lambda b, c: (0, 0)),
        ],
        out_specs=pl.BlockSpec((seq, LANES), lambda b, c: (b, c)),
        out_shape=jax.ShapeDtypeStruct((n_tok, D_C), F32),
        scratch_shapes=[pltpu.VMEM((nj, DFT_BLK, LANES), F32)],
        compiler_params=_cparams(("parallel", "parallel")),
        name="hyena_out",
    )(dd, pc, pc, pc, cw, cw, cw, cb, cb, cb, hb, kai)


def _mix_mlp_kernel(ya_ref, yb_ref, yc_ref, x_ref, ga_ref, gb_ref, gc_ref, wo_ref, g_ref, wu_ref,
                    wd_ref, gf_ref, o_ref, *, ff_chunk, final_norm):
    y = jnp.concatenate([_rms(ya_ref[...], ga_ref[...]), _rms(yb_ref[...], gb_ref[...]),
                         _rms(yc_ref[...], gc_ref[...])], axis=-1).astype(BF16)
    x = x_ref[...] + jnp.dot(y, wo_ref[...], preferred_element_type=F32)
    h = _rms(x, g_ref[...]).astype(BF16)
    acc = x
    for c in range(D_FF // ff_chunk):
        sl = slice(c * ff_chunk, (c + 1) * ff_chunk)
        u = jnp.maximum(jnp.dot(h, wu_ref[:, sl], preferred_element_type=F32), 0.0)
        acc = acc + jnp.dot((u * u).astype(BF16), wd_ref[sl, :], preferred_element_type=F32)
    if final_norm:
        acc = _rms(acc, gf_ref[...])
    o_ref[...] = acc


def _mix_mlp(ya, yb, yc, x2, ga, gb, gc, wo_bf, g, wu_bf, wd_bf, gf, tm, final_norm, ff_chunk=1024):
    n_tok = x2.shape[0]
    kern = functools.partial(_mix_mlp_kernel, ff_chunk=ff_chunk, final_norm=final_norm)
    row = lambda w: pl.BlockSpec((tm, w), lambda i: (i, 0))
    vec = lambda w: pl.BlockSpec((1, w), lambda i: (0, 0))
    resident = lambda r, c: pl.BlockSpec((r, c), lambda i: (0, 0), pipeline_mode=pl.Buffered(1))
    return pl.pallas_call(
        kern,
        grid=(n_tok // tm,),
        in_specs=[row(D_A), row(D_B), row(D_C), row(D_MODEL), vec(D_A), vec(D_B), vec(D_C),
                  resident(D_MODEL, D_MODEL), vec(D_MODEL), resident(D_MODEL, D_FF),
                  resident(D_FF, D_MODEL), vec(D_MODEL)],
        out_specs=row(D_MODEL),
        out_shape=jax.ShapeDtypeStruct((n_tok, D_MODEL), F32),
        compiler_params=_cparams(("parallel",)),
        name="mix_mlp",
    )(ya, yb, yc, x2, ga, gb, gc, wo_bf, g, wu_bf, wd_bf, gf)


def _rope_tables(seq):
    pos = np.arange(seq, dtype=np.float32)
    inv_freq = (np.float32(ROPE_THETA) ** (-np.arange(0, ROT_DIM, 2, dtype=np.float32) / ROT_DIM))
    ang = (pos[:, None] * inv_freq[None, :]).astype(np.float32)
    cos, sin = np.cos(ang).astype(np.float32), np.sin(ang).astype(np.float32)
    rest = HEAD_DIM - ROT_DIM
    c_head = np.concatenate([cos, cos, np.ones((seq, rest), np.float32)], axis=1)
    s_head = np.concatenate([-sin, sin, np.zeros((seq, rest), np.float32)], axis=1)
    scale = np.float32(HEAD_DIM ** -0.5)
    c = np.concatenate([np.tile(c_head, (1, N_Q_HEADS)) * scale, np.tile(c_head, (1, N_KV_HEADS))], axis=1)
    s = np.concatenate([np.tile(s_head, (1, N_Q_HEADS)) * scale, np.tile(s_head, (1, N_KV_HEADS))], axis=1)
    return jnp.asarray(c), jnp.asarray(s)


def _hyena_position_tables(seq):
    t = np.linspace(0.0, 1.0, seq, dtype=np.float32)[:, None]
    w = (2.0 * math.pi * np.arange(seq, dtype=np.float32)[:, None] / seq).astype(np.float32)
    f = np.linspace(1e-4, HY_BANDS - 1, HY_BANDS, dtype=np.float32)[None, :]
    fw = (f * w).astype(np.float32)
    z = np.concatenate([t, np.cos(fw), -np.sin(fw)], axis=-1).astype(np.float32)
    deltas = np.abs(np.linspace(HY_MIN_DECAY, HY_MAX_DECAY, D_C, dtype=np.float32))
    decay = np.exp(-t * deltas[None, :]).astype(np.float32)
    decay_b = decay.copy()
    decay_b[0] = 0.0
    half = seq // 2
    zp = np.zeros((half, 2, HY_WIDTH), np.float32)
    zp[:, 0, :HY_EMB] = z[:half]
    zp[:, 1, :HY_EMB] = z[half:]
    dec4 = np.concatenate([decay[:half], decay_b[:half], decay[half:], decay_b[half:]], axis=1)
    return jnp.asarray(zp.reshape(half, 2 * HY_WIDTH)), jnp.asarray(dec4)


def _blockdiag2(a):
    z = jnp.zeros_like(a)
    return jnp.concatenate([jnp.concatenate([a, z], axis=1), jnp.concatenate([z, a], axis=1)], axis=0)


def _dft_tables(seq):
    n = 2 * seq
    nj = seq // DFT_BLK
    kk = np.arange(DFT_NBLK)
    ang = 2.0 * np.pi * np.outer(kk, np.arange(nj)) / DFT_NBLK
    eye = np.eye(DFT_SUB)
    ka = np.concatenate([np.kron(np.cos(ang), eye), np.kron(-np.sin(ang), eye)], axis=0)
    kai = np.concatenate([np.kron(np.cos(ang).T, eye), np.kron(-np.sin(ang).T, eye)], axis=1) / n
    m = np.arange(DFT_BLK)
    k = kk[:, None, None] + DFT_NBLK * np.arange(DFT_BLK)[None, :, None]
    ph = 2.0 * np.pi * ((k * m[None, None, :]) % n) / n
    gre, gim = np.cos(ph), -np.sin(ph)
    g = np.concatenate([np.concatenate([gre, -gim], axis=2), np.concatenate([gim, gre], axis=2)], axis=1)
    gi = np.transpose(g, (0, 2, 1))
    as_bf = lambda a: jnp.asarray(a.astype(np.float32)).astype(BF16)
    return as_bf(ka), as_bf(kai), as_bf(g), as_bf(gi)


def _lru_blockdiag(w):
    nb = w.shape[1] // 2
    w = w.reshape(2, nb, 2, w.shape[2], w.shape[3])
    z = jnp.zeros_like(w[:, :, 0])
    top = jnp.concatenate([w[:, :, 0], z], axis=-1)
    bot = jnp.concatenate([z, w[:, :, 1]], axis=-1)
    return jnp.concatenate([top, bot], axis=-2)


def kernel(x, norm_mix_g, w_in, conv_a_w, conv_a_b, lru_wa, lru_ba, lru_wx, lru_bx, lru_lambda,
           attn_sink, hy_conv_w, hy_conv_b, hy_w1, hy_b1, hy_freq, hy_w2, hy_b2, hy_w3, hy_bias,
           gnorm_a, gnorm_b, gnorm_c, w_out, norm_mlp_g, w_up, w_down, final_norm_g):
    batch, seq, _ = x.shape
    depth = w_in.shape[0]
    n_tok = batch * seq
    tm = 1024
    tm_mlp = 512
    ng = D_A // LANES

    rc, rs = _rope_tables(seq)
    zemb, dec4 = _hyena_position_tables(seq)
    ka, kai, gtab, gitab = _dft_tables(seq)

    xs = x.reshape(n_tok, D_MODEL)
    for i in range(depth):
        pa, pq, pc = _in_proj(xs, norm_mix_g[i][None], w_in[i].astype(BF16), rc, rs, seq, tm)

        wa, wx = _lru_blockdiag(lru_wa[i]), _lru_blockdiag(lru_wx[i])
        wg = jnp.concatenate([wa[0], wx[0], wa[1], wx[1]], axis=-1).astype(BF16)
        tile = lambda v: v.reshape(ng, 1, LANES)
        bias = jnp.concatenate([tile(lru_ba[i][0]), tile(lru_bx[i][0]),
                                tile(lru_ba[i][1]), tile(lru_bx[i][1])], axis=-1)
        lam = jnp.concatenate([tile(lru_lambda[i][0]), tile(lru_lambda[i][1])], axis=-1)
        y_a = _lru(pa, conv_a_w[i], conv_a_b[i][None], wg, bias, lam, batch, seq)

        y_b = _attention(pq, attn_sink[i], batch, seq)

        w1p = jnp.zeros((HY_WIDTH, HY_WIDTH), F32).at[:HY_EMB].set(hy_w1[i])
        pair = lambda v: jnp.concatenate([v, v])[None]
        filt = _hy_filter(zemb, _blockdiag2(w1p), pair(hy_b1[i]), pair(hy_freq[i]),
                          _blockdiag2(hy_w2[i]), pair(hy_b2[i]), _blockdiag2(hy_w3[i]), dec4)
        cw, cb = hy_conv_w[i], hy_conv_b[i][None]
        cdata = _hy_fwd_data(pc, cw, cb, ka, batch, seq)
        cfilt = _hy_fwd_filt(filt.reshape(2, seq // DFT_BLK, DFT_BLK, D_C), ka)
        dd = _hy_inner(cdata, cfilt, gtab, gitab)
        y_c = _hy_out(dd, pc, cw, cb, hy_bias[i][None], kai, batch, seq)

        xs = _mix_mlp(y_a, y_b, y_c, xs, gnorm_a[i][None], gnorm_b[i][None], gnorm_c[i][None],
                      w_out[i].astype(BF16), norm_mlp_g[i][None], w_up[i].astype(BF16),
                      w_down[i].astype(BF16), final_norm_g[None], tm, final_norm=(i == depth - 1))
    return xs.reshape(batch, seq, D_MODEL)
```

```python
import functools
import math

import numpy as np
import jax
import jax.numpy as jnp
from jax import lax
from jax.experimental import pallas as pl
from jax.experimental.pallas import tpu as pltpu

F32 = jnp.float32
BF16 = jnp.bfloat16

D_MODEL = 1024
D_A = 384
D_B = 384
D_C = 256
HEAD_DIM = 64
N_Q_HEADS = 6
N_KV_HEADS = 2
GROUP = N_Q_HEADS // N_KV_HEADS
D_KV = N_KV_HEADS * HEAD_DIM
D_QKV = D_B + 2 * D_KV
D_IN = 2 * D_A + D_QKV + 3 * D_C
C_LRU = 8.0
WINDOW = 128
BLOCK = 128
ROPE_THETA = 500000.0
ROT_DIM = HEAD_DIM // 4
HY_EMB = 33
HY_BANDS = (HY_EMB - 1) // 2
HY_WIDTH = 64
HY_TARGET = 1e-2
HY_MAX_DECAY = math.log(HY_TARGET) / 0.3
HY_MIN_DECAY = math.log(HY_TARGET) / 1.5
D_FF = 4 * D_MODEL
EPS = 1e-6
NEG = -1e30

LANES = 128
SUBLANES = 8
VMEM_LIMIT = 56 * 1024 * 1024

SCAN_SEGS = SUBLANES
DFT_BLK = 128
DFT_NBLK = 64
DFT_SUB = 16


def _cparams(sem):
    return pltpu.CompilerParams(dimension_semantics=sem, vmem_limit_bytes=VMEM_LIMIT)


def _rms(x, g):
    return x * lax.rsqrt(jnp.mean(x * x, axis=-1, keepdims=True) + EPS) * g


def _in_proj_kernel(x_ref, g_ref, w_ref, rc_ref, rs_ref, oa_ref, oq_ref, ov_ref, oc_ref):
    h = _rms(x_ref[...], g_ref[...]).astype(BF16)
    oa_ref[...] = jnp.dot(h, w_ref[:, :2 * D_A], preferred_element_type=F32)
    oc_ref[...] = jnp.dot(h, w_ref[:, 2 * D_A + D_QKV:], preferred_element_type=F32)
    qkv = jnp.dot(h, w_ref[:, 2 * D_A:2 * D_A + D_QKV], preferred_element_type=F32)
    qk = qkv[:, :D_B + D_KV]
    n = D_B + D_KV
    half = ROT_DIM // 2
    lane = lax.broadcasted_iota(jnp.int32, qk.shape, 1) % HEAD_DIM
    swapped = jnp.where(lane < half, pltpu.roll(qk, n - half, axis=1), pltpu.roll(qk, half, axis=1))
    oq_ref[...] = (qk * rc_ref[...] + swapped * rs_ref[...]).astype(BF16)
    ov_ref[...] = qkv[:, n:].T.astype(BF16)


def _in_proj(x2, g, w_bf, rc, rs, seq, tm):
    n_tok = x2.shape[0]
    nrb = seq // tm
    return pl.pallas_call(
        _in_proj_kernel,
        grid=(n_tok // tm,),
        in_specs=[
            pl.BlockSpec((tm, D_MODEL), lambda i: (i, 0)),
            pl.BlockSpec((1, D_MODEL), lambda i: (0, 0)),
            pl.BlockSpec((D_MODEL, D_IN), lambda i: (0, 0)),
            pl.BlockSpec((tm, D_B + D_KV), lambda i: (i % nrb, 0)),
            pl.BlockSpec((tm, D_B + D_KV), lambda i: (i % nrb, 0)),
        ],
        out_specs=[
            pl.BlockSpec((tm, 2 * D_A), lambda i: (i, 0)),
            pl.BlockSpec((tm, D_B + D_KV), lambda i: (i, 0)),
            pl.BlockSpec((D_KV, tm), lambda i: (0, i)),
            pl.BlockSpec((tm, 3 * D_C), lambda i: (i, 0)),
        ],
        out_shape=[
            jax.ShapeDtypeStruct((n_tok, 2 * D_A), F32),
            jax.ShapeDtypeStruct((n_tok, D_B + D_KV), BF16),
            jax.ShapeDtypeStruct((D_KV, n_tok), BF16),
            jax.ShapeDtypeStruct((n_tok, 3 * D_C), F32),
        ],
        compiler_params=_cparams(("parallel",)),
        name="in_proj",
    )(x2, g, w_bf, rc, rs)


def _dwconv(pad_ref, r0, rows, w, b, pad_left):
    return _conv_taps(pad_ref[pl.ds(r0, rows + 2 * SUBLANES), :], rows, w, b, pad_left)


def _conv_taps(win, rows, w, b, pad_left):
    total = rows + 2 * SUBLANES
    acc = None
    for k in range(w.shape[0]):
        shift = (pad_left - k) % total
        tap = win if shift == 0 else pltpu.roll(win, shift, axis=0)
        term = tap[SUBLANES:SUBLANES + rows] * w[k:k + 1, :]
        acc = term if acc is None else acc + term
    return acc + b


def _dwconv_block(src_ref, j, nblk, w, b, pad_left):
    r0 = j * DFT_BLK
    zeros = jnp.zeros((SUBLANES, src_ref.shape[1]), F32)
    top = zeros if j == 0 else src_ref[r0 - SUBLANES:r0, :]
    bot = zeros if j == nblk - 1 else src_ref[r0 + DFT_BLK:r0 + DFT_BLK + SUBLANES, :]
    win = jnp.concatenate([top, src_ref[r0:r0 + DFT_BLK, :], bot], axis=0)
    return _conv_taps(win, DFT_BLK, w, b, pad_left)


def _fill_padded(pad_ref, src_ref, seq):
    zeros = jnp.zeros((SUBLANES, pad_ref.shape[1]), F32)
    pad_ref[0:SUBLANES, :] = zeros
    pad_ref[seq + SUBLANES:seq + 2 * SUBLANES, :] = zeros
    pad_ref[SUBLANES:seq + SUBLANES, :] = src_ref[...]


def _softplus(x):
    return jnp.maximum(x, 0.0) + jnp.log1p(jnp.exp(-jnp.abs(x)))


def _gelu_tanh(x):
    c = math.sqrt(2.0 / math.pi)
    return x * (0.5 * (1.0 + jnp.tanh(c * (x + 0.044715 * (x * x * x)))))


def _lru_kernel(u_ref, gate_ref, cw_ref, cb_ref, w_ref, bias_ref, lam_ref, o_ref,
                upad, xci, gg, af, bf, ar, br, pfs, hfs, prs, hrs, *, seq, rows, unroll):
    seg = seq // SCAN_SEGS
    tiles = rows // SCAN_SEGS
    nchunk = seq // rows
    per_seg = seg // rows
    _fill_padded(upad, u_ref, seq)
    cw = cw_ref[...]
    cb = cb_ref[...]
    w = w_ref[0] * 0.5
    bias = bias_ref[0] * 0.5
    hnsp = (-0.5 * C_LRU) * _softplus(-lam_ref[0])

    def conv(ci, carry):
        r0 = pl.multiple_of(ci * rows, rows)
        dst = pl.ds((ci % per_seg) * (rows * SCAN_SEGS) + ci // per_seg, rows, stride=SCAN_SEGS)
        xci[dst, :] = _dwconv(upad, r0, rows, cw, cb, 2)
        gg[dst, :] = _gelu_tanh(gate_ref[pl.ds(r0, rows), :])
        return carry

    lax.fori_loop(0, nchunk, conv, 0)

    def gates(ci, carry):
        r0 = pl.multiple_of(ci * rows, rows)
        xc = xci[pl.ds(r0, rows), :]
        t = jnp.tanh(jnp.dot(xc.astype(BF16), w, preferred_element_type=F32) + bias)
        hxc = 0.5 * xc
        for d, (a_ref, b_ref) in enumerate(((af, bf), (ar, br))):
            tr = t[:, (2 * d) * LANES:(2 * d + 1) * LANES]
            ti = t[:, (2 * d + 1) * LANES:(2 * d + 2) * LANES]
            hn = hnsp[:, d * LANES:(d + 1) * LANES]
            log_a = tr * hn + hn
            a = jnp.exp(log_a)
            nem = (-1.0 - a * a) * jnp.tanh(log_a)
            a_ref[pl.ds(r0, rows), :] = a
            b_ref[pl.ds(r0, rows), :] = jnp.sqrt(nem) * (ti * hxc + hxc)
        return carry

    lax.fori_loop(0, nchunk, gates, 0, unroll=2)

    def scan(it, carry):
        pf, hf, pr, hr = carry
        for u in range(unroll):
            i = it * unroll + u
            rf = pl.multiple_of(i * SCAN_SEGS, SCAN_SEGS)
            a = af[pl.ds(rf, SCAN_SEGS), :]
            pf = a * pf
            hf = a * hf + bf[pl.ds(rf, SCAN_SEGS), :]
            pfs[pl.ds(rf, SCAN_SEGS), :] = pf
            hfs[pl.ds(rf, SCAN_SEGS), :] = hf
            rr = pl.multiple_of((seg - 1 - i) * SCAN_SEGS, SCAN_SEGS)
            a = ar[pl.ds(rr, SCAN_SEGS), :]
            pr = a * pr
            hr = a * hr + br[pl.ds(rr, SCAN_SEGS), :]
            prs[pl.ds(rr, SCAN_SEGS), :] = pr
            hrs[pl.ds(rr, SCAN_SEGS), :] = hr
        return pf, hf, pr, hr

    one = jnp.ones((SCAN_SEGS, LANES), F32)
    zero = jnp.zeros((SCAN_SEGS, LANES), F32)
    pf, hf, pr, hr = lax.fori_loop(0, seg // unroll, scan, (one, zero, one, zero))

    sub = lax.broadcasted_iota(jnp.int32, (SCAN_SEGS, LANES), 0)
    cf = zero
    cr = zero
    for _ in range(SCAN_SEGS - 1):
        cf = jnp.where(sub == 0, 0.0, pltpu.roll(hf + pf * cf, 1, axis=0))
        cr = jnp.where(sub == SCAN_SEGS - 1, 0.0, pltpu.roll(hr + pr * cr, SCAN_SEGS - 1, axis=0))

    def combine(ci, carry):
        r0 = pl.multiple_of(ci * rows, rows)
        sl = pl.ds(r0, rows)
        tile3 = lambda ref: ref[sl, :].reshape(tiles, SCAN_SEGS, LANES)
        h = (tile3(hfs) + tile3(pfs) * cf[None]) + (tile3(hrs) + tile3(prs) * cr[None])
        out = h.reshape(rows, LANES) * gg[sl, :]
        for k in range(tiles):
            o_ref[pl.ds(ci * tiles + k, SCAN_SEGS, stride=seg), :] = (
                out[k * SCAN_SEGS:(k + 1) * SCAN_SEGS])
        return carry

    lax.fori_loop(0, nchunk, combine, 0)


def _lru(pa, cw, cb, wg, bias, lam, batch, seq, rows=128, unroll=8):
    n_tok = pa.shape[0]
    ng = D_A // LANES
    kern = functools.partial(_lru_kernel, seq=seq, rows=rows, unroll=unroll)
    return pl.pallas_call(
        kern,
        grid=(batch, ng),
        in_specs=[
            pl.BlockSpec((seq, LANES), lambda b, g: (b, g)),
            pl.BlockSpec((seq, LANES), lambda b, g: (b, ng + g)),
            pl.BlockSpec((cw.shape[0], LANES), lambda b, g: (0, g)),
            pl.BlockSpec((1, LANES), lambda b, g: (0, g)),
            pl.BlockSpec((1, LANES, 4 * LANES), lambda b, g: (g, 0, 0)),
            pl.BlockSpec((1, 1, 4 * LANES), lambda b, g: (g, 0, 0)),
            pl.BlockSpec((1, 1, 2 * LANES), lambda b, g: (g, 0, 0)),
        ],
        out_specs=pl.BlockSpec((seq, LANES), lambda b, g: (b, g)),
        out_shape=jax.ShapeDtypeStruct((n_tok, D_A), F32),
        scratch_shapes=[pltpu.VMEM((seq + 2 * SUBLANES, LANES), F32)]
        + [pltpu.VMEM((seq, LANES), F32) for _ in range(10)],
        compiler_params=_cparams(("parallel", "parallel")),
        name="rglru",
    )(pa, pa, cw, cb, wg, bias, lam)


_ATTN_STRAIGHT = (0, 2, 3, 5)
_ATTN_ROLLED = (1, 4)
_ONES_ROWS = 16


def _attn_kernel(sink_ref, q_ref, k_ref, vt_ref, o_ref, *, seq, tq):
    band = 3 * BLOCK
    nsub = tq // BLOCK
    i = pl.program_id(1)
    ki = lax.broadcasted_iota(jnp.int32, (band, BLOCK), 0)
    qi = lax.broadcasted_iota(jnp.int32, (band, BLOCK), 1)
    lo_q = lax.broadcasted_iota(jnp.int32, (BLOCK, 2 * HEAD_DIM), 1) < HEAD_DIM
    ones = jnp.ones((_ONES_ROWS, band), BF16)
    nt = (((1,), (1,)), ((), ()))

    for jb in range(nsub):
        rows = slice(jb * BLOCK, (jb + 1) * BLOCK)
        q0 = (i * nsub + jb) * BLOCK
        k0 = pl.multiple_of(jnp.clip(q0 - BLOCK, 0, seq - band), BLOCK)
        bias = jnp.where(jnp.abs((q0 + qi) - (k0 + ki)) <= WINDOW, 0.0, NEG)
        kb = k_ref[pl.ds(k0, band), :]
        kbs = pltpu.roll(kb, HEAD_DIM, axis=1)
        vt = vt_ref[:, pl.ds(k0, band)]

        def own_half(h):
            qt = q_ref[rows, (h // 2) * 2 * HEAD_DIM:(h // 2 + 1) * 2 * HEAD_DIM]
            return jnp.where(lo_q, qt, 0.0) if h % 2 == 0 else jnp.where(lo_q, 0.0, qt)

        qa = jnp.concatenate([own_half(h) for h in _ATTN_STRAIGHT], axis=0)
        qb = jnp.concatenate([own_half(h) for h in _ATTN_ROLLED], axis=0)
        sa = lax.dot_general(kb, qa, nt, preferred_element_type=F32)
        sb = lax.dot_general(kbs, qb, nt, preferred_element_type=F32)
        st = {h: sa[:, n * BLOCK:(n + 1) * BLOCK] for n, h in enumerate(_ATTN_STRAIGHT)}
        st.update({h: sb[:, n * BLOCK:(n + 1) * BLOCK] for n, h in enumerate(_ATTN_ROLLED)})
        outs = []
        for kv in range(N_KV_HEADS):
            ps, esk = [], []
            for h in range(kv * GROUP, (kv + 1) * GROUP):
                s = st[h] + bias
                sk = sink_ref[h]
                m = jnp.maximum(jnp.max(s, axis=0, keepdims=True), sk)
                ps.append(jnp.exp(s - m).astype(BF16))
                esk.append(jnp.exp(sk - m))
            lhs = jnp.concatenate([vt[kv * HEAD_DIM:(kv + 1) * HEAD_DIM], ones], axis=0)
            ov = jnp.dot(lhs, jnp.concatenate(ps, axis=1), preferred_element_type=F32)
            res = ov[:HEAD_DIM] / (ov[HEAD_DIM:HEAD_DIM + 1] + jnp.concatenate(esk, axis=1))
            outs += [res[:, g * BLOCK:(g + 1) * BLOCK] for g in range(GROUP)]
        o_ref[rows, :] = jnp.concatenate(outs, axis=0).T


def _attention(pq, vt, sink, batch, seq, tq=512):
    n_tok = pq.shape[0]
    nq = seq // tq
    kern = functools.partial(_attn_kernel, seq=seq, tq=tq)
    return pl.pallas_call(
        kern,
        grid=(batch, nq),
        in_specs=[
            pl.BlockSpec(memory_space=pltpu.SMEM),
            pl.BlockSpec((tq, D_B), lambda b, i: (b * nq + i, 0)),
            pl.BlockSpec((seq, D_KV), lambda b, i: (b, D_B // D_KV)),
            pl.BlockSpec((D_KV, seq), lambda b, i: (0, b)),
        ],
        out_specs=pl.BlockSpec((tq, D_B), lambda b, i: (b * nq + i, 0)),
        out_shape=jax.ShapeDtypeStruct((n_tok, D_B), F32),
        compiler_params=_cparams(("parallel", "arbitrary")),
        name="win_attn",
    )(sink, pq, pq, vt)


def _hy_filter_kernel(z_ref, w1_ref, b1_ref, fr_ref, w2_ref, b2_ref, w3_ref, dec_ref, o_ref):
    hi = lax.Precision.HIGHEST
    fr = fr_ref[...]
    h = jnp.sin(fr * (jnp.dot(z_ref[...], w1_ref[...], preferred_element_type=F32, precision=hi)
                      + b1_ref[...]))
    h = jnp.sin(fr * (jnp.dot(h, w2_ref[...], preferred_element_type=F32, precision=hi) + b2_ref[...]))
    f = jnp.dot(h, w3_ref[...], preferred_element_type=F32, precision=hi) * dec_ref[...]
    for half in range(2):
        for d in range(2):
            c0 = (2 * half + d) * D_C
            o_ref[d, half] = f[:, c0:c0 + D_C]


def _hy_filter(zemb2, w1, b1, fr, w2, b2, w3, dec4, rows=512):
    n = zemb2.shape[0]
    full = lambda a: pl.BlockSpec(a.shape, lambda i: (0,) * a.ndim)
    return pl.pallas_call(
        _hy_filter_kernel,
        grid=(n // rows,),
        in_specs=[pl.BlockSpec((rows, zemb2.shape[1]), lambda i: (i, 0)),
                  full(w1), full(b1), full(fr), full(w2), full(b2), full(w3),
                  pl.BlockSpec((rows, 4 * D_C), lambda i: (i, 0))],
        out_specs=pl.BlockSpec((2, 2, rows, D_C), lambda i: (0, 0, i, 0)),
        out_shape=jax.ShapeDtypeStruct((2, 2, n, D_C), F32),
        compiler_params=_cparams(("parallel",)),
        name="hyena_filter",
    )(zemb2, w1, b1, fr, w2, b2, w3, dec4)


def _dft_outer_fwd(zs, ka_ref, o_ref):
    nj, _, lanes = zs.shape
    for p in range(DFT_BLK // DFT_SUB):
        sl = slice(p * DFT_SUB, (p + 1) * DFT_SUB)
        xg = zs[:, sl, :].reshape(nj * DFT_SUB, lanes).astype(BF16)
        c = jnp.dot(ka_ref[...], xg, preferred_element_type=F32)
        o_ref[0, :, :, sl, :] = c.reshape(2, DFT_NBLK, DFT_SUB, lanes).astype(BF16)


def _hy_fwd_data_kernel(x1_ref, v_ref, cw_ref, cb_ref, ka_ref, o_ref, zs, *, seq):
    nj = seq // DFT_BLK
    cw = cw_ref[...]
    cb = cb_ref[...]
    for j in range(nj):
        x1 = _dwconv_block(x1_ref, j, nj, cw[:, D_C:2 * D_C], cb[:, D_C:2 * D_C], 1)
        v = _dwconv_block(v_ref, j, nj, cw[:, 2 * D_C:], cb[:, 2 * D_C:], 1)
        zs[j] = v * x1
    _dft_outer_fwd(zs, ka_ref, o_ref)


def _hy_fwd_filt_kernel(f_ref, ka_ref, o_ref):
    _dft_outer_fwd(f_ref.at[0], ka_ref, o_ref)


def _hy_fwd_data(pc, cw, cb, ka, batch, seq):
    nj = seq // DFT_BLK
    kern = functools.partial(_hy_fwd_data_kernel, seq=seq)
    return pl.pallas_call(
        kern,
        grid=(batch,),
        in_specs=[
            pl.BlockSpec((seq, D_C), lambda b: (b, 1)),
            pl.BlockSpec((seq, D_C), lambda b: (b, 2)),
            pl.BlockSpec(cw.shape, lambda b: (0, 0)),
            pl.BlockSpec(cb.shape, lambda b: (0, 0)),
            pl.BlockSpec(ka.shape, lambda b: (0, 0)),
        ],
        out_specs=pl.BlockSpec((1, 2, DFT_NBLK, DFT_BLK, D_C), lambda b: (b, 0, 0, 0, 0)),
        out_shape=jax.ShapeDtypeStruct((batch, 2, DFT_NBLK, DFT_BLK, D_C), BF16),
        scratch_shapes=[pltpu.VMEM((nj, DFT_BLK, D_C), F32)],
        compiler_params=_cparams(("parallel",)),
        name="hyena_dft_outer",
    )(pc, pc, cw, cb, ka)


def _hy_fwd_filt(filt4, ka):
    ndir, nj = filt4.shape[0], filt4.shape[1]
    return pl.pallas_call(
        _hy_fwd_filt_kernel,
        grid=(ndir,),
        in_specs=[
            pl.BlockSpec((1, nj, DFT_BLK, D_C), lambda b: (b, 0, 0, 0)),
            pl.BlockSpec(ka.shape, lambda b: (0, 0)),
        ],
        out_specs=pl.BlockSpec((1, 2, DFT_NBLK, DFT_BLK, D_C), lambda b: (b, 0, 0, 0, 0)),
        out_shape=jax.ShapeDtypeStruct((ndir, 2, DFT_NBLK, DFT_BLK, D_C), BF16),
        compiler_params=_cparams(("parallel",)),
        name="hyena_dft_outer_filter",
    )(filt4, ka)


def _hy_inner_kernel(c_ref, f_ref, g_ref, gi_ref, o_ref, *, batch, kper):
    for q in range(kper):
        g = g_ref[q]
        gi = gi_ref[q]
        hf = jnp.dot(g, f_ref[0, :, q].reshape(2 * DFT_BLK, D_C), preferred_element_type=F32)
        hb = jnp.dot(g, f_ref[1, :, q].reshape(2 * DFT_BLK, D_C), preferred_element_type=F32)
        hre = hf[:DFT_BLK] + hb[:DFT_BLK]
        him = hf[DFT_BLK:] - hb[DFT_BLK:]
        for b in range(batch):
            x = jnp.dot(g, c_ref[b, :, q].reshape(2 * DFT_BLK, D_C), preferred_element_type=F32)
            xre, xim = x[:DFT_BLK], x[DFT_BLK:]
            y = jnp.concatenate([xre * hre - xim * him, xre * him + xim * hre], axis=0)
            d = jnp.dot(gi, y.astype(BF16), preferred_element_type=F32)
            o_ref[b, :, q] = d.reshape(2, DFT_BLK, D_C).astype(BF16)


def _hy_inner(cdata, cfilt, g, gi, kper=2):
    batch = cdata.shape[0]
    kern = functools.partial(_hy_inner_kernel, batch=batch, kper=kper)
    blk = lambda nb: pl.BlockSpec((nb, 2, kper, DFT_BLK, D_C), lambda k: (0, 0, k, 0, 0))
    return pl.pallas_call(
        kern,
        grid=(DFT_NBLK // kper,),
        in_specs=[blk(batch), blk(cfilt.shape[0]),
                  pl.BlockSpec((kper, 2 * DFT_BLK, 2 * DFT_BLK), lambda k: (k, 0, 0)),
                  pl.BlockSpec((kper, 2 * DFT_BLK, 2 * DFT_BLK), lambda k: (k, 0, 0))],
        out_specs=blk(batch),
        out_shape=jax.ShapeDtypeStruct(cdata.shape, BF16),
        compiler_params=_cparams(("parallel",)),
        name="hyena_dft_inner",
    )(cdata, cfilt, g, gi)


def _hy_out_kernel(d_ref, x0_ref, x1_ref, v_ref, w0_ref, w1_ref, wv_ref, b0_ref, b1_ref, bv_ref,
                   hb_ref, kai_ref, o_ref, ys, *, seq):
    nj = seq // DFT_BLK
    for p in range(DFT_BLK // DFT_SUB):
        sl = slice(p * DFT_SUB, (p + 1) * DFT_SUB)
        rhs = d_ref[0, :, :, sl, :].reshape(2 * DFT_NBLK * DFT_SUB, LANES)
        y = jnp.dot(kai_ref[...], rhs, preferred_element_type=F32)
        ys[:, sl, :] = y.reshape(nj, DFT_SUB, LANES)
    hb = hb_ref[...]
    for j in range(nj):
        x0 = _dwconv_block(x0_ref, j, nj, w0_ref[...], b0_ref[...], 1)
        x1 = _dwconv_block(x1_ref, j, nj, w1_ref[...], b1_ref[...], 1)
        v = _dwconv_block(v_ref, j, nj, wv_ref[...], bv_ref[...], 1)
        z = v * x1
        o_ref[j * DFT_BLK:(j + 1) * DFT_BLK, :] = (ys[j] + z * hb) * x0


def _hy_out(dd, pc, cw, cb, hb, kai, batch, seq):
    n_tok = pc.shape[0]
    nh = D_C // LANES
    nj = seq // DFT_BLK
    kern = functools.partial(_hy_out_kernel, seq=seq)
    part = lambda p, r: pl.BlockSpec((r, LANES), lambda b, c: (0, p * nh + c))
    return pl.pallas_call(
        kern,
        grid=(batch, nh),
        in_specs=[
            pl.BlockSpec((1, 2, DFT_NBLK, DFT_BLK, LANES), lambda b, c: (b, 0, 0, 0, c)),
            pl.BlockSpec((seq, LANES), lambda b, c: (b, c)),
            pl.BlockSpec((seq, LANES), lambda b, c: (b, nh + c)),
            pl.BlockSpec((seq, LANES), lambda b, c: (b, 2 * nh + c)),
            part(0, cw.shape[0]), part(1, cw.shape[0]), part(2, cw.shape[0]),
            part(0, 1), part(1, 1), part(2, 1),
            pl.BlockSpec((1, LANES), lambda b, c: (0, c)),
            pl.BlockSpec(kai.shape, lambda b, c: (0, 0)),
        ],
        out_specs=pl.BlockSpec((seq, LANES), lambda b, c: (b, c)),
        out_shape=jax.ShapeDtypeStruct((n_tok, D_C), F32),
        scratch_shapes=[pltpu.VMEM((nj, DFT_BLK, LANES), F32)],
        compiler_params=_cparams(("parallel", "parallel")),
        name="hyena_out",
    )(dd, pc, pc, pc, cw, cw, cw, cb, cb, cb, hb, kai)


def _mix_mlp_kernel(ya_ref, yb_ref, yc_ref, x_ref, ga_ref, gb_ref, gc_ref, wo_ref, g_ref, wu_ref,
                    wd_ref, gf_ref, o_ref, *, ff_chunk, final_norm):
    y = jnp.concatenate([_rms(ya_ref[...], ga_ref[...]), _rms(yb_ref[...], gb_ref[...]),
                         _rms(yc_ref[...], gc_ref[...])], axis=-1).astype(BF16)
    x = x_ref[...] + jnp.dot(y, wo_ref[...], preferred_element_type=F32)
    h = _rms(x, g_ref[...]).astype(BF16)
    acc = x
    for c in range(D_FF // ff_chunk):
        sl = slice(c * ff_chunk, (c + 1) * ff_chunk)
        u = jnp.maximum(jnp.dot(h, wu_ref[:, sl], preferred_element_type=F32), 0.0)
        acc = acc + jnp.dot((u * u).astype(BF16), wd_ref[sl, :], preferred_element_type=F32)
    if final_norm:
        acc = _rms(acc, gf_ref[...])
    o_ref[...] = acc


def _mix_mlp(ya, yb, yc, x2, ga, gb, gc, wo_bf, g, wu_bf, wd_bf, gf, tm, final_norm, ff_chunk=1024):
    n_tok = x2.shape[0]
    kern = functools.partial(_mix_mlp_kernel, ff_chunk=ff_chunk, final_norm=final_norm)
    row = lambda w: pl.BlockSpec((tm, w), lambda i: (i, 0))
    vec = lambda w: pl.BlockSpec((1, w), lambda i: (0, 0))
    resident = lambda r, c: pl.BlockSpec((r, c), lambda i: (0, 0), pipeline_mode=pl.Buffered(1))
    return pl.pallas_call(
        kern,
        grid=(n_tok // tm,),
        in_specs=[row(D_A), row(D_B), row(D_C), row(D_MODEL), vec(D_A), vec(D_B), vec(D_C),
                  resident(D_MODEL, D_MODEL), vec(D_MODEL), resident(D_MODEL, D_FF),
                  resident(D_FF, D_MODEL), vec(D_MODEL)],
        out_specs=row(D_MODEL),
        out_shape=jax.ShapeDtypeStruct((n_tok, D_MODEL), F32),
        compiler_params=_cparams(("parallel",)),
        name="mix_mlp",
    )(ya, yb, yc, x2, ga, gb, gc, wo_bf, g, wu_bf, wd_bf, gf)


def _rope_tables(seq):
    pos = np.arange(seq, dtype=np.float32)
    inv_freq = (np.float32(ROPE_THETA) ** (-np.arange(0, ROT_DIM, 2, dtype=np.float32) / ROT_DIM))
    ang = (pos[:, None] * inv_freq[None, :]).astype(np.float32)
    cos, sin = np.cos(ang).astype(np.float32), np.sin(ang).astype(np.float32)
    rest = HEAD_DIM - ROT_DIM
    c_head = np.concatenate([cos, cos, np.ones((seq, rest), np.float32)], axis=1)
    s_head = np.concatenate([-sin, sin, np.zeros((seq, rest), np.float32)], axis=1)
    scale = np.float32(HEAD_DIM ** -0.5)
    c = np.concatenate([np.tile(c_head, (1, N_Q_HEADS)) * scale, np.tile(c_head, (1, N_KV_HEADS))], axis=1)
    s = np.concatenate([np.tile(s_head, (1, N_Q_HEADS)) * scale, np.tile(s_head, (1, N_KV_HEADS))], axis=1)
    return jnp.asarray(c), jnp.asarray(s)


def _hyena_position_tables(seq):
    t = np.linspace(0.0, 1.0, seq, dtype=np.float32)[:, None]
    w = (2.0 * math.pi * np.arange(seq, dtype=np.float32)[:, None] / seq).astype(np.float32)
    f = np.linspace(1e-4, HY_BANDS - 1, HY_BANDS, dtype=np.float32)[None, :]
    fw = (f * w).astype(np.float32)
    z = np.concatenate([t, np.cos(fw), -np.sin(fw)], axis=-1).astype(np.float32)
    deltas = np.abs(np.linspace(HY_MIN_DECAY, HY_MAX_DECAY, D_C, dtype=np.float32))
    decay = np.exp(-t * deltas[None, :]).astype(np.float32)
    decay_b = decay.copy()
    decay_b[0] = 0.0
    half = seq // 2
    zp = np.zeros((half, 2, HY_WIDTH), np.float32)
    zp[:, 0, :HY_EMB] = z[:half]
    zp[:, 1, :HY_EMB] = z[half:]
    dec4 = np.concatenate([decay[:half], decay_b[:half], decay[half:], decay_b[half:]], axis=1)
    return jnp.asarray(zp.reshape(half, 2 * HY_WIDTH)), jnp.asarray(dec4)


def _blockdiag2(a):
    z = jnp.zeros_like(a)
    return jnp.concatenate([jnp.concatenate([a, z], axis=1), jnp.concatenate([z, a], axis=1)], axis=0)


def _dft_tables(seq):
    n = 2 * seq
    nj = seq // DFT_BLK
    kk = np.arange(DFT_NBLK)
    ang = 2.0 * np.pi * np.outer(kk, np.arange(nj)) / DFT_NBLK
    eye = np.eye(DFT_SUB)
    ka = np.concatenate([np.kron(np.cos(ang), eye), np.kron(-np.sin(ang), eye)], axis=0)
    kai = np.concatenate([np.kron(np.cos(ang).T, eye), np.kron(-np.sin(ang).T, eye)], axis=1) / n
    m = np.arange(DFT_BLK)
    k = kk[:, None, None] + DFT_NBLK * np.arange(DFT_BLK)[None, :, None]
    ph = 2.0 * np.pi * ((k * m[None, None, :]) % n) / n
    gre, gim = np.cos(ph), -np.sin(ph)
    g = np.concatenate([np.concatenate([gre, -gim], axis=2), np.concatenate([gim, gre], axis=2)], axis=1)
    gi = np.transpose(g, (0, 2, 1))
    as_bf = lambda a: jnp.asarray(a.astype(np.float32)).astype(BF16)
    return as_bf(ka), as_bf(kai), as_bf(g), as_bf(gi)


def _lru_blockdiag(w):
    nb = w.shape[1] // 2
    w = w.reshape(2, nb, 2, w.shape[2], w.shape[3])
    z = jnp.zeros_like(w[:, :, 0])
    top = jnp.concatenate([w[:, :, 0], z], axis=-1)
    bot = jnp.concatenate([z, w[:, :, 1]], axis=-1)
    return jnp.concatenate([top, bot], axis=-2)


def kernel(x, norm_mix_g, w_in, conv_a_w, conv_a_b, lru_wa, lru_ba, lru_wx, lru_bx, lru_lambda,
           attn_sink, hy_conv_w, hy_conv_b, hy_w1, hy_b1, hy_freq, hy_w2, hy_b2, hy_w3, hy_bias,
           gnorm_a, gnorm_b, gnorm_c, w_out, norm_mlp_g, w_up, w_down, final_norm_g):
    batch, seq, _ = x.shape
    depth = w_in.shape[0]
    n_tok = batch * seq
    tm = 1024
    tm_mlp = 512
    ng = D_A // LANES

    rc, rs = _rope_tables(seq)
    zemb, dec4 = _hyena_position_tables(seq)
    ka, kai, gtab, gitab = _dft_tables(seq)

    xs = x.reshape(n_tok, D_MODEL)
    for i in range(depth):
        pa, pq, vt, pc = _in_proj(xs, norm_mix_g[i][None], w_in[i].astype(BF16), rc, rs, seq, tm)

        wa, wx = _lru_blockdiag(lru_wa[i]), _lru_blockdiag(lru_wx[i])
        wg = jnp.concatenate([wa[0], wx[0], wa[1], wx[1]], axis=-1).astype(BF16)
        tile = lambda v: v.reshape(ng, 1, LANES)
        bias = jnp.concatenate([tile(lru_ba[i][0]), tile(lru_bx[i][0]),
                                tile(lru_ba[i][1]), tile(lru_bx[i][1])], axis=-1)
        lam = jnp.concatenate([tile(lru_lambda[i][0]), tile(lru_lambda[i][1])], axis=-1)
        y_a = _lru(pa, conv_a_w[i], conv_a_b[i][None], wg, bias, lam, batch, seq)

        y_b = _attention(pq, vt, attn_sink[i], batch, seq)

        w1p = jnp.zeros((HY_WIDTH, HY_WIDTH), F32).at[:HY_EMB].set(hy_w1[i])
        pair = lambda v: jnp.concatenate([v, v])[None]
        filt = _hy_filter(zemb, _blockdiag2(w1p), pair(hy_b1[i]), pair(hy_freq[i]),
                          _blockdiag2(hy_w2[i]), pair(hy_b2[i]), _blockdiag2(hy_w3[i]), dec4)
        cw, cb = hy_conv_w[i], hy_conv_b[i][None]
        cdata = _hy_fwd_data(pc, cw, cb, ka, batch, seq)
        cfilt = _hy_fwd_filt(filt.reshape(2, seq // DFT_BLK, DFT_BLK, D_C), ka)
        dd = _hy_inner(cdata, cfilt, gtab, gitab)
        y_c = _hy_out(dd, pc, cw, cb, hy_bias[i][None], kai, batch, seq)

        xs = _mix_mlp(y_a, y_b, y_c, xs, gnorm_a[i][None], gnorm_b[i][None], gnorm_c[i][None],
                      w_out[i].astype(BF16), norm_mlp_g[i][None], w_up[i].astype(BF16),
                      w_down[i].astype(BF16), final_norm_g[None], tm, final_norm=(i == depth - 1))
    return xs.reshape(batch, seq, D_MODEL)
```

```python
import functools
import math

import numpy as np
import jax
import jax.numpy as jnp
from jax import lax
from jax.experimental import pallas as pl
from jax.experimental.pallas import tpu as pltpu

F32 = jnp.float32
BF16 = jnp.bfloat16

D_MODEL = 1024
D_A = 384
D_B = 384
D_C = 256
HEAD_DIM = 64
N_Q_HEADS = 6
N_KV_HEADS = 2
GROUP = N_Q_HEADS // N_KV_HEADS
D_KV = N_KV_HEADS * HEAD_DIM
D_QKV = D_B + 2 * D_KV
D_IN = 2 * D_A + D_QKV + 3 * D_C
C_LRU = 8.0
WINDOW = 128
BLOCK = 128
ROPE_THETA = 500000.0
ROT_DIM = HEAD_DIM // 4
HY_EMB = 33
HY_BANDS = (HY_EMB - 1) // 2
HY_WIDTH = 64
HY_TARGET = 1e-2
HY_MAX_DECAY = math.log(HY_TARGET) / 0.3
HY_MIN_DECAY = math.log(HY_TARGET) / 1.5
D_FF = 4 * D_MODEL
EPS = 1e-6
NEG = -1e30

LANES = 128
SUBLANES = 8
VMEM_LIMIT = 56 * 1024 * 1024

SCAN_SEGS = SUBLANES
DFT_BLK = 128
DFT_NBLK = 64
DFT_SUB = 16
IN_PROJ_CHUNKS = 2


def _cparams(sem):
    return pltpu.CompilerParams(dimension_semantics=sem, vmem_limit_bytes=VMEM_LIMIT)


def _rms(x, g):
    return x * lax.rsqrt(jnp.mean(x * x, axis=-1, keepdims=True) + EPS) * g


def _in_proj_kernel(x_ref, g_ref, w_ref, rc_ref, rs_ref, oa_ref, oq_ref, ov_ref, oc_ref):
    n = D_B + D_KV
    half = ROT_DIM // 2
    tm = x_ref.shape[0]
    for c in range(IN_PROJ_CHUNKS):
        r = slice(c * tm // IN_PROJ_CHUNKS, (c + 1) * tm // IN_PROJ_CHUNKS)
        h = _rms(x_ref[r, :], g_ref[...]).astype(BF16)
        oa_ref[r, :] = jnp.dot(h, w_ref[:, :2 * D_A], preferred_element_type=F32).astype(BF16)
        oc_ref[r, :] = jnp.dot(h, w_ref[:, 2 * D_A + D_QKV:], preferred_element_type=F32).astype(BF16)
        qkv = jnp.dot(h, w_ref[:, 2 * D_A:2 * D_A + D_QKV], preferred_element_type=F32)
        qk = qkv[:, :n]
        lane = lax.broadcasted_iota(jnp.int32, qk.shape, 1) % HEAD_DIM
        swapped = jnp.where(lane < half, pltpu.roll(qk, n - half, axis=1), pltpu.roll(qk, half, axis=1))
        oq_ref[r, :] = (qk * rc_ref[r, :] + swapped * rs_ref[r, :]).astype(BF16)
        ov_ref[:, r] = qkv[:, n:].T.astype(BF16)


def _in_proj(x2, g, w_bf, rc, rs, seq, tm):
    n_tok = x2.shape[0]
    nrb = seq // tm
    return pl.pallas_call(
        _in_proj_kernel,
        grid=(n_tok // tm,),
        in_specs=[
            pl.BlockSpec((tm, D_MODEL), lambda i: (i, 0)),
            pl.BlockSpec((1, D_MODEL), lambda i: (0, 0)),
            pl.BlockSpec((D_MODEL, D_IN), lambda i: (0, 0)),
            pl.BlockSpec((tm, D_B + D_KV), lambda i: (i % nrb, 0)),
            pl.BlockSpec((tm, D_B + D_KV), lambda i: (i % nrb, 0)),
        ],
        out_specs=[
            pl.BlockSpec((tm, 2 * D_A), lambda i: (i, 0)),
            pl.BlockSpec((tm, D_B + D_KV), lambda i: (i, 0)),
            pl.BlockSpec((D_KV, tm), lambda i: (0, i)),
            pl.BlockSpec((tm, 3 * D_C), lambda i: (i, 0)),
        ],
        out_shape=[
            jax.ShapeDtypeStruct((n_tok, 2 * D_A), BF16),
            jax.ShapeDtypeStruct((n_tok, D_B + D_KV), BF16),
            jax.ShapeDtypeStruct((D_KV, n_tok), BF16),
            jax.ShapeDtypeStruct((n_tok, 3 * D_C), BF16),
        ],
        compiler_params=_cparams(("parallel",)),
        name="in_proj",
    )(x2, g, w_bf, rc, rs)


def _dwconv(pad_ref, r0, rows, w, b, pad_left):
    return _conv_taps(pad_ref[pl.ds(r0, rows + 2 * SUBLANES), :], rows, w, b, pad_left, SUBLANES)


def _conv_taps(win, rows, w, b, pad_left, halo):
    total = rows + 2 * halo
    acc = None
    for k in range(w.shape[0]):
        shift = (pad_left - k) % total
        tap = win if shift == 0 else pltpu.roll(win, shift, axis=0)
        term = tap[halo:halo + rows] * w[k:k + 1, :]
        acc = term if acc is None else acc + term
    return acc + b


def _dwconv_block(src_ref, j, nblk, w, b, pad_left):
    r0 = j * DFT_BLK
    halo = 2 * SUBLANES
    zeros = jnp.zeros((halo, src_ref.shape[1]), src_ref.dtype)
    top = zeros if j == 0 else src_ref[r0 - halo:r0, :]
    bot = zeros if j == nblk - 1 else src_ref[r0 + DFT_BLK:r0 + DFT_BLK + halo, :]
    win = jnp.concatenate([top, src_ref[r0:r0 + DFT_BLK, :], bot], axis=0).astype(F32)
    return _conv_taps(win, DFT_BLK, w, b, pad_left, halo)


def _fill_padded(pad_ref, src_ref, seq):
    zeros = jnp.zeros((SUBLANES, pad_ref.shape[1]), F32)
    pad_ref[0:SUBLANES, :] = zeros
    pad_ref[seq + SUBLANES:seq + 2 * SUBLANES, :] = zeros
    pad_ref[SUBLANES:seq + SUBLANES, :] = src_ref[...].astype(F32)


def _softplus(x):
    return jnp.maximum(x, 0.0) + jnp.log1p(jnp.exp(-jnp.abs(x)))


def _gelu_tanh(x):
    c = math.sqrt(2.0 / math.pi)
    return x * (0.5 * (1.0 + jnp.tanh(c * (x + 0.044715 * (x * x * x)))))


def _lru_kernel(u_ref, gate_ref, cw_ref, cb_ref, w_ref, bias_ref, lam_ref, o_ref,
                upad, xci, gg, af, bf, ar, br, pfs, hfs, prs, hrs, *, seq, rows, unroll):
    seg = seq // SCAN_SEGS
    tiles = rows // SCAN_SEGS
    nchunk = seq // rows
    per_seg = seg // rows
    _fill_padded(upad, u_ref, seq)
    cw = cw_ref[...]
    cb = cb_ref[...]
    w = w_ref[0] * 0.5
    bias = bias_ref[0] * 0.5
    hnsp = (-0.5 * C_LRU) * _softplus(-lam_ref[0])

    def conv(ci, carry):
        r0 = pl.multiple_of(ci * rows, rows)
        dst = pl.ds((ci % per_seg) * (rows * SCAN_SEGS) + ci // per_seg, rows, stride=SCAN_SEGS)
        xci[dst, :] = _dwconv(upad, r0, rows, cw, cb, 2)
        gg[dst, :] = _gelu_tanh(gate_ref[pl.ds(r0, rows), :].astype(F32))
        return carry

    lax.fori_loop(0, nchunk, conv, 0)

    def gates(ci, carry):
        r0 = pl.multiple_of(ci * rows, rows)
        xc = xci[pl.ds(r0, rows), :]
        t = jnp.tanh(jnp.dot(xc.astype(BF16), w, preferred_element_type=F32) + bias)
        hxc = 0.5 * xc
        for d, (a_ref, b_ref) in enumerate(((af, bf), (ar, br))):
            tr = t[:, (2 * d) * LANES:(2 * d + 1) * LANES]
            ti = t[:, (2 * d + 1) * LANES:(2 * d + 2) * LANES]
            hn = hnsp[:, d * LANES:(d + 1) * LANES]
            log_a = tr * hn + hn
            a = jnp.exp(log_a)
            nem = (-1.0 - a * a) * jnp.tanh(log_a)
            a_ref[pl.ds(r0, rows), :] = a
            b_ref[pl.ds(r0, rows), :] = jnp.sqrt(nem) * (ti * hxc + hxc)
        return carry

    lax.fori_loop(0, nchunk, gates, 0, unroll=2)

    def scan(it, carry):
        pf, hf, pr, hr = carry
        for u in range(unroll):
            i = it * unroll + u
            rf = pl.multiple_of(i * SCAN_SEGS, SCAN_SEGS)
            a = af[pl.ds(rf, SCAN_SEGS), :]
            pf = a * pf
            hf = a * hf + bf[pl.ds(rf, SCAN_SEGS), :]
            pfs[pl.ds(rf, SCAN_SEGS), :] = pf
            hfs[pl.ds(rf, SCAN_SEGS), :] = hf
            rr = pl.multiple_of((seg - 1 - i) * SCAN_SEGS, SCAN_SEGS)
            a = ar[pl.ds(rr, SCAN_SEGS), :]
            pr = a * pr
            hr = a * hr + br[pl.ds(rr, SCAN_SEGS), :]
            prs[pl.ds(rr, SCAN_SEGS), :] = pr
            hrs[pl.ds(rr, SCAN_SEGS), :] = hr
        return pf, hf, pr, hr

    one = jnp.ones((SCAN_SEGS, LANES), F32)
    zero = jnp.zeros((SCAN_SEGS, LANES), F32)
    pf, hf, pr, hr = lax.fori_loop(0, seg // unroll, scan, (one, zero, one, zero))

    sub = lax.broadcasted_iota(jnp.int32, (SCAN_SEGS, LANES), 0)
    cf = zero
    cr = zero
    for _ in range(SCAN_SEGS - 1):
        cf = jnp.where(sub == 0, 0.0, pltpu.roll(hf + pf * cf, 1, axis=0))
        cr = jnp.where(sub == SCAN_SEGS - 1, 0.0, pltpu.roll(hr + pr * cr, SCAN_SEGS - 1, axis=0))

    def combine(ci, carry):
        r0 = pl.multiple_of(ci * rows, rows)
        sl = pl.ds(r0, rows)
        tile3 = lambda ref: ref[sl, :].reshape(tiles, SCAN_SEGS, LANES)
        h = (tile3(hfs) + tile3(pfs) * cf[None]) + (tile3(hrs) + tile3(prs) * cr[None])
        out = h.reshape(rows, LANES) * gg[sl, :]
        for k in range(tiles):
            o_ref[pl.ds(ci * tiles + k, SCAN_SEGS, stride=seg), :] = (
                out[k * SCAN_SEGS:(k + 1) * SCAN_SEGS])
        return carry

    lax.fori_loop(0, nchunk, combine, 0)


def _lru(pa, cw, cb, wg, bias, lam, batch, seq, rows=128, unroll=8):
    n_tok = pa.shape[0]
    ng = D_A // LANES
    kern = functools.partial(_lru_kernel, seq=seq, rows=rows, unroll=unroll)
    return pl.pallas_call(
        kern,
        grid=(batch, ng),
        in_specs=[
            pl.BlockSpec((seq, LANES), lambda b, g: (b, g)),
            pl.BlockSpec((seq, LANES), lambda b, g: (b, ng + g)),
            pl.BlockSpec((cw.shape[0], LANES), lambda b, g: (0, g)),
            pl.BlockSpec((1, LANES), lambda b, g: (0, g)),
            pl.BlockSpec((1, LANES, 4 * LANES), lambda b, g: (g, 0, 0)),
            pl.BlockSpec((1, 1, 4 * LANES), lambda b, g: (g, 0, 0)),
            pl.BlockSpec((1, 1, 2 * LANES), lambda b, g: (g, 0, 0)),
        ],
        out_specs=pl.BlockSpec((seq, LANES), lambda b, g: (b, g)),
        out_shape=jax.ShapeDtypeStruct((n_tok, D_A), F32),
        scratch_shapes=[pltpu.VMEM((seq + 2 * SUBLANES, LANES), F32)]
        + [pltpu.VMEM((seq, LANES), F32) for _ in range(10)],
        compiler_params=_cparams(("parallel", "parallel")),
        name="rglru",
    )(pa, pa, cw, cb, wg, bias, lam)


_ATTN_STRAIGHT = (0, 2, 3, 5)
_ATTN_ROLLED = (1, 4)
_ONES_ROWS = 16


def _attn_kernel(sink_ref, q_ref, k_ref, vt_ref, o_ref, *, seq, tq):
    band = 3 * BLOCK
    nsub = tq // BLOCK
    i = pl.program_id(1)
    ki = lax.broadcasted_iota(jnp.int32, (band, BLOCK), 0)
    qi = lax.broadcasted_iota(jnp.int32, (band, BLOCK), 1)
    lo_q = lax.broadcasted_iota(jnp.int32, (BLOCK, 2 * HEAD_DIM), 1) < HEAD_DIM
    ones = jnp.ones((_ONES_ROWS, band), BF16)
    nt = (((1,), (1,)), ((), ()))

    for jb in range(nsub):
        rows = slice(jb * BLOCK, (jb + 1) * BLOCK)
        q0 = (i * nsub + jb) * BLOCK
        k0 = pl.multiple_of(jnp.clip(q0 - BLOCK, 0, seq - band), BLOCK)
        bias = jnp.where(jnp.abs((q0 + qi) - (k0 + ki)) <= WINDOW, 0.0, NEG)
        kb = k_ref[pl.ds(k0, band), :]
        kbs = pltpu.roll(kb, HEAD_DIM, axis=1)
        vt = vt_ref[:, pl.ds(k0, band)]

        def own_half(h):
            qt = q_ref[rows, (h // 2) * 2 * HEAD_DIM:(h // 2 + 1) * 2 * HEAD_DIM]
            return jnp.where(lo_q, qt, 0.0) if h % 2 == 0 else jnp.where(lo_q, 0.0, qt)

        qa = jnp.concatenate([own_half(h) for h in _ATTN_STRAIGHT], axis=0)
        qb = jnp.concatenate([own_half(h) for h in _ATTN_ROLLED], axis=0)
        sa = lax.dot_general(kb, qa, nt, preferred_element_type=F32)
        sb = lax.dot_general(kbs, qb, nt, preferred_element_type=F32)
        st = {h: sa[:, n * BLOCK:(n + 1) * BLOCK] for n, h in enumerate(_ATTN_STRAIGHT)}
        st.update({h: sb[:, n * BLOCK:(n + 1) * BLOCK] for n, h in enumerate(_ATTN_ROLLED)})
        outs = []
        for kv in range(N_KV_HEADS):
            ps, esk = [], []
            for h in range(kv * GROUP, (kv + 1) * GROUP):
                s = st[h] + bias
                sk = sink_ref[h]
                m = jnp.maximum(jnp.max(s, axis=0, keepdims=True), sk)
                ps.append(jnp.exp(s - m).astype(BF16))
                esk.append(jnp.exp(sk - m))
            lhs = jnp.concatenate([vt[kv * HEAD_DIM:(kv + 1) * HEAD_DIM], ones], axis=0)
            ov = jnp.dot(lhs, jnp.concatenate(ps, axis=1), preferred_element_type=F32)
            res = ov[:HEAD_DIM] / (ov[HEAD_DIM:HEAD_DIM + 1] + jnp.concatenate(esk, axis=1))
            outs += [res[:, g * BLOCK:(g + 1) * BLOCK] for g in range(GROUP)]
        o_ref[rows, :] = jnp.concatenate(outs, axis=0).T


def _attention(pq, vt, sink, batch, seq, tq=512):
    n_tok = pq.shape[0]
    nq = seq // tq
    kern = functools.partial(_attn_kernel, seq=seq, tq=tq)
    return pl.pallas_call(
        kern,
        grid=(batch, nq),
        in_specs=[
            pl.BlockSpec(memory_space=pltpu.SMEM),
            pl.BlockSpec((tq, D_B), lambda b, i: (b * nq + i, 0)),
            pl.BlockSpec((seq, D_KV), lambda b, i: (b, D_B // D_KV)),
            pl.BlockSpec((D_KV, seq), lambda b, i: (0, b)),
        ],
        out_specs=pl.BlockSpec((tq, D_B), lambda b, i: (b * nq + i, 0)),
        out_shape=jax.ShapeDtypeStruct((n_tok, D_B), F32),
        compiler_params=_cparams(("parallel", "arbitrary")),
        name="win_attn",
    )(sink, pq, pq, vt)


def _hy_filter_kernel(z_ref, w1_ref, b1_ref, fr_ref, w2_ref, b2_ref, w3_ref, dec_ref, o_ref):
    hi = lax.Precision.HIGHEST
    fr = fr_ref[...]
    h = jnp.sin(fr * (jnp.dot(z_ref[...], w1_ref[...], preferred_element_type=F32, precision=hi)
                      + b1_ref[...]))
    h = jnp.sin(fr * (jnp.dot(h, w2_ref[...], preferred_element_type=F32, precision=hi) + b2_ref[...]))
    f = jnp.dot(h, w3_ref[...], preferred_element_type=F32, precision=hi) * dec_ref[...]
    for half in range(2):
        for d in range(2):
            c0 = (2 * half + d) * D_C
            o_ref[d, half] = f[:, c0:c0 + D_C]


def _hy_filter(zemb2, w1, b1, fr, w2, b2, w3, dec4, rows=512):
    n = zemb2.shape[0]
    full = lambda a: pl.BlockSpec(a.shape, lambda i: (0,) * a.ndim)
    return pl.pallas_call(
        _hy_filter_kernel,
        grid=(n // rows,),
        in_specs=[pl.BlockSpec((rows, zemb2.shape[1]), lambda i: (i, 0)),
                  full(w1), full(b1), full(fr), full(w2), full(b2), full(w3),
                  pl.BlockSpec((rows, 4 * D_C), lambda i: (i, 0))],
        out_specs=pl.BlockSpec((2, 2, rows, D_C), lambda i: (0, 0, i, 0)),
        out_shape=jax.ShapeDtypeStruct((2, 2, n, D_C), F32),
        compiler_params=_cparams(("parallel",)),
        name="hyena_filter",
    )(zemb2, w1, b1, fr, w2, b2, w3, dec4)


def _dft_outer_fwd(zs, ka_ref, o_ref):
    nj, _, lanes = zs.shape
    for p in range(DFT_BLK // DFT_SUB):
        sl = slice(p * DFT_SUB, (p + 1) * DFT_SUB)
        xg = zs[:, sl, :].reshape(nj * DFT_SUB, lanes).astype(BF16)
        c = jnp.dot(ka_ref[...], xg, preferred_element_type=F32)
        o_ref[0, :, :, sl, :] = c.reshape(2, DFT_NBLK, DFT_SUB, lanes).astype(BF16)


def _hy_fwd_data_kernel(x1_ref, v_ref, cw_ref, cb_ref, ka_ref, o_ref, zs, *, seq):
    nj = seq // DFT_BLK
    cw = cw_ref[...]
    cb = cb_ref[...]
    for j in range(nj):
        x1 = _dwconv_block(x1_ref, j, nj, cw[:, D_C:2 * D_C], cb[:, D_C:2 * D_C], 1)
        v = _dwconv_block(v_ref, j, nj, cw[:, 2 * D_C:], cb[:, 2 * D_C:], 1)
        zs[j] = v * x1
    _dft_outer_fwd(zs, ka_ref, o_ref)


def _hy_fwd_filt_kernel(f_ref, ka_ref, o_ref):
    _dft_outer_fwd(f_ref.at[0], ka_ref, o_ref)


def _hy_fwd_data(pc, cw, cb, ka, batch, seq):
    nj = seq // DFT_BLK
    kern = functools.partial(_hy_fwd_data_kernel, seq=seq)
    return pl.pallas_call(
        kern,
        grid=(batch,),
        in_specs=[
            pl.BlockSpec((seq, D_C), lambda b: (b, 1)),
            pl.BlockSpec((seq, D_C), lambda b: (b, 2)),
            pl.BlockSpec(cw.shape, lambda b: (0, 0)),
            pl.BlockSpec(cb.shape, lambda b: (0, 0)),
            pl.BlockSpec(ka.shape, lambda b: (0, 0)),
        ],
        out_specs=pl.BlockSpec((1, 2, DFT_NBLK, DFT_BLK, D_C), lambda b: (b, 0, 0, 0, 0)),
        out_shape=jax.ShapeDtypeStruct((batch, 2, DFT_NBLK, DFT_BLK, D_C), BF16),
        scratch_shapes=[pltpu.VMEM((nj, DFT_BLK, D_C), F32)],
        compiler_params=_cparams(("parallel",)),
        name="hyena_dft_outer",
    )(pc, pc, cw, cb, ka)


def _hy_fwd_filt(filt4, ka):
    ndir, nj = filt4.shape[0], filt4.shape[1]
    return pl.pallas_call(
        _hy_fwd_filt_kernel,
        grid=(ndir,),
        in_specs=[
            pl.BlockSpec((1, nj, DFT_BLK, D_C), lambda b: (b, 0, 0, 0)),
            pl.BlockSpec(ka.shape, lambda b: (0, 0)),
        ],
        out_specs=pl.BlockSpec((1, 2, DFT_NBLK, DFT_BLK, D_C), lambda b: (b, 0, 0, 0, 0)),
        out_shape=jax.ShapeDtypeStruct((ndir, 2, DFT_NBLK, DFT_BLK, D_C), BF16),
        compiler_params=_cparams(("parallel",)),
        name="hyena_dft_outer_filter",
    )(filt4, ka)


def _hy_inner_kernel(c_ref, f_ref, g_ref, gi_ref, o_ref, *, batch, kper):
    for q in range(kper):
        g = g_ref[q]
        gi = gi_ref[q]
        hf = jnp.dot(g, f_ref[0, :, q].reshape(2 * DFT_BLK, D_C), preferred_element_type=F32)
        hb = jnp.dot(g, f_ref[1, :, q].reshape(2 * DFT_BLK, D_C), preferred_element_type=F32)
        hre = hf[:DFT_BLK] + hb[:DFT_BLK]
        him = hf[DFT_BLK:] - hb[DFT_BLK:]
        for b in range(batch):
            x = jnp.dot(g, c_ref[b, :, q].reshape(2 * DFT_BLK, D_C), preferred_element_type=F32)
            xre, xim = x[:DFT_BLK], x[DFT_BLK:]
            y = jnp.concatenate([xre * hre - xim * him, xre * him + xim * hre], axis=0)
            d = jnp.dot(gi, y.astype(BF16), preferred_element_type=F32)
            o_ref[b, :, q] = d.reshape(2, DFT_BLK, D_C).astype(BF16)


def _hy_inner(cdata, cfilt, g, gi, kper=2):
    batch = cdata.shape[0]
    kern = functools.partial(_hy_inner_kernel, batch=batch, kper=kper)
    blk = lambda nb: pl.BlockSpec((nb, 2, kper, DFT_BLK, D_C), lambda k: (0, 0, k, 0, 0))
    return pl.pallas_call(
        kern,
        grid=(DFT_NBLK // kper,),
        in_specs=[blk(batch), blk(cfilt.shape[0]),
                  pl.BlockSpec((kper, 2 * DFT_BLK, 2 * DFT_BLK), lambda k: (k, 0, 0)),
                  pl.BlockSpec((kper, 2 * DFT_BLK, 2 * DFT_BLK), lambda k: (k, 0, 0))],
        out_specs=blk(batch),
        out_shape=jax.ShapeDtypeStruct(cdata.shape, BF16),
        compiler_params=_cparams(("parallel",)),
        name="hyena_dft_inner",
    )(cdata, cfilt, g, gi)


def _hy_out_kernel(d_ref, x0_ref, x1_ref, v_ref, cw_ref, cb_ref, hb_ref, kai_ref, o_ref, ys, *, seq):
    nj = seq // DFT_BLK
    for p in range(DFT_BLK // DFT_SUB):
        sl = slice(p * DFT_SUB, (p + 1) * DFT_SUB)
        rhs = d_ref[0, :, :, sl, :].reshape(2 * DFT_NBLK * DFT_SUB, D_C)
        y = jnp.dot(kai_ref[...], rhs, preferred_element_type=F32)
        ys[:, sl, :] = y.reshape(nj, DFT_SUB, D_C)
    cw = cw_ref[...]
    cb = cb_ref[...]
    hb = hb_ref[...]
    for j in range(nj):
        x0 = _dwconv_block(x0_ref, j, nj, cw[:, :D_C], cb[:, :D_C], 1)
        x1 = _dwconv_block(x1_ref, j, nj, cw[:, D_C:2 * D_C], cb[:, D_C:2 * D_C], 1)
        v = _dwconv_block(v_ref, j, nj, cw[:, 2 * D_C:], cb[:, 2 * D_C:], 1)
        z = v * x1
        o_ref[j * DFT_BLK:(j + 1) * DFT_BLK, :] = (ys[j] + z * hb) * x0


def _hy_out(dd, pc, cw, cb, hb, kai, batch, seq):
    n_tok = pc.shape[0]
    nj = seq // DFT_BLK
    kern = functools.partial(_hy_out_kernel, seq=seq)
    whole = lambda a: pl.BlockSpec(a.shape, lambda b: (0, 0))
    return pl.pallas_call(
        kern,
        grid=(batch,),
        in_specs=[
            pl.BlockSpec((1, 2, DFT_NBLK, DFT_BLK, D_C), lambda b: (b, 0, 0, 0, 0)),
            pl.BlockSpec((seq, D_C), lambda b: (b, 0)),
            pl.BlockSpec((seq, D_C), lambda b: (b, 1)),
            pl.BlockSpec((seq, D_C), lambda b: (b, 2)),
            whole(cw), whole(cb), whole(hb), whole(kai),
        ],
        out_specs=pl.BlockSpec((seq, D_C), lambda b: (b, 0)),
        out_shape=jax.ShapeDtypeStruct((n_tok, D_C), F32),
        scratch_shapes=[pltpu.VMEM((nj, DFT_BLK, D_C), F32)],
        compiler_params=_cparams(("parallel",)),
        name="hyena_out",
    )(dd, pc, pc, pc, cw, cb, hb, kai)


def _mix_mlp_kernel(ya_ref, yb_ref, yc_ref, x_ref, ga_ref, gb_ref, gc_ref, wo_ref, g_ref, wu_ref,
                    wd_ref, gf_ref, o_ref, *, ff_chunk, final_norm):
    y = jnp.concatenate([_rms(ya_ref[...], ga_ref[...]), _rms(yb_ref[...], gb_ref[...]),
                         _rms(yc_ref[...], gc_ref[...])], axis=-1).astype(BF16)
    x = x_ref[...] + jnp.dot(y, wo_ref[...], preferred_element_type=F32)
    h = _rms(x, g_ref[...]).astype(BF16)
    acc = x
    for c in range(D_FF // ff_chunk):
        sl = slice(c * ff_chunk, (c + 1) * ff_chunk)
        u = jnp.maximum(jnp.dot(h, wu_ref[:, sl], preferred_element_type=F32), 0.0)
        acc = acc + jnp.dot((u * u).astype(BF16), wd_ref[sl, :], preferred_element_type=F32)
    if final_norm:
        acc = _rms(acc, gf_ref[...])
    o_ref[...] = acc


def _mix_mlp(ya, yb, yc, x2, ga, gb, gc, wo_bf, g, wu_bf, wd_bf, gf, tm, final_norm, ff_chunk=1024):
    n_tok = x2.shape[0]
    kern = functools.partial(_mix_mlp_kernel, ff_chunk=ff_chunk, final_norm=final_norm)
    row = lambda w: pl.BlockSpec((tm, w), lambda i: (i, 0))
    vec = lambda w: pl.BlockSpec((1, w), lambda i: (0, 0))
    resident = lambda r, c: pl.BlockSpec((r, c), lambda i: (0, 0), pipeline_mode=pl.Buffered(1))
    return pl.pallas_call(
        kern,
        grid=(n_tok // tm,),
        in_specs=[row(D_A), row(D_B), row(D_C), row(D_MODEL), vec(D_A), vec(D_B), vec(D_C),
                  resident(D_MODEL, D_MODEL), vec(D_MODEL), resident(D_MODEL, D_FF),
                  resident(D_FF, D_MODEL), vec(D_MODEL)],
        out_specs=row(D_MODEL),
        out_shape=jax.ShapeDtypeStruct((n_tok, D_MODEL), F32),
        compiler_params=_cparams(("parallel",)),
        name="mix_mlp",
    )(ya, yb, yc, x2, ga, gb, gc, wo_bf, g, wu_bf, wd_bf, gf)


def _rope_tables(seq):
    pos = np.arange(seq, dtype=np.float32)
    inv_freq = (np.float32(ROPE_THETA) ** (-np.arange(0, ROT_DIM, 2, dtype=np.float32) / ROT_DIM))
    ang = (pos[:, None] * inv_freq[None, :]).astype(np.float32)
    cos, sin = np.cos(ang).astype(np.float32), np.sin(ang).astype(np.float32)
    rest = HEAD_DIM - ROT_DIM
    c_head = np.concatenate([cos, cos, np.ones((seq, rest), np.float32)], axis=1)
    s_head = np.concatenate([-sin, sin, np.zeros((seq, rest), np.float32)], axis=1)
    scale = np.float32(HEAD_DIM ** -0.5)
    c = np.concatenate([np.tile(c_head, (1, N_Q_HEADS)) * scale, np.tile(c_head, (1, N_KV_HEADS))], axis=1)
    s = np.concatenate([np.tile(s_head, (1, N_Q_HEADS)) * scale, np.tile(s_head, (1, N_KV_HEADS))], axis=1)
    return jnp.asarray(c), jnp.asarray(s)


def _hyena_position_tables(seq):
    t = np.linspace(0.0, 1.0, seq, dtype=np.float32)[:, None]
    w = (2.0 * math.pi * np.arange(seq, dtype=np.float32)[:, None] / seq).astype(np.float32)
    f = np.linspace(1e-4, HY_BANDS - 1, HY_BANDS, dtype=np.float32)[None, :]
    fw = (f * w).astype(np.float32)
    z = np.concatenate([t, np.cos(fw), -np.sin(fw)], axis=-1).astype(np.float32)
    deltas = np.abs(np.linspace(HY_MIN_DECAY, HY_MAX_DECAY, D_C, dtype=np.float32))
    decay = np.exp(-t * deltas[None, :]).astype(np.float32)
    decay_b = decay.copy()
    decay_b[0] = 0.0
    half = seq // 2
    zp = np.zeros((half, 2, HY_WIDTH), np.float32)
    zp[:, 0, :HY_EMB] = z[:half]
    zp[:, 1, :HY_EMB] = z[half:]
    dec4 = np.concatenate([decay[:half], decay_b[:half], decay[half:], decay_b[half:]], axis=1)
    return jnp.asarray(zp.reshape(half, 2 * HY_WIDTH)), jnp.asarray(dec4)


def _blockdiag2(a):
    z = jnp.zeros_like(a)
    return jnp.concatenate([jnp.concatenate([a, z], axis=1), jnp.concatenate([z, a], axis=1)], axis=0)


def _dft_tables(seq):
    n = 2 * seq
    nj = seq // DFT_BLK
    kk = np.arange(DFT_NBLK)
    ang = 2.0 * np.pi * np.outer(kk, np.arange(nj)) / DFT_NBLK
    eye = np.eye(DFT_SUB)
    ka = np.concatenate([np.kron(np.cos(ang), eye), np.kron(-np.sin(ang), eye)], axis=0)
    kai = np.concatenate([np.kron(np.cos(ang).T, eye), np.kron(-np.sin(ang).T, eye)], axis=1) / n
    m = np.arange(DFT_BLK)
    k = kk[:, None, None] + DFT_NBLK * np.arange(DFT_BLK)[None, :, None]
    ph = 2.0 * np.pi * ((k * m[None, None, :]) % n) / n
    gre, gim = np.cos(ph), -np.sin(ph)
    g = np.concatenate([np.concatenate([gre, -gim], axis=2), np.concatenate([gim, gre], axis=2)], axis=1)
    gi = np.transpose(g, (0, 2, 1))
    as_bf = lambda a: jnp.asarray(a.astype(np.float32)).astype(BF16)
    return as_bf(ka), as_bf(kai), as_bf(g), as_bf(gi)


def _lru_blockdiag(w):
    nb = w.shape[1] // 2
    w = w.reshape(2, nb, 2, w.shape[2], w.shape[3])
    z = jnp.zeros_like(w[:, :, 0])
    top = jnp.concatenate([w[:, :, 0], z], axis=-1)
    bot = jnp.concatenate([z, w[:, :, 1]], axis=-1)
    return jnp.concatenate([top, bot], axis=-2)


def kernel(x, norm_mix_g, w_in, conv_a_w, conv_a_b, lru_wa, lru_ba, lru_wx, lru_bx, lru_lambda,
           attn_sink, hy_conv_w, hy_conv_b, hy_w1, hy_b1, hy_freq, hy_w2, hy_b2, hy_w3, hy_bias,
           gnorm_a, gnorm_b, gnorm_c, w_out, norm_mlp_g, w_up, w_down, final_norm_g):
    batch, seq, _ = x.shape
    depth = w_in.shape[0]
    n_tok = batch * seq
    tm = 1024
    tm_mlp = 512
    ng = D_A // LANES

    rc, rs = _rope_tables(seq)
    zemb, dec4 = _hyena_position_tables(seq)
    ka, kai, gtab, gitab = _dft_tables(seq)

    xs = x.reshape(n_tok, D_MODEL)
    for i in range(depth):
        pa, pq, vt, pc = _in_proj(xs, norm_mix_g[i][None], w_in[i].astype(BF16), rc, rs, seq, tm)

        wa, wx = _lru_blockdiag(lru_wa[i]), _lru_blockdiag(lru_wx[i])
        wg = jnp.concatenate([wa[0], wx[0], wa[1], wx[1]], axis=-1).astype(BF16)
        tile = lambda v: v.reshape(ng, 1, LANES)
        bias = jnp.concatenate([tile(lru_ba[i][0]), tile(lru_bx[i][0]),
                                tile(lru_ba[i][1]), tile(lru_bx[i][1])], axis=-1)
        lam = jnp.concatenate([tile(lru_lambda[i][0]), tile(lru_lambda[i][1])], axis=-1)
        y_a = _lru(pa, conv_a_w[i], conv_a_b[i][None], wg, bias, lam, batch, seq)

        y_b = _attention(pq, vt, attn_sink[i], batch, seq)

        w1p = jnp.zeros((HY_WIDTH, HY_WIDTH), F32).at[:HY_EMB].set(hy_w1[i])
        pair = lambda v: jnp.concatenate([v, v])[None]
        filt = _hy_filter(zemb, _blockdiag2(w1p), pair(hy_b1[i]), pair(hy_freq[i]),
                          _blockdiag2(hy_w2[i]), pair(hy_b2[i]), _blockdiag2(hy_w3[i]), dec4)
        cw, cb = hy_conv_w[i], hy_conv_b[i][None]
        cdata = _hy_fwd_data(pc, cw, cb, ka, batch, seq)
        cfilt = _hy_fwd_filt(filt.reshape(2, seq // DFT_BLK, DFT_BLK, D_C), ka)
        dd = _hy_inner(cdata, cfilt, gtab, gitab)
        y_c = _hy_out(dd, pc, cw, cb, hy_bias[i][None], kai, batch, seq)

        xs = _mix_mlp(y_a, y_b, y_c, xs, gnorm_a[i][None], gnorm_b[i][None], gnorm_c[i][None],
                      w_out[i].astype(BF16), norm_mlp_g[i][None], w_up[i].astype(BF16),
                      w_down[i].astype(BF16), final_norm_g[None], tm, final_norm=(i == depth - 1))
    return xs.reshape(batch, seq, D_MODEL)
```

```python
import functools
import math

import numpy as np
import jax
import jax.numpy as jnp
from jax import lax
from jax.experimental import pallas as pl
from jax.experimental.pallas import tpu as pltpu

F32 = jnp.float32
BF16 = jnp.bfloat16

D_MODEL = 1024
D_A = 384
D_B = 384
D_C = 256
HEAD_DIM = 64
N_Q_HEADS = 6
N_KV_HEADS = 2
GROUP = N_Q_HEADS // N_KV_HEADS
D_KV = N_KV_HEADS * HEAD_DIM
D_QKV = D_B + 2 * D_KV
D_IN = 2 * D_A + D_QKV + 3 * D_C
C_LRU = 8.0
WINDOW = 128
BLOCK = 128
ROPE_THETA = 500000.0
ROT_DIM = HEAD_DIM // 4
HY_EMB = 33
HY_BANDS = (HY_EMB - 1) // 2
HY_WIDTH = 64
HY_TARGET = 1e-2
HY_MAX_DECAY = math.log(HY_TARGET) / 0.3
HY_MIN_DECAY = math.log(HY_TARGET) / 1.5
D_FF = 4 * D_MODEL
EPS = 1e-6
NEG = -1e30

LANES = 128
SUBLANES = 8
VMEM_LIMIT = 56 * 1024 * 1024

SCAN_SEGS = SUBLANES
DFT_BLK = 128
DFT_NBLK = 64
DFT_SUB = 16
IN_PROJ_CHUNKS = 2


def _cparams(sem):
    return pltpu.CompilerParams(dimension_semantics=sem, vmem_limit_bytes=VMEM_LIMIT)


def _rms(x, g):
    return x * lax.rsqrt(jnp.mean(x * x, axis=-1, keepdims=True) + EPS) * g


def _in_proj_kernel(x_ref, g_ref, w_ref, rc_ref, rs_ref, oa_ref, oq_ref, ov_ref, oc_ref):
    n = D_B + D_KV
    half = ROT_DIM // 2
    tm = x_ref.shape[0]
    for c in range(IN_PROJ_CHUNKS):
        r = slice(c * tm // IN_PROJ_CHUNKS, (c + 1) * tm // IN_PROJ_CHUNKS)
        h = _rms(x_ref[r, :], g_ref[...]).astype(BF16)
        oa_ref[r, :] = jnp.dot(h, w_ref[:, :2 * D_A], preferred_element_type=F32).astype(BF16)
        oc_ref[r, :] = jnp.dot(h, w_ref[:, 2 * D_A + D_QKV:], preferred_element_type=F32).astype(BF16)
        qkv = jnp.dot(h, w_ref[:, 2 * D_A:2 * D_A + D_QKV], preferred_element_type=F32)
        qk = qkv[:, :n]
        lane = lax.broadcasted_iota(jnp.int32, qk.shape, 1) % HEAD_DIM
        swapped = jnp.where(lane < half, pltpu.roll(qk, n - half, axis=1), pltpu.roll(qk, half, axis=1))
        oq_ref[r, :] = (qk * rc_ref[r, :] + swapped * rs_ref[r, :]).astype(BF16)
        ov_ref[:, r] = qkv[:, n:].T.astype(BF16)


def _in_proj(x2, g, w_bf, rc, rs, seq, tm):
    n_tok = x2.shape[0]
    nrb = seq // tm
    return pl.pallas_call(
        _in_proj_kernel,
        grid=(n_tok // tm,),
        in_specs=[
            pl.BlockSpec((tm, D_MODEL), lambda i: (i, 0)),
            pl.BlockSpec((1, D_MODEL), lambda i: (0, 0)),
            pl.BlockSpec((D_MODEL, D_IN), lambda i: (0, 0)),
            pl.BlockSpec((tm, D_B + D_KV), lambda i: (i % nrb, 0)),
            pl.BlockSpec((tm, D_B + D_KV), lambda i: (i % nrb, 0)),
        ],
        out_specs=[
            pl.BlockSpec((tm, 2 * D_A), lambda i: (i, 0)),
            pl.BlockSpec((tm, D_B + D_KV), lambda i: (i, 0)),
            pl.BlockSpec((D_KV, tm), lambda i: (0, i)),
            pl.BlockSpec((tm, 3 * D_C), lambda i: (i, 0)),
        ],
        out_shape=[
            jax.ShapeDtypeStruct((n_tok, 2 * D_A), BF16),
            jax.ShapeDtypeStruct((n_tok, D_B + D_KV), BF16),
            jax.ShapeDtypeStruct((D_KV, n_tok), BF16),
            jax.ShapeDtypeStruct((n_tok, 3 * D_C), BF16),
        ],
        compiler_params=_cparams(("parallel",)),
        name="in_proj",
    )(x2, g, w_bf, rc, rs)


def _dwconv(pad_ref, r0, rows, w, b, pad_left):
    return _conv_taps(pad_ref[pl.ds(r0, rows + 2 * SUBLANES), :], rows, w, b, pad_left, SUBLANES)


def _conv_taps(win, rows, w, b, pad_left, halo):
    total = rows + 2 * halo
    acc = None
    for k in range(w.shape[0]):
        shift = (pad_left - k) % total
        tap = win if shift == 0 else pltpu.roll(win, shift, axis=0)
        term = tap[halo:halo + rows] * w[k:k + 1, :]
        acc = term if acc is None else acc + term
    return acc + b


def _dwconv_block(src_ref, j, nblk, w, b, pad_left):
    r0 = j * DFT_BLK
    halo = 2 * SUBLANES
    zeros = jnp.zeros((halo, src_ref.shape[1]), src_ref.dtype)
    top = zeros if j == 0 else src_ref[r0 - halo:r0, :]
    bot = zeros if j == nblk - 1 else src_ref[r0 + DFT_BLK:r0 + DFT_BLK + halo, :]
    win = jnp.concatenate([top, src_ref[r0:r0 + DFT_BLK, :], bot], axis=0).astype(F32)
    return _conv_taps(win, DFT_BLK, w, b, pad_left, halo)


def _fill_padded(pad_ref, src_ref, seq):
    zeros = jnp.zeros((SUBLANES, pad_ref.shape[1]), F32)
    pad_ref[0:SUBLANES, :] = zeros
    pad_ref[seq + SUBLANES:seq + 2 * SUBLANES, :] = zeros
    pad_ref[SUBLANES:seq + SUBLANES, :] = src_ref[...].astype(F32)


def _softplus(x):
    return jnp.maximum(x, 0.0) + jnp.log1p(jnp.exp(-jnp.abs(x)))


def _gelu_tanh(x):
    c = math.sqrt(2.0 / math.pi)
    return x * (0.5 * (1.0 + jnp.tanh(c * (x + 0.044715 * (x * x * x)))))


def _lru_kernel(u_ref, gate_ref, cw_ref, cb_ref, w_ref, bias_ref, lam_ref, o_ref,
                upad, xci, gg, af, bf, ar, br, pfs, hfs, prs, hrs, g0, g1, *, seq, rows, unroll):
    seg = seq // SCAN_SEGS
    tiles = rows // SCAN_SEGS
    nchunk = seq // rows
    per_seg = seg // rows
    _fill_padded(upad, u_ref, seq)
    cw = cw_ref[...]
    cb = cb_ref[...]
    w = w_ref[0] * 0.5
    bias = bias_ref[0] * 0.5
    hnsp = (-0.5 * C_LRU) * _softplus(-lam_ref[0])

    def conv(ci, carry):
        r0 = pl.multiple_of(ci * rows, rows)
        dst = pl.ds((ci % per_seg) * (rows * SCAN_SEGS) + ci // per_seg, rows, stride=SCAN_SEGS)
        xci[dst, :] = _dwconv(upad, r0, rows, cw, cb, 2)
        gg[dst, :] = _gelu_tanh(gate_ref[pl.ds(r0, rows), :].astype(F32))
        return carry

    lax.fori_loop(0, nchunk, conv, 0)

    grows = g0.shape[0]
    ngate = seq // grows

    def gate_matmul(ci, g_ref):
        r0 = pl.multiple_of(ci * grows, grows)
        g_ref[...] = jnp.dot(xci[pl.ds(r0, grows), :].astype(BF16), w, preferred_element_type=F32)

    def gate_math(ci, g_ref):
        r0 = pl.multiple_of(ci * grows, grows)
        xc = xci[pl.ds(r0, grows), :]
        t = jnp.tanh(g_ref[...] + bias)
        hxc = 0.5 * xc
        for d, (a_ref, b_ref) in enumerate(((af, bf), (ar, br))):
            tr = t[:, (2 * d) * LANES:(2 * d + 1) * LANES]
            ti = t[:, (2 * d + 1) * LANES:(2 * d + 2) * LANES]
            hn = hnsp[:, d * LANES:(d + 1) * LANES]
            log_a = tr * hn + hn
            a = jnp.exp(log_a)
            nem = (-1.0 - a * a) * jnp.tanh(log_a)
            a_ref[pl.ds(r0, grows), :] = a
            b_ref[pl.ds(r0, grows), :] = jnp.sqrt(nem) * (ti * hxc + hxc)

    def gates(k, carry):
        gate_matmul(2 * k + 1, g1)
        gate_math(2 * k, g0)
        gate_matmul(jnp.minimum(2 * k + 2, ngate - 1), g0)
        gate_math(2 * k + 1, g1)
        return carry

    gate_matmul(0, g0)
    lax.fori_loop(0, ngate // 2, gates, 0)

    def scan(it, carry):
        def two_steps(p, h, a_ref, b_ref, p_out, h_out, r0, r1):
            a0 = a_ref[pl.ds(r0, SCAN_SEGS), :]
            b0 = b_ref[pl.ds(r0, SCAN_SEGS), :]
            a1 = a_ref[pl.ds(r1, SCAN_SEGS), :]
            a01 = a1 * a0
            b01 = a1 * b0 + b_ref[pl.ds(r1, SCAN_SEGS), :]
            p_out[pl.ds(r0, SCAN_SEGS), :] = a0 * p
            h_out[pl.ds(r0, SCAN_SEGS), :] = a0 * h + b0
            p = a01 * p
            h = a01 * h + b01
            p_out[pl.ds(r1, SCAN_SEGS), :] = p
            h_out[pl.ds(r1, SCAN_SEGS), :] = h
            return p, h

        pf, hf, pr, hr = carry
        for u in range(0, unroll, 2):
            i = it * unroll + u
            row = lambda pos: pl.multiple_of(pos * SCAN_SEGS, SCAN_SEGS)
            pf, hf = two_steps(pf, hf, af, bf, pfs, hfs, row(i), row(i + 1))
            pr, hr = two_steps(pr, hr, ar, br, prs, hrs, row(seg - 1 - i), row(seg - 2 - i))
        return pf, hf, pr, hr

    one = jnp.ones((SCAN_SEGS, LANES), F32)
    zero = jnp.zeros((SCAN_SEGS, LANES), F32)
    pf, hf, pr, hr = lax.fori_loop(0, seg // unroll, scan, (one, zero, one, zero))

    sub = lax.broadcasted_iota(jnp.int32, (SCAN_SEGS, LANES), 0)
    cf = zero
    cr = zero
    for _ in range(SCAN_SEGS - 1):
        cf = jnp.where(sub == 0, 0.0, pltpu.roll(hf + pf * cf, 1, axis=0))
        cr = jnp.where(sub == SCAN_SEGS - 1, 0.0, pltpu.roll(hr + pr * cr, SCAN_SEGS - 1, axis=0))

    def combine(ci, carry):
        r0 = pl.multiple_of(ci * rows, rows)
        sl = pl.ds(r0, rows)
        tile3 = lambda ref: ref[sl, :].reshape(tiles, SCAN_SEGS, LANES)
        h = (tile3(hfs) + tile3(pfs) * cf[None]) + (tile3(hrs) + tile3(prs) * cr[None])
        out = h.reshape(rows, LANES) * gg[sl, :]
        for k in range(tiles):
            o_ref[pl.ds(ci * tiles + k, SCAN_SEGS, stride=seg), :] = (
                out[k * SCAN_SEGS:(k + 1) * SCAN_SEGS])
        return carry

    lax.fori_loop(0, nchunk, combine, 0)


def _lru(pa, cw, cb, wg, bias, lam, batch, seq, rows=128, gate_rows=512, unroll=16):
    n_tok = pa.shape[0]
    ng = D_A // LANES
    kern = functools.partial(_lru_kernel, seq=seq, rows=rows, unroll=unroll)
    return pl.pallas_call(
        kern,
        grid=(batch, ng),
        in_specs=[
            pl.BlockSpec((seq, LANES), lambda b, g: (b, g)),
            pl.BlockSpec((seq, LANES), lambda b, g: (b, ng + g)),
            pl.BlockSpec((cw.shape[0], LANES), lambda b, g: (0, g)),
            pl.BlockSpec((1, LANES), lambda b, g: (0, g)),
            pl.BlockSpec((1, LANES, 4 * LANES), lambda b, g: (g, 0, 0)),
            pl.BlockSpec((1, 1, 4 * LANES), lambda b, g: (g, 0, 0)),
            pl.BlockSpec((1, 1, 2 * LANES), lambda b, g: (g, 0, 0)),
        ],
        out_specs=pl.BlockSpec((seq, LANES), lambda b, g: (b, g)),
        out_shape=jax.ShapeDtypeStruct((n_tok, D_A), F32),
        scratch_shapes=[pltpu.VMEM((seq + 2 * SUBLANES, LANES), F32)]
        + [pltpu.VMEM((seq, LANES), F32) for _ in range(10)]
        + [pltpu.VMEM((gate_rows, 4 * LANES), F32) for _ in range(2)],
        compiler_params=_cparams(("parallel", "parallel")),
        name="rglru",
    )(pa, pa, cw, cb, wg, bias, lam)


_ATTN_STRAIGHT = (0, 2, 3, 5)
_ATTN_ROLLED = (1, 4)
_ONES_ROWS = 16


def _attn_kernel(sink_ref, q_ref, k_ref, vt_ref, o_ref, *, seq, tq):
    band = 3 * BLOCK
    nsub = tq // BLOCK
    i = pl.program_id(1)
    ki = lax.broadcasted_iota(jnp.int32, (band, BLOCK), 0)
    qi = lax.broadcasted_iota(jnp.int32, (band, BLOCK), 1)
    lo_q = lax.broadcasted_iota(jnp.int32, (BLOCK, 2 * HEAD_DIM), 1) < HEAD_DIM
    ones = jnp.ones((_ONES_ROWS, band), BF16)
    nt = (((1,), (1,)), ((), ()))

    for jb in range(nsub):
        rows = slice(jb * BLOCK, (jb + 1) * BLOCK)
        q0 = (i * nsub + jb) * BLOCK
        k0 = pl.multiple_of(jnp.clip(q0 - BLOCK, 0, seq - band), BLOCK)
        bias = jnp.where(jnp.abs((q0 + qi) - (k0 + ki)) <= WINDOW, 0.0, NEG)
        kb = k_ref[pl.ds(k0, band), :]
        kbs = pltpu.roll(kb, HEAD_DIM, axis=1)
        vt = vt_ref[:, pl.ds(k0, band)]

        def own_half(h):
            qt = q_ref[rows, (h // 2) * 2 * HEAD_DIM:(h // 2 + 1) * 2 * HEAD_DIM]
            return jnp.where(lo_q, qt, 0.0) if h % 2 == 0 else jnp.where(lo_q, 0.0, qt)

        qa = jnp.concatenate([own_half(h) for h in _ATTN_STRAIGHT], axis=0)
        qb = jnp.concatenate([own_half(h) for h in _ATTN_ROLLED], axis=0)
        sa = lax.dot_general(kb, qa, nt, preferred_element_type=F32)
        sb = lax.dot_general(kbs, qb, nt, preferred_element_type=F32)
        st = {h: sa[:, n * BLOCK:(n + 1) * BLOCK] for n, h in enumerate(_ATTN_STRAIGHT)}
        st.update({h: sb[:, n * BLOCK:(n + 1) * BLOCK] for n, h in enumerate(_ATTN_ROLLED)})
        outs = []
        for kv in range(N_KV_HEADS):
            ps, esk = [], []
            for h in range(kv * GROUP, (kv + 1) * GROUP):
                s = st[h] + bias
                sk = sink_ref[h]
                m = jnp.maximum(jnp.max(s, axis=0, keepdims=True), sk)
                ps.append(jnp.exp(s - m).astype(BF16))
                esk.append(jnp.exp(sk - m))
            lhs = jnp.concatenate([vt[kv * HEAD_DIM:(kv + 1) * HEAD_DIM], ones], axis=0)
            ov = jnp.dot(lhs, jnp.concatenate(ps, axis=1), preferred_element_type=F32)
            res = ov[:HEAD_DIM] / (ov[HEAD_DIM:HEAD_DIM + 1] + jnp.concatenate(esk, axis=1))
            outs += [res[:, g * BLOCK:(g + 1) * BLOCK] for g in range(GROUP)]
        o_ref[rows, :] = jnp.concatenate(outs, axis=0).T


def _attention(pq, vt, sink, batch, seq, tq=512):
    n_tok = pq.shape[0]
    nq = seq // tq
    kern = functools.partial(_attn_kernel, seq=seq, tq=tq)
    return pl.pallas_call(
        kern,
        grid=(batch, nq),
        in_specs=[
            pl.BlockSpec(memory_space=pltpu.SMEM),
            pl.BlockSpec((tq, D_B), lambda b, i: (b * nq + i, 0)),
            pl.BlockSpec((seq, D_KV), lambda b, i: (b, D_B // D_KV)),
            pl.BlockSpec((D_KV, seq), lambda b, i: (0, b)),
        ],
        out_specs=pl.BlockSpec((tq, D_B), lambda b, i: (b * nq + i, 0)),
        out_shape=jax.ShapeDtypeStruct((n_tok, D_B), F32),
        compiler_params=_cparams(("parallel", "arbitrary")),
        name="win_attn",
    )(sink, pq, pq, vt)


def _hy_filter_kernel(z_ref, w1_ref, b1_ref, fr_ref, w2_ref, b2_ref, w3_ref, dec_ref, o_ref):
    hi = lax.Precision.HIGHEST
    fr = fr_ref[...]
    h = jnp.sin(fr * (jnp.dot(z_ref[...], w1_ref[...], preferred_element_type=F32, precision=hi)
                      + b1_ref[...]))
    h = jnp.sin(fr * (jnp.dot(h, w2_ref[...], preferred_element_type=F32, precision=hi) + b2_ref[...]))
    f = jnp.dot(h, w3_ref[...], preferred_element_type=F32, precision=hi) * dec_ref[...]
    for half in range(2):
        for d in range(2):
            c0 = (2 * half + d) * D_C
            o_ref[d, half] = f[:, c0:c0 + D_C]


def _hy_filter(zemb2, w1, b1, fr, w2, b2, w3, dec4, rows=512):
    n = zemb2.shape[0]
    full = lambda a: pl.BlockSpec(a.shape, lambda i: (0,) * a.ndim)
    return pl.pallas_call(
        _hy_filter_kernel,
        grid=(n // rows,),
        in_specs=[pl.BlockSpec((rows, zemb2.shape[1]), lambda i: (i, 0)),
                  full(w1), full(b1), full(fr), full(w2), full(b2), full(w3),
                  pl.BlockSpec((rows, 4 * D_C), lambda i: (i, 0))],
        out_specs=pl.BlockSpec((2, 2, rows, D_C), lambda i: (0, 0, i, 0)),
        out_shape=jax.ShapeDtypeStruct((2, 2, n, D_C), F32),
        compiler_params=_cparams(("parallel",)),
        name="hyena_filter",
    )(zemb2, w1, b1, fr, w2, b2, w3, dec4)


def _dft_outer_fwd(zs, ka_ref, o_ref):
    nj, _, lanes = zs.shape
    for p in range(DFT_BLK // DFT_SUB):
        sl = slice(p * DFT_SUB, (p + 1) * DFT_SUB)
        xg = zs[:, sl, :].reshape(nj * DFT_SUB, lanes).astype(BF16)
        c = jnp.dot(ka_ref[...], xg, preferred_element_type=F32)
        o_ref[0, :, :, sl, :] = c.reshape(2, DFT_NBLK, DFT_SUB, lanes).astype(BF16)


def _hy_fwd_data_kernel(x1_ref, v_ref, cw_ref, cb_ref, ka_ref, o_ref, zs, *, seq):
    nj = seq // DFT_BLK
    cw = cw_ref[...]
    cb = cb_ref[...]
    for j in range(nj):
        x1 = _dwconv_block(x1_ref, j, nj, cw[:, D_C:2 * D_C], cb[:, D_C:2 * D_C], 1)
        v = _dwconv_block(v_ref, j, nj, cw[:, 2 * D_C:], cb[:, 2 * D_C:], 1)
        zs[j] = v * x1
    _dft_outer_fwd(zs, ka_ref, o_ref)


def _hy_fwd_filt_kernel(f_ref, ka_ref, o_ref):
    _dft_outer_fwd(f_ref.at[0], ka_ref, o_ref)


def _hy_fwd_data(pc, cw, cb, ka, batch, seq):
    nj = seq // DFT_BLK
    kern = functools.partial(_hy_fwd_data_kernel, seq=seq)
    return pl.pallas_call(
        kern,
        grid=(batch,),
        in_specs=[
            pl.BlockSpec((seq, D_C), lambda b: (b, 1)),
            pl.BlockSpec((seq, D_C), lambda b: (b, 2)),
            pl.BlockSpec(cw.shape, lambda b: (0, 0)),
            pl.BlockSpec(cb.shape, lambda b: (0, 0)),
            pl.BlockSpec(ka.shape, lambda b: (0, 0)),
        ],
        out_specs=pl.BlockSpec((1, 2, DFT_NBLK, DFT_BLK, D_C), lambda b: (b, 0, 0, 0, 0)),
        out_shape=jax.ShapeDtypeStruct((batch, 2, DFT_NBLK, DFT_BLK, D_C), BF16),
        scratch_shapes=[pltpu.VMEM((nj, DFT_BLK, D_C), F32)],
        compiler_params=_cparams(("parallel",)),
        name="hyena_dft_outer",
    )(pc, pc, cw, cb, ka)


def _hy_fwd_filt(filt4, ka):
    ndir, nj = filt4.shape[0], filt4.shape[1]
    return pl.pallas_call(
        _hy_fwd_filt_kernel,
        grid=(ndir,),
        in_specs=[
            pl.BlockSpec((1, nj, DFT_BLK, D_C), lambda b: (b, 0, 0, 0)),
            pl.BlockSpec(ka.shape, lambda b: (0, 0)),
        ],
        out_specs=pl.BlockSpec((1, 2, DFT_NBLK, DFT_BLK, D_C), lambda b: (b, 0, 0, 0, 0)),
        out_shape=jax.ShapeDtypeStruct((ndir, 2, DFT_NBLK, DFT_BLK, D_C), BF16),
        compiler_params=_cparams(("parallel",)),
        name="hyena_dft_outer_filter",
    )(filt4, ka)


def _hy_inner_kernel(c_ref, f_ref, g_ref, gi_ref, o_ref, *, batch, kper):
    for q in range(kper):
        g = g_ref[q]
        gi = gi_ref[q]
        hf = jnp.dot(g, f_ref[0, :, q].reshape(2 * DFT_BLK, D_C), preferred_element_type=F32)
        hb = jnp.dot(g, f_ref[1, :, q].reshape(2 * DFT_BLK, D_C), preferred_element_type=F32)
        hre = hf[:DFT_BLK] + hb[:DFT_BLK]
        him = hf[DFT_BLK:] - hb[DFT_BLK:]
        for b in range(batch):
            x = jnp.dot(g, c_ref[b, :, q].reshape(2 * DFT_BLK, D_C), preferred_element_type=F32)
            xre, xim = x[:DFT_BLK], x[DFT_BLK:]
            y = jnp.concatenate([xre * hre - xim * him, xre * him + xim * hre], axis=0)
            d = jnp.dot(gi, y.astype(BF16), preferred_element_type=F32)
            o_ref[b, :, q] = d.reshape(2, DFT_BLK, D_C).astype(BF16)


def _hy_inner(cdata, cfilt, g, gi, kper=4):
    batch = cdata.shape[0]
    kern = functools.partial(_hy_inner_kernel, batch=batch, kper=kper)
    blk = lambda nb: pl.BlockSpec((nb, 2, kper, DFT_BLK, D_C), lambda k: (0, 0, k, 0, 0))
    return pl.pallas_call(
        kern,
        grid=(DFT_NBLK // kper,),
        in_specs=[blk(batch), blk(cfilt.shape[0]),
                  pl.BlockSpec((kper, 2 * DFT_BLK, 2 * DFT_BLK), lambda k: (k, 0, 0)),
                  pl.BlockSpec((kper, 2 * DFT_BLK, 2 * DFT_BLK), lambda k: (k, 0, 0))],
        out_specs=blk(batch),
        out_shape=jax.ShapeDtypeStruct(cdata.shape, BF16),
        compiler_params=_cparams(("parallel",)),
        name="hyena_dft_inner",
    )(cdata, cfilt, g, gi)


def _hy_out_kernel(d_ref, x0_ref, x1_ref, v_ref, cw_ref, cb_ref, hb_ref, kai_ref, o_ref, ys, *, seq):
    nj = seq // DFT_BLK
    for p in range(DFT_BLK // DFT_SUB):
        sl = slice(p * DFT_SUB, (p + 1) * DFT_SUB)
        rhs = d_ref[0, :, :, sl, :].reshape(2 * DFT_NBLK * DFT_SUB, D_C)
        y = jnp.dot(kai_ref[...], rhs, preferred_element_type=F32)
        ys[:, sl, :] = y.reshape(nj, DFT_SUB, D_C)
    cw = cw_ref[...]
    cb = cb_ref[...]
    hb = hb_ref[...]
    for j in range(nj):
        x0 = _dwconv_block(x0_ref, j, nj, cw[:, :D_C], cb[:, :D_C], 1)
        x1 = _dwconv_block(x1_ref, j, nj, cw[:, D_C:2 * D_C], cb[:, D_C:2 * D_C], 1)
        v = _dwconv_block(v_ref, j, nj, cw[:, 2 * D_C:], cb[:, 2 * D_C:], 1)
        z = v * x1
        o_ref[j * DFT_BLK:(j + 1) * DFT_BLK, :] = (ys[j] + z * hb) * x0


def _hy_out(dd, pc, cw, cb, hb, kai, batch, seq):
    n_tok = pc.shape[0]
    nj = seq // DFT_BLK
    kern = functools.partial(_hy_out_kernel, seq=seq)
    whole = lambda a: pl.BlockSpec(a.shape, lambda b: (0, 0))
    return pl.pallas_call(
        kern,
        grid=(batch,),
        in_specs=[
            pl.BlockSpec((1, 2, DFT_NBLK, DFT_BLK, D_C), lambda b: (b, 0, 0, 0, 0)),
            pl.BlockSpec((seq, D_C), lambda b: (b, 0)),
            pl.BlockSpec((seq, D_C), lambda b: (b, 1)),
            pl.BlockSpec((seq, D_C), lambda b: (b, 2)),
            whole(cw), whole(cb), whole(hb), whole(kai),
        ],
        out_specs=pl.BlockSpec((seq, D_C), lambda b: (b, 0)),
        out_shape=jax.ShapeDtypeStruct((n_tok, D_C), F32),
        scratch_shapes=[pltpu.VMEM((nj, DFT_BLK, D_C), F32)],
        compiler_params=_cparams(("parallel",)),
        name="hyena_out",
    )(dd, pc, pc, pc, cw, cb, hb, kai)


def _mix_mlp_kernel(ya_ref, yb_ref, yc_ref, x_ref, ga_ref, gb_ref, gc_ref, wo_ref, g_ref, wu_ref,
                    wd_ref, gf_ref, o_ref, *, ff_chunk, final_norm):
    y = jnp.concatenate([_rms(ya_ref[...], ga_ref[...]), _rms(yb_ref[...], gb_ref[...]),
                         _rms(yc_ref[...], gc_ref[...])], axis=-1).astype(BF16)
    x = x_ref[...] + jnp.dot(y, wo_ref[...], preferred_element_type=F32)
    h = _rms(x, g_ref[...]).astype(BF16)
    acc = x
    for c in range(D_FF // ff_chunk):
        sl = slice(c * ff_chunk, (c + 1) * ff_chunk)
        u = jnp.maximum(jnp.dot(h, wu_ref[:, sl], preferred_element_type=F32), 0.0)
        acc = acc + jnp.dot((u * u).astype(BF16), wd_ref[sl, :], preferred_element_type=F32)
    if final_norm:
        acc = _rms(acc, gf_ref[...])
    o_ref[...] = acc


def _mix_mlp(ya, yb, yc, x2, ga, gb, gc, wo_bf, g, wu_bf, wd_bf, gf, tm, final_norm, ff_chunk=1024):
    n_tok = x2.shape[0]
    kern = functools.partial(_mix_mlp_kernel, ff_chunk=ff_chunk, final_norm=final_norm)
    row = lambda w: pl.BlockSpec((tm, w), lambda i: (i, 0))
    vec = lambda w: pl.BlockSpec((1, w), lambda i: (0, 0))
    resident = lambda r, c: pl.BlockSpec((r, c), lambda i: (0, 0), pipeline_mode=pl.Buffered(1))
    return pl.pallas_call(
        kern,
        grid=(n_tok // tm,),
        in_specs=[row(D_A), row(D_B), row(D_C), row(D_MODEL), vec(D_A), vec(D_B), vec(D_C),
                  resident(D_MODEL, D_MODEL), vec(D_MODEL), resident(D_MODEL, D_FF),
                  resident(D_FF, D_MODEL), vec(D_MODEL)],
        out_specs=row(D_MODEL),
        out_shape=jax.ShapeDtypeStruct((n_tok, D_MODEL), F32),
        compiler_params=_cparams(("parallel",)),
        name="mix_mlp",
    )(ya, yb, yc, x2, ga, gb, gc, wo_bf, g, wu_bf, wd_bf, gf)


def _rope_tables(seq):
    pos = np.arange(seq, dtype=np.float32)
    inv_freq = (np.float32(ROPE_THETA) ** (-np.arange(0, ROT_DIM, 2, dtype=np.float32) / ROT_DIM))
    ang = (pos[:, None] * inv_freq[None, :]).astype(np.float32)
    cos, sin = np.cos(ang).astype(np.float32), np.sin(ang).astype(np.float32)
    rest = HEAD_DIM - ROT_DIM
    c_head = np.concatenate([cos, cos, np.ones((seq, rest), np.float32)], axis=1)
    s_head = np.concatenate([-sin, sin, np.zeros((seq, rest), np.float32)], axis=1)
    scale = np.float32(HEAD_DIM ** -0.5)
    c = np.concatenate([np.tile(c_head, (1, N_Q_HEADS)) * scale, np.tile(c_head, (1, N_KV_HEADS))], axis=1)
    s = np.concatenate([np.tile(s_head, (1, N_Q_HEADS)) * scale, np.tile(s_head, (1, N_KV_HEADS))], axis=1)
    return jnp.asarray(c), jnp.asarray(s)


def _hyena_position_tables(seq):
    t = np.linspace(0.0, 1.0, seq, dtype=np.float32)[:, None]
    w = (2.0 * math.pi * np.arange(seq, dtype=np.float32)[:, None] / seq).astype(np.float32)
    f = np.linspace(1e-4, HY_BANDS - 1, HY_BANDS, dtype=np.float32)[None, :]
    fw = (f * w).astype(np.float32)
    z = np.concatenate([t, np.cos(fw), -np.sin(fw)], axis=-1).astype(np.float32)
    deltas = np.abs(np.linspace(HY_MIN_DECAY, HY_MAX_DECAY, D_C, dtype=np.float32))
    decay = np.exp(-t * deltas[None, :]).astype(np.float32)
    decay_b = decay.copy()
    decay_b[0] = 0.0
    half = seq // 2
    zp = np.zeros((half, 2, HY_WIDTH), np.float32)
    zp[:, 0, :HY_EMB] = z[:half]
    zp[:, 1, :HY_EMB] = z[half:]
    dec4 = np.concatenate([decay[:half], decay_b[:half], decay[half:], decay_b[half:]], axis=1)
    return jnp.asarray(zp.reshape(half, 2 * HY_WIDTH)), jnp.asarray(dec4)


def _blockdiag2(a):
    z = jnp.zeros_like(a)
    return jnp.concatenate([jnp.concatenate([a, z], axis=1), jnp.concatenate([z, a], axis=1)], axis=0)


def _dft_tables(seq):
    n = 2 * seq
    nj = seq // DFT_BLK
    kk = np.arange(DFT_NBLK)
    ang = 2.0 * np.pi * np.outer(kk, np.arange(nj)) / DFT_NBLK
    eye = np.eye(DFT_SUB)
    ka = np.concatenate([np.kron(np.cos(ang), eye), np.kron(-np.sin(ang), eye)], axis=0)
    kai = np.concatenate([np.kron(np.cos(ang).T, eye), np.kron(-np.sin(ang).T, eye)], axis=1) / n
    m = np.arange(DFT_BLK)
    k = kk[:, None, None] + DFT_NBLK * np.arange(DFT_BLK)[None, :, None]
    ph = 2.0 * np.pi * ((k * m[None, None, :]) % n) / n
    gre, gim = np.cos(ph), -np.sin(ph)
    g = np.concatenate([np.concatenate([gre, -gim], axis=2), np.concatenate([gim, gre], axis=2)], axis=1)
    gi = np.transpose(g, (0, 2, 1))
    as_bf = lambda a: jnp.asarray(a.astype(np.float32)).astype(BF16)
    return as_bf(ka), as_bf(kai), as_bf(g), as_bf(gi)


def _lru_blockdiag(w):
    nb = w.shape[1] // 2
    w = w.reshape(2, nb, 2, w.shape[2], w.shape[3])
    z = jnp.zeros_like(w[:, :, 0])
    top = jnp.concatenate([w[:, :, 0], z], axis=-1)
    bot = jnp.concatenate([z, w[:, :, 1]], axis=-1)
    return jnp.concatenate([top, bot], axis=-2)


def kernel(x, norm_mix_g, w_in, conv_a_w, conv_a_b, lru_wa, lru_ba, lru_wx, lru_bx, lru_lambda,
           attn_sink, hy_conv_w, hy_conv_b, hy_w1, hy_b1, hy_freq, hy_w2, hy_b2, hy_w3, hy_bias,
           gnorm_a, gnorm_b, gnorm_c, w_out, norm_mlp_g, w_up, w_down, final_norm_g):
    batch, seq, _ = x.shape
    depth = w_in.shape[0]
    n_tok = batch * seq
    tm = 1024
    tm_mlp = 512
    ng = D_A // LANES

    rc, rs = _rope_tables(seq)
    zemb, dec4 = _hyena_position_tables(seq)
    ka, kai, gtab, gitab = _dft_tables(seq)

    xs = x.reshape(n_tok, D_MODEL)
    for i in range(depth):
        pa, pq, vt, pc = _in_proj(xs, norm_mix_g[i][None], w_in[i].astype(BF16), rc, rs, seq, tm)

        wa, wx = _lru_blockdiag(lru_wa[i]), _lru_blockdiag(lru_wx[i])
        wg = jnp.concatenate([wa[0], wx[0], wa[1], wx[1]], axis=-1).astype(BF16)
        tile = lambda v: v.reshape(ng, 1, LANES)
        bias = jnp.concatenate([tile(lru_ba[i][0]), tile(lru_bx[i][0]),
                                tile(lru_ba[i][1]), tile(lru_bx[i][1])], axis=-1)
        lam = jnp.concatenate([tile(lru_lambda[i][0]), tile(lru_lambda[i][1])], axis=-1)
        y_a = _lru(pa, conv_a_w[i], conv_a_b[i][None], wg, bias, lam, batch, seq)

        y_b = _attention(pq, vt, attn_sink[i], batch, seq)

        w1p = jnp.zeros((HY_WIDTH, HY_WIDTH), F32).at[:HY_EMB].set(hy_w1[i])
        pair = lambda v: jnp.concatenate([v, v])[None]
        filt = _hy_filter(zemb, _blockdiag2(w1p), pair(hy_b1[i]), pair(hy_freq[i]),
                          _blockdiag2(hy_w2[i]), pair(hy_b2[i]), _blockdiag2(hy_w3[i]), dec4)
        cw, cb = hy_conv_w[i], hy_conv_b[i][None]
        cdata = _hy_fwd_data(pc, cw, cb, ka, batch, seq)
        cfilt = _hy_fwd_filt(filt.reshape(2, seq // DFT_BLK, DFT_BLK, D_C), ka)
        dd = _hy_inner(cdata, cfilt, gtab, gitab)
        y_c = _hy_out(dd, pc, cw, cb, hy_bias[i][None], kai, batch, seq)

        xs = _mix_mlp(y_a, y_b, y_c, xs, gnorm_a[i][None], gnorm_b[i][None], gnorm_c[i][None],
                      w_out[i].astype(BF16), norm_mlp_g[i][None], w_up[i].astype(BF16),
                      w_down[i].astype(BF16), final_norm_g[None], tm, final_norm=(i == depth - 1))
    return xs.reshape(batch, seq, D_MODEL)
```

```python
import functools
import math

import numpy as np
import jax
import jax.numpy as jnp
from jax import lax
from jax.experimental import pallas as pl
from jax.experimental.pallas import tpu as pltpu

F32 = jnp.float32
BF16 = jnp.bfloat16

D_MODEL = 1024
D_A = 384
D_B = 384
D_C = 256
HEAD_DIM = 64
N_Q_HEADS = 6
N_KV_HEADS = 2
GROUP = N_Q_HEADS // N_KV_HEADS
D_KV = N_KV_HEADS * HEAD_DIM
D_QKV = D_B + 2 * D_KV
D_IN = 2 * D_A + D_QKV + 3 * D_C
C_LRU = 8.0
WINDOW = 128
BLOCK = 128
ROPE_THETA = 500000.0
ROT_DIM = HEAD_DIM // 4
HY_EMB = 33
HY_BANDS = (HY_EMB - 1) // 2
HY_WIDTH = 64
HY_TARGET = 1e-2
HY_MAX_DECAY = math.log(HY_TARGET) / 0.3
HY_MIN_DECAY = math.log(HY_TARGET) / 1.5
D_FF = 4 * D_MODEL
EPS = 1e-6
NEG = -1e30
LOG2E = math.log2(math.e)

LANES = 128
SUBLANES = 8
VMEM_LIMIT = 56 * 1024 * 1024

SCAN_SEGS = SUBLANES
DFT_BLK = 128
DFT_NBLK = 64
DFT_SUB = 16
IN_PROJ_CHUNKS = 2


def _cparams(sem):
    return pltpu.CompilerParams(dimension_semantics=sem, vmem_limit_bytes=VMEM_LIMIT)


def _rms(x, g):
    return x * lax.rsqrt(jnp.mean(x * x, axis=-1, keepdims=True) + EPS) * g


def _in_proj_kernel(x_ref, g_ref, w_ref, rc_ref, rs_ref, oa_ref, oq_ref, ov_ref, oc_ref):
    n = D_B + D_KV
    half = ROT_DIM // 2
    tm = x_ref.shape[0]
    for c in range(IN_PROJ_CHUNKS):
        r = slice(c * tm // IN_PROJ_CHUNKS, (c + 1) * tm // IN_PROJ_CHUNKS)
        h = _rms(x_ref[r, :], g_ref[...]).astype(BF16)
        oa_ref[r, :] = jnp.dot(h, w_ref[:, :2 * D_A], preferred_element_type=F32).astype(BF16)
        oc_ref[r, :] = jnp.dot(h, w_ref[:, 2 * D_A + D_QKV:], preferred_element_type=F32).astype(BF16)
        qkv = jnp.dot(h, w_ref[:, 2 * D_A:2 * D_A + D_QKV], preferred_element_type=F32)
        qk = qkv[:, :n]
        lane = lax.broadcasted_iota(jnp.int32, qk.shape, 1) % HEAD_DIM
        swapped = jnp.where(lane < half, pltpu.roll(qk, n - half, axis=1), pltpu.roll(qk, half, axis=1))
        oq_ref[r, :] = (qk * rc_ref[r, :] + swapped * rs_ref[r, :]).astype(BF16)
        ov_ref[:, r] = qkv[:, n:].T.astype(BF16)


def _in_proj(x2, g, w_bf, rc, rs, seq, tm):
    n_tok = x2.shape[0]
    nrb = seq // tm
    return pl.pallas_call(
        _in_proj_kernel,
        grid=(n_tok // tm,),
        in_specs=[
            pl.BlockSpec((tm, D_MODEL), lambda i: (i, 0)),
            pl.BlockSpec((1, D_MODEL), lambda i: (0, 0)),
            pl.BlockSpec((D_MODEL, D_IN), lambda i: (0, 0)),
            pl.BlockSpec((tm, D_B + D_KV), lambda i: (i % nrb, 0)),
            pl.BlockSpec((tm, D_B + D_KV), lambda i: (i % nrb, 0)),
        ],
        out_specs=[
            pl.BlockSpec((tm, 2 * D_A), lambda i: (i, 0)),
            pl.BlockSpec((tm, D_B + D_KV), lambda i: (i, 0)),
            pl.BlockSpec((D_KV, tm), lambda i: (0, i)),
            pl.BlockSpec((tm, 3 * D_C), lambda i: (i, 0)),
        ],
        out_shape=[
            jax.ShapeDtypeStruct((n_tok, 2 * D_A), BF16),
            jax.ShapeDtypeStruct((n_tok, D_B + D_KV), BF16),
            jax.ShapeDtypeStruct((D_KV, n_tok), BF16),
            jax.ShapeDtypeStruct((n_tok, 3 * D_C), BF16),
        ],
        compiler_params=_cparams(("parallel",)),
        name="in_proj",
    )(x2, g, w_bf, rc, rs)


def _dwconv(pad_ref, r0, rows, w, b, pad_left):
    return _conv_taps(pad_ref[pl.ds(r0, rows + 2 * SUBLANES), :], rows, w, b, pad_left, SUBLANES)


def _conv_taps(win, rows, w, b, pad_left, halo):
    total = rows + 2 * halo
    acc = None
    for k in range(w.shape[0]):
        shift = (pad_left - k) % total
        tap = win if shift == 0 else pltpu.roll(win, shift, axis=0)
        term = tap[halo:halo + rows] * w[k:k + 1, :]
        acc = term if acc is None else acc + term
    return acc + b


def _dwconv_block(src_ref, j, nblk, w, b, pad_left):
    r0 = j * DFT_BLK
    halo = 2 * SUBLANES
    zeros = jnp.zeros((halo, src_ref.shape[1]), src_ref.dtype)
    top = zeros if j == 0 else src_ref[r0 - halo:r0, :]
    bot = zeros if j == nblk - 1 else src_ref[r0 + DFT_BLK:r0 + DFT_BLK + halo, :]
    win = jnp.concatenate([top, src_ref[r0:r0 + DFT_BLK, :], bot], axis=0).astype(F32)
    return _conv_taps(win, DFT_BLK, w, b, pad_left, halo)


def _fill_padded(pad_ref, src_ref, seq):
    zeros = jnp.zeros((SUBLANES, pad_ref.shape[1]), F32)
    pad_ref[0:SUBLANES, :] = zeros
    pad_ref[seq + SUBLANES:seq + 2 * SUBLANES, :] = zeros
    pad_ref[SUBLANES:seq + SUBLANES, :] = src_ref[...].astype(F32)


def _softplus(x):
    return jnp.maximum(x, 0.0) + jnp.log1p(jnp.exp(-jnp.abs(x)))


def _gelu_tanh(x):
    c = math.sqrt(2.0 / math.pi)
    return x * (0.5 * (1.0 + jnp.tanh(c * (x + 0.044715 * (x * x * x)))))


def _lru_kernel(u_ref, gate_ref, cw_ref, cb_ref, w_ref, bias_ref, lam_ref, o_ref,
                upad, xci, gg, af, bf, ar, br, pfs, hfs, prs, hrs, g0, g1, *, seq, rows, unroll):
    seg = seq // SCAN_SEGS
    tiles = rows // SCAN_SEGS
    nchunk = seq // rows
    per_seg = seg // rows
    _fill_padded(upad, u_ref, seq)
    cw = cw_ref[...]
    cb = cb_ref[...]
    w = w_ref[0] * 0.5
    bias = bias_ref[0] * 0.5
    hnsp = (-0.5 * C_LRU) * _softplus(-lam_ref[0])

    def conv(ci, carry):
        r0 = pl.multiple_of(ci * rows, rows)
        dst = pl.ds((ci % per_seg) * (rows * SCAN_SEGS) + ci // per_seg, rows, stride=SCAN_SEGS)
        xci[dst, :] = _dwconv(upad, r0, rows, cw, cb, 2)
        gg[dst, :] = _gelu_tanh(gate_ref[pl.ds(r0, rows), :].astype(F32))
        return carry

    lax.fori_loop(0, nchunk, conv, 0)

    grows = g0.shape[0]
    ngate = seq // grows

    def gate_matmul(ci, g_ref):
        r0 = pl.multiple_of(ci * grows, grows)
        g_ref[...] = jnp.dot(xci[pl.ds(r0, grows), :].astype(BF16), w, preferred_element_type=F32)

    def gate_math(ci, g_ref):
        r0 = pl.multiple_of(ci * grows, grows)
        xc = xci[pl.ds(r0, grows), :]
        t = jnp.tanh(g_ref[...] + bias)
        hxc = 0.5 * xc
        for d, (a_ref, b_ref) in enumerate(((af, bf), (ar, br))):
            tr = t[:, (2 * d) * LANES:(2 * d + 1) * LANES]
            ti = t[:, (2 * d + 1) * LANES:(2 * d + 2) * LANES]
            hn = hnsp[:, d * LANES:(d + 1) * LANES]
            log_a = tr * hn + hn
            a = jnp.exp(log_a)
            nem = (-1.0 - a * a) * jnp.tanh(log_a)
            a_ref[pl.ds(r0, grows), :] = a
            b_ref[pl.ds(r0, grows), :] = jnp.sqrt(nem) * (ti * hxc + hxc)

    def gates(k, carry):
        gate_matmul(2 * k + 1, g1)
        gate_math(2 * k, g0)
        gate_matmul(jnp.minimum(2 * k + 2, ngate - 1), g0)
        gate_math(2 * k + 1, g1)
        return carry

    gate_matmul(0, g0)
    lax.fori_loop(0, ngate // 2, gates, 0)

    def scan(it, carry):
        def two_steps(p, h, a_ref, b_ref, p_out, h_out, r0, r1):
            a0 = a_ref[pl.ds(r0, SCAN_SEGS), :]
            b0 = b_ref[pl.ds(r0, SCAN_SEGS), :]
            a1 = a_ref[pl.ds(r1, SCAN_SEGS), :]
            a01 = a1 * a0
            b01 = a1 * b0 + b_ref[pl.ds(r1, SCAN_SEGS), :]
            p_out[pl.ds(r0, SCAN_SEGS), :] = a0 * p
            h_out[pl.ds(r0, SCAN_SEGS), :] = a0 * h + b0
            p = a01 * p
            h = a01 * h + b01
            p_out[pl.ds(r1, SCAN_SEGS), :] = p
            h_out[pl.ds(r1, SCAN_SEGS), :] = h
            return p, h

        pf, hf, pr, hr = carry
        for u in range(0, unroll, 2):
            i = it * unroll + u
            row = lambda pos: pl.multiple_of(pos * SCAN_SEGS, SCAN_SEGS)
            pf, hf = two_steps(pf, hf, af, bf, pfs, hfs, row(i), row(i + 1))
            pr, hr = two_steps(pr, hr, ar, br, prs, hrs, row(seg - 1 - i), row(seg - 2 - i))
        return pf, hf, pr, hr

    one = jnp.ones((SCAN_SEGS, LANES), F32)
    zero = jnp.zeros((SCAN_SEGS, LANES), F32)
    pf, hf, pr, hr = lax.fori_loop(0, seg // unroll, scan, (one, zero, one, zero))

    sub = lax.broadcasted_iota(jnp.int32, (SCAN_SEGS, LANES), 0)
    cf = zero
    cr = zero
    for _ in range(SCAN_SEGS - 1):
        cf = jnp.where(sub == 0, 0.0, pltpu.roll(hf + pf * cf, 1, axis=0))
        cr = jnp.where(sub == SCAN_SEGS - 1, 0.0, pltpu.roll(hr + pr * cr, SCAN_SEGS - 1, axis=0))

    def combine(ci, carry):
        r0 = pl.multiple_of(ci * rows, rows)
        sl = pl.ds(r0, rows)
        tile3 = lambda ref: ref[sl, :].reshape(tiles, SCAN_SEGS, LANES)
        h = (tile3(hfs) + tile3(pfs) * cf[None]) + (tile3(hrs) + tile3(prs) * cr[None])
        out = h.reshape(rows, LANES) * gg[sl, :]
        for k in range(tiles):
            o_ref[pl.ds(ci * tiles + k, SCAN_SEGS, stride=seg), :] = (
                out[k * SCAN_SEGS:(k + 1) * SCAN_SEGS])
        return carry

    lax.fori_loop(0, nchunk, combine, 0)


def _lru(pa, cw, cb, wg, bias, lam, batch, seq, rows=128, gate_rows=512, unroll=16):
    n_tok = pa.shape[0]
    ng = D_A // LANES
    kern = functools.partial(_lru_kernel, seq=seq, rows=rows, unroll=unroll)
    return pl.pallas_call(
        kern,
        grid=(batch, ng),
        in_specs=[
            pl.BlockSpec((seq, LANES), lambda b, g: (b, g)),
            pl.BlockSpec((seq, LANES), lambda b, g: (b, ng + g)),
            pl.BlockSpec((cw.shape[0], LANES), lambda b, g: (0, g)),
            pl.BlockSpec((1, LANES), lambda b, g: (0, g)),
            pl.BlockSpec((1, LANES, 4 * LANES), lambda b, g: (g, 0, 0)),
            pl.BlockSpec((1, 1, 4 * LANES), lambda b, g: (g, 0, 0)),
            pl.BlockSpec((1, 1, 2 * LANES), lambda b, g: (g, 0, 0)),
        ],
        out_specs=pl.BlockSpec((seq, LANES), lambda b, g: (b, g)),
        out_shape=jax.ShapeDtypeStruct((n_tok, D_A), F32),
        scratch_shapes=[pltpu.VMEM((seq + 2 * SUBLANES, LANES), F32)]
        + [pltpu.VMEM((seq, LANES), F32) for _ in range(10)]
        + [pltpu.VMEM((gate_rows, 4 * LANES), F32) for _ in range(2)],
        compiler_params=_cparams(("parallel", "parallel")),
        name="rglru",
    )(pa, pa, cw, cb, wg, bias, lam)


_ATTN_STRAIGHT = (0, 2, 3, 5)
_ATTN_ROLLED = (1, 4)
_ONES_ROWS = 16


def _attn_kernel(sink_ref, bias_ref, q_ref, k_ref, vt_ref, o_ref, s_scr, p_scr, e_scr, *, seq):
    band = 3 * BLOCK
    nblk = seq // BLOCK
    lo_q = lax.broadcasted_iota(jnp.int32, (BLOCK, 2 * HEAD_DIM), 1) < HEAD_DIM
    ones = jnp.ones((_ONES_ROWS, band), BF16)
    nt = (((1,), (1,)), ((), ()))
    stack = _ATTN_STRAIGHT + _ATTN_ROLLED
    ns = len(_ATTN_STRAIGHT) * BLOCK

    def window(j):
        q0 = j * BLOCK
        return q0, pl.multiple_of(jnp.clip(q0 - BLOCK, 0, seq - band), BLOCK)

    def scores(j, slot):
        q0, k0 = window(j)
        rows = pl.ds(pl.multiple_of(q0, BLOCK), BLOCK)
        kb = k_ref[pl.ds(k0, band), :]
        kbs = pltpu.roll(kb, HEAD_DIM, axis=1)

        def own_half(h):
            qt = q_ref[rows, (h // 2) * 2 * HEAD_DIM:(h // 2 + 1) * 2 * HEAD_DIM]
            return jnp.where(lo_q, qt, 0.0) if h % 2 == 0 else jnp.where(lo_q, 0.0, qt)

        qa = jnp.concatenate([own_half(h) for h in _ATTN_STRAIGHT], axis=0)
        qb = jnp.concatenate([own_half(h) for h in _ATTN_ROLLED], axis=0)
        s_scr[slot, :, :ns] = lax.dot_general(kb, qa, nt, preferred_element_type=F32)
        s_scr[slot, :, ns:] = lax.dot_general(kbs, qb, nt, preferred_element_type=F32)

    def softmax(j, slot):
        q0, k0 = window(j)
        bias = bias_ref[(q0 - k0) // BLOCK]
        for h in range(N_Q_HEADS):
            src = stack.index(h) * BLOCK
            s = s_scr[slot, :, src:src + BLOCK] + bias
            sk = sink_ref[h] * LOG2E
            m = jnp.maximum(jnp.max(s, axis=0, keepdims=True), sk)
            p_scr[slot, :, h * BLOCK:(h + 1) * BLOCK] = jnp.exp2(s - m).astype(BF16)
            e_scr[slot, :, h * BLOCK:(h + 1) * BLOCK] = jnp.broadcast_to(
                jnp.exp2(sk - m), (SUBLANES, BLOCK))

    def values(j, slot):
        q0, k0 = window(j)
        vt = vt_ref[:, pl.ds(k0, band)]
        outs = []
        for kv in range(N_KV_HEADS):
            cols = slice(kv * GROUP * BLOCK, (kv + 1) * GROUP * BLOCK)
            lhs = jnp.concatenate([vt[kv * HEAD_DIM:(kv + 1) * HEAD_DIM], ones], axis=0)
            ov = jnp.dot(lhs, p_scr[slot, :, cols], preferred_element_type=F32)
            res = ov[:HEAD_DIM] / (ov[HEAD_DIM:HEAD_DIM + 1] + e_scr[slot, 0:1, cols])
            outs += [res[:, g * BLOCK:(g + 1) * BLOCK] for g in range(GROUP)]
        rows = pl.ds(pl.multiple_of(q0, BLOCK), BLOCK)
        o_ref[rows, :] = jnp.concatenate(outs, axis=0).T

    scores(0, 0)
    softmax(0, 0)
    scores(1, 1)

    def step(jj, carry):
        j = 2 * jj
        values(j, 0)
        softmax(j + 1, 1)
        scores(j + 2, 0)
        values(j + 1, 1)
        softmax(j + 2, 0)
        scores(j + 3, 1)
        return carry

    lax.fori_loop(0, (nblk - 2) // 2, step, 0)
    values(nblk - 2, 0)
    softmax(nblk - 1, 1)
    values(nblk - 1, 1)


def _attn_bias_table():
    ki = np.arange(3 * BLOCK)[None, :, None]
    qi = np.arange(BLOCK)[None, None, :]
    off = (np.arange(3) * BLOCK)[:, None, None]
    return jnp.asarray(np.where(np.abs(off + qi - ki) <= WINDOW, 0.0, NEG).astype(np.float32))


def _attention(pq, vt, sink, bias, batch, seq):
    n_tok = pq.shape[0]
    band = 3 * BLOCK
    kern = functools.partial(_attn_kernel, seq=seq)
    return pl.pallas_call(
        kern,
        grid=(batch,),
        in_specs=[
            pl.BlockSpec(memory_space=pltpu.SMEM),
            pl.BlockSpec(bias.shape, lambda b: (0, 0, 0)),
            pl.BlockSpec((seq, D_B), lambda b: (b, 0)),
            pl.BlockSpec((seq, D_KV), lambda b: (b, D_B // D_KV)),
            pl.BlockSpec((D_KV, seq), lambda b: (0, b)),
        ],
        out_specs=pl.BlockSpec((seq, D_B), lambda b: (b, 0)),
        out_shape=jax.ShapeDtypeStruct((n_tok, D_B), F32),
        scratch_shapes=[pltpu.VMEM((2, band, N_Q_HEADS * BLOCK), F32),
                        pltpu.VMEM((2, band, N_Q_HEADS * BLOCK), BF16),
                        pltpu.VMEM((2, SUBLANES, N_Q_HEADS * BLOCK), F32)],
        compiler_params=_cparams(("parallel",)),
        name="win_attn",
    )(sink, bias, pq, pq, vt)


def _hy_filter_kernel(z_ref, w1_ref, b1_ref, fr_ref, w2_ref, b2_ref, w3_ref, dec_ref, o_ref):
    hi = lax.Precision.HIGHEST
    fr = fr_ref[...]
    h = jnp.sin(fr * (jnp.dot(z_ref[...], w1_ref[...], preferred_element_type=F32, precision=hi)
                      + b1_ref[...]))
    h = jnp.sin(fr * (jnp.dot(h, w2_ref[...], preferred_element_type=F32, precision=hi) + b2_ref[...]))
    f = jnp.dot(h, w3_ref[...], preferred_element_type=F32, precision=hi) * dec_ref[...]
    for half in range(2):
        for d in range(2):
            c0 = (2 * half + d) * D_C
            o_ref[d, half] = f[:, c0:c0 + D_C]


def _hy_filter(zemb2, w1, b1, fr, w2, b2, w3, dec4, rows=512):
    n = zemb2.shape[0]
    full = lambda a: pl.BlockSpec(a.shape, lambda i: (0,) * a.ndim)
    return pl.pallas_call(
        _hy_filter_kernel,
        grid=(n // rows,),
        in_specs=[pl.BlockSpec((rows, zemb2.shape[1]), lambda i: (i, 0)),
                  full(w1), full(b1), full(fr), full(w2), full(b2), full(w3),
                  pl.BlockSpec((rows, 4 * D_C), lambda i: (i, 0))],
        out_specs=pl.BlockSpec((2, 2, rows, D_C), lambda i: (0, 0, i, 0)),
        out_shape=jax.ShapeDtypeStruct((2, 2, n, D_C), F32),
        compiler_params=_cparams(("parallel",)),
        name="hyena_filter",
    )(zemb2, w1, b1, fr, w2, b2, w3, dec4)


def _dft_outer_fwd(zs, ka_ref, o_ref):
    nj, _, lanes = zs.shape
    for p in range(DFT_BLK // DFT_SUB):
        sl = slice(p * DFT_SUB, (p + 1) * DFT_SUB)
        xg = zs[:, sl, :].reshape(nj * DFT_SUB, lanes).astype(BF16)
        c = jnp.dot(ka_ref[...], xg, preferred_element_type=F32)
        o_ref[0, :, :, sl, :] = c.reshape(2, DFT_NBLK, DFT_SUB, lanes).astype(BF16)


def _hy_fwd_data_kernel(x1_ref, v_ref, cw_ref, cb_ref, ka_ref, o_ref, zs, *, seq):
    nj = seq // DFT_BLK
    cw = cw_ref[...]
    cb = cb_ref[...]
    for j in range(nj):
        x1 = _dwconv_block(x1_ref, j, nj, cw[:, D_C:2 * D_C], cb[:, D_C:2 * D_C], 1)
        v = _dwconv_block(v_ref, j, nj, cw[:, 2 * D_C:], cb[:, 2 * D_C:], 1)
        zs[j] = v * x1
    _dft_outer_fwd(zs, ka_ref, o_ref)


def _hy_fwd_filt_kernel(f_ref, ka_ref, o_ref):
    _dft_outer_fwd(f_ref.at[0], ka_ref, o_ref)


def _hy_fwd_data(pc, cw, cb, ka, batch, seq):
    nj = seq // DFT_BLK
    kern = functools.partial(_hy_fwd_data_kernel, seq=seq)
    return pl.pallas_call(
        kern,
        grid=(batch,),
        in_specs=[
            pl.BlockSpec((seq, D_C), lambda b: (b, 1)),
            pl.BlockSpec((seq, D_C), lambda b: (b, 2)),
            pl.BlockSpec(cw.shape, lambda b: (0, 0)),
            pl.BlockSpec(cb.shape, lambda b: (0, 0)),
            pl.BlockSpec(ka.shape, lambda b: (0, 0)),
        ],
        out_specs=pl.BlockSpec((1, 2, DFT_NBLK, DFT_BLK, D_C), lambda b: (b, 0, 0, 0, 0)),
        out_shape=jax.ShapeDtypeStruct((batch, 2, DFT_NBLK, DFT_BLK, D_C), BF16),
        scratch_shapes=[pltpu.VMEM((nj, DFT_BLK, D_C), F32)],
        compiler_params=_cparams(("parallel",)),
        name="hyena_dft_outer",
    )(pc, pc, cw, cb, ka)


def _hy_fwd_filt(filt4, ka):
    ndir, nj = filt4.shape[0], filt4.shape[1]
    return pl.pallas_call(
        _hy_fwd_filt_kernel,
        grid=(ndir,),
        in_specs=[
            pl.BlockSpec((1, nj, DFT_BLK, D_C), lambda b: (b, 0, 0, 0)),
            pl.BlockSpec(ka.shape, lambda b: (0, 0)),
        ],
        out_specs=pl.BlockSpec((1, 2, DFT_NBLK, DFT_BLK, D_C), lambda b: (b, 0, 0, 0, 0)),
        out_shape=jax.ShapeDtypeStruct((ndir, 2, DFT_NBLK, DFT_BLK, D_C), BF16),
        compiler_params=_cparams(("parallel",)),
        name="hyena_dft_outer_filter",
    )(filt4, ka)


def _hy_inner_kernel(c_ref, f_ref, g_ref, gi_ref, o_ref, *, batch, kper):
    for q in range(kper):
        g = g_ref[q]
        gi = gi_ref[q]
        hf = jnp.dot(g, f_ref[0, :, q].reshape(2 * DFT_BLK, D_C), preferred_element_type=F32)
        hb = jnp.dot(g, f_ref[1, :, q].reshape(2 * DFT_BLK, D_C), preferred_element_type=F32)
        hre = hf[:DFT_BLK] + hb[:DFT_BLK]
        him = hf[DFT_BLK:] - hb[DFT_BLK:]
        for b in range(batch):
            x = jnp.dot(g, c_ref[b, :, q].reshape(2 * DFT_BLK, D_C), preferred_element_type=F32)
            xre, xim = x[:DFT_BLK], x[DFT_BLK:]
            y = jnp.concatenate([xre * hre - xim * him, xre * him + xim * hre], axis=0)
            d = jnp.dot(gi, y.astype(BF16), preferred_element_type=F32)
            o_ref[b, :, q] = d.reshape(2, DFT_BLK, D_C).astype(BF16)


def _hy_inner(cdata, cfilt, g, gi, kper=4):
    batch = cdata.shape[0]
    kern = functools.partial(_hy_inner_kernel, batch=batch, kper=kper)
    blk = lambda nb: pl.BlockSpec((nb, 2, kper, DFT_BLK, D_C), lambda k: (0, 0, k, 0, 0))
    return pl.pallas_call(
        kern,
        grid=(DFT_NBLK // kper,),
        in_specs=[blk(batch), blk(cfilt.shape[0]),
                  pl.BlockSpec((kper, 2 * DFT_BLK, 2 * DFT_BLK), lambda k: (k, 0, 0)),
                  pl.BlockSpec((kper, 2 * DFT_BLK, 2 * DFT_BLK), lambda k: (k, 0, 0))],
        out_specs=blk(batch),
        out_shape=jax.ShapeDtypeStruct(cdata.shape, BF16),
        compiler_params=_cparams(("parallel",)),
        name="hyena_dft_inner",
    )(cdata, cfilt, g, gi)


def _hy_out_kernel(d_ref, x0_ref, x1_ref, v_ref, cw_ref, cb_ref, hb_ref, kai_ref, o_ref, ys, *, seq):
    nj = seq // DFT_BLK
    for p in range(DFT_BLK // DFT_SUB):
        sl = slice(p * DFT_SUB, (p + 1) * DFT_SUB)
        rhs = d_ref[0, :, :, sl, :].reshape(2 * DFT_NBLK * DFT_SUB, D_C)
        y = jnp.dot(kai_ref[...], rhs, preferred_element_type=F32)
        ys[:, sl, :] = y.reshape(nj, DFT_SUB, D_C)
    cw = cw_ref[...]
    cb = cb_ref[...]
    hb = hb_ref[...]
    for j in range(nj):
        x0 = _dwconv_block(x0_ref, j, nj, cw[:, :D_C], cb[:, :D_C], 1)
        x1 = _dwconv_block(x1_ref, j, nj, cw[:, D_C:2 * D_C], cb[:, D_C:2 * D_C], 1)
        v = _dwconv_block(v_ref, j, nj, cw[:, 2 * D_C:], cb[:, 2 * D_C:], 1)
        z = v * x1
        o_ref[j * DFT_BLK:(j + 1) * DFT_BLK, :] = (ys[j] + z * hb) * x0


def _hy_out(dd, pc, cw, cb, hb, kai, batch, seq):
    n_tok = pc.shape[0]
    nj = seq // DFT_BLK
    kern = functools.partial(_hy_out_kernel, seq=seq)
    whole = lambda a: pl.BlockSpec(a.shape, lambda b: (0, 0))
    return pl.pallas_call(
        kern,
        grid=(batch,),
        in_specs=[
            pl.BlockSpec((1, 2, DFT_NBLK, DFT_BLK, D_C), lambda b: (b, 0, 0, 0, 0)),
            pl.BlockSpec((seq, D_C), lambda b: (b, 0)),
            pl.BlockSpec((seq, D_C), lambda b: (b, 1)),
            pl.BlockSpec((seq, D_C), lambda b: (b, 2)),
            whole(cw), whole(cb), whole(hb), whole(kai),
        ],
        out_specs=pl.BlockSpec((seq, D_C), lambda b: (b, 0)),
        out_shape=jax.ShapeDtypeStruct((n_tok, D_C), F32),
        scratch_shapes=[pltpu.VMEM((nj, DFT_BLK, D_C), F32)],
        compiler_params=_cparams(("parallel",)),
        name="hyena_out",
    )(dd, pc, pc, pc, cw, cb, hb, kai)


def _mix_mlp_kernel(ya_ref, yb_ref, yc_ref, x_ref, ga_ref, gb_ref, gc_ref, wo_ref, g_ref, wu_ref,
                    wd_ref, gf_ref, o_ref, *, ff_chunk, final_norm):
    y = jnp.concatenate([_rms(ya_ref[...], ga_ref[...]), _rms(yb_ref[...], gb_ref[...]),
                         _rms(yc_ref[...], gc_ref[...])], axis=-1).astype(BF16)
    x = x_ref[...] + jnp.dot(y, wo_ref[...], preferred_element_type=F32)
    h = _rms(x, g_ref[...]).astype(BF16)
    acc = x
    for c in range(D_FF // ff_chunk):
        sl = slice(c * ff_chunk, (c + 1) * ff_chunk)
        u = jnp.maximum(jnp.dot(h, wu_ref[:, sl], preferred_element_type=F32), 0.0)
        acc = acc + jnp.dot((u * u).astype(BF16), wd_ref[sl, :], preferred_element_type=F32)
    if final_norm:
        acc = _rms(acc, gf_ref[...])
    o_ref[...] = acc


def _mix_mlp(ya, yb, yc, x2, ga, gb, gc, wo_bf, g, wu_bf, wd_bf, gf, tm, final_norm, ff_chunk=1024):
    n_tok = x2.shape[0]
    kern = functools.partial(_mix_mlp_kernel, ff_chunk=ff_chunk, final_norm=final_norm)
    row = lambda w: pl.BlockSpec((tm, w), lambda i: (i, 0))
    vec = lambda w: pl.BlockSpec((1, w), lambda i: (0, 0))
    resident = lambda r, c: pl.BlockSpec((r, c), lambda i: (0, 0), pipeline_mode=pl.Buffered(1))
    return pl.pallas_call(
        kern,
        grid=(n_tok // tm,),
        in_specs=[row(D_A), row(D_B), row(D_C), row(D_MODEL), vec(D_A), vec(D_B), vec(D_C),
                  resident(D_MODEL, D_MODEL), vec(D_MODEL), resident(D_MODEL, D_FF),
                  resident(D_FF, D_MODEL), vec(D_MODEL)],
        out_specs=row(D_MODEL),
        out_shape=jax.ShapeDtypeStruct((n_tok, D_MODEL), F32),
        compiler_params=_cparams(("parallel",)),
        name="mix_mlp",
    )(ya, yb, yc, x2, ga, gb, gc, wo_bf, g, wu_bf, wd_bf, gf)


def _rope_tables(seq):
    pos = np.arange(seq, dtype=np.float32)
    inv_freq = (np.float32(ROPE_THETA) ** (-np.arange(0, ROT_DIM, 2, dtype=np.float32) / ROT_DIM))
    ang = (pos[:, None] * inv_freq[None, :]).astype(np.float32)
    cos, sin = np.cos(ang).astype(np.float32), np.sin(ang).astype(np.float32)
    rest = HEAD_DIM - ROT_DIM
    c_head = np.concatenate([cos, cos, np.ones((seq, rest), np.float32)], axis=1)
    s_head = np.concatenate([-sin, sin, np.zeros((seq, rest), np.float32)], axis=1)
    scale = np.float32(HEAD_DIM ** -0.5 * LOG2E)
    c = np.concatenate([np.tile(c_head, (1, N_Q_HEADS)) * scale, np.tile(c_head, (1, N_KV_HEADS))], axis=1)
    s = np.concatenate([np.tile(s_head, (1, N_Q_HEADS)) * scale, np.tile(s_head, (1, N_KV_HEADS))], axis=1)
    return jnp.asarray(c), jnp.asarray(s)


def _hyena_position_tables(seq):
    t = np.linspace(0.0, 1.0, seq, dtype=np.float32)[:, None]
    w = (2.0 * math.pi * np.arange(seq, dtype=np.float32)[:, None] / seq).astype(np.float32)
    f = np.linspace(1e-4, HY_BANDS - 1, HY_BANDS, dtype=np.float32)[None, :]
    fw = (f * w).astype(np.float32)
    z = np.concatenate([t, np.cos(fw), -np.sin(fw)], axis=-1).astype(np.float32)
    deltas = np.abs(np.linspace(HY_MIN_DECAY, HY_MAX_DECAY, D_C, dtype=np.float32))
    decay = np.exp(-t * deltas[None, :]).astype(np.float32)
    decay_b = decay.copy()
    decay_b[0] = 0.0
    half = seq // 2
    zp = np.zeros((half, 2, HY_WIDTH), np.float32)
    zp[:, 0, :HY_EMB] = z[:half]
    zp[:, 1, :HY_EMB] = z[half:]
    dec4 = np.concatenate([decay[:half], decay_b[:half], decay[half:], decay_b[half:]], axis=1)
    return jnp.asarray(zp.reshape(half, 2 * HY_WIDTH)), jnp.asarray(dec4)


def _blockdiag2(a):
    z = jnp.zeros_like(a)
    return jnp.concatenate([jnp.concatenate([a, z], axis=1), jnp.concatenate([z, a], axis=1)], axis=0)


def _dft_tables(seq):
    n = 2 * seq
    nj = seq // DFT_BLK
    kk = np.arange(DFT_NBLK)
    ang = 2.0 * np.pi * np.outer(kk, np.arange(nj)) / DFT_NBLK
    eye = np.eye(DFT_SUB)
    ka = np.concatenate([np.kron(np.cos(ang), eye), np.kron(-np.sin(ang), eye)], axis=0)
    kai = np.concatenate([np.kron(np.cos(ang).T, eye), np.kron(-np.sin(ang).T, eye)], axis=1) / n
    m = np.arange(DFT_BLK)
    k = kk[:, None, None] + DFT_NBLK * np.arange(DFT_BLK)[None, :, None]
    ph = 2.0 * np.pi * ((k * m[None, None, :]) % n) / n
    gre, gim = np.cos(ph), -np.sin(ph)
    g = np.concatenate([np.concatenate([gre, -gim], axis=2), np.concatenate([gim, gre], axis=2)], axis=1)
    gi = np.transpose(g, (0, 2, 1))
    as_bf = lambda a: jnp.asarray(a.astype(np.float32)).astype(BF16)
    return as_bf(ka), as_bf(kai), as_bf(g), as_bf(gi)


def _lru_blockdiag(w):
    nb = w.shape[1] // 2
    w = w.reshape(2, nb, 2, w.shape[2], w.shape[3])
    z = jnp.zeros_like(w[:, :, 0])
    top = jnp.concatenate([w[:, :, 0], z], axis=-1)
    bot = jnp.concatenate([z, w[:, :, 1]], axis=-1)
    return jnp.concatenate([top, bot], axis=-2)


def kernel(x, norm_mix_g, w_in, conv_a_w, conv_a_b, lru_wa, lru_ba, lru_wx, lru_bx, lru_lambda,
           attn_sink, hy_conv_w, hy_conv_b, hy_w1, hy_b1, hy_freq, hy_w2, hy_b2, hy_w3, hy_bias,
           gnorm_a, gnorm_b, gnorm_c, w_out, norm_mlp_g, w_up, w_down, final_norm_g):
    batch, seq, _ = x.shape
    depth = w_in.shape[0]
    n_tok = batch * seq
    tm = 1024
    tm_mlp = 512
    ng = D_A // LANES

    rc, rs = _rope_tables(seq)
    attn_bias = _attn_bias_table()
    zemb, dec4 = _hyena_position_tables(seq)
    ka, kai, gtab, gitab = _dft_tables(seq)

    xs = x.reshape(n_tok, D_MODEL)
    for i in range(depth):
        pa, pq, vt, pc = _in_proj(xs, norm_mix_g[i][None], w_in[i].astype(BF16), rc, rs, seq, tm)

        wa, wx = _lru_blockdiag(lru_wa[i]), _lru_blockdiag(lru_wx[i])
        wg = jnp.concatenate([wa[0], wx[0], wa[1], wx[1]], axis=-1).astype(BF16)
        tile = lambda v: v.reshape(ng, 1, LANES)
        bias = jnp.concatenate([tile(lru_ba[i][0]), tile(lru_bx[i][0]),
                                tile(lru_ba[i][1]), tile(lru_bx[i][1])], axis=-1)
        lam = jnp.concatenate([tile(lru_lambda[i][0]), tile(lru_lambda[i][1])], axis=-1)
        y_a = _lru(pa, conv_a_w[i], conv_a_b[i][None], wg, bias, lam, batch, seq)

        y_b = _attention(pq, vt, attn_sink[i], attn_bias, batch, seq)

        w1p = jnp.zeros((HY_WIDTH, HY_WIDTH), F32).at[:HY_EMB].set(hy_w1[i])
        pair = lambda v: jnp.concatenate([v, v])[None]
        filt = _hy_filter(zemb, _blockdiag2(w1p), pair(hy_b1[i]), pair(hy_freq[i]),
                          _blockdiag2(hy_w2[i]), pair(hy_b2[i]), _blockdiag2(hy_w3[i]), dec4)
        cw, cb = hy_conv_w[i], hy_conv_b[i][None]
        cdata = _hy_fwd_data(pc, cw, cb, ka, batch, seq)
        cfilt = _hy_fwd_filt(filt.reshape(2, seq // DFT_BLK, DFT_BLK, D_C), ka)
        dd = _hy_inner(cdata, cfilt, gtab, gitab)
        y_c = _hy_out(dd, pc, cw, cb, hy_bias[i][None], kai, batch, seq)

        xs = _mix_mlp(y_a, y_b, y_c, xs, gnorm_a[i][None], gnorm_b[i][None], gnorm_c[i][None],
                      w_out[i].astype(BF16), norm_mlp_g[i][None], w_up[i].astype(BF16),
                      w_down[i].astype(BF16), final_norm_g[None], tm, final_norm=(i == depth - 1))
    return xs.reshape(batch, seq, D_MODEL)
```

```python
import functools
import math

import numpy as np
import jax
import jax.numpy as jnp
from jax import lax
from jax.experimental import pallas as pl
from jax.experimental.pallas import tpu as pltpu

F32 = jnp.float32
BF16 = jnp.bfloat16

D_MODEL = 1024
D_A = 384
D_B = 384
D_C = 256
HEAD_DIM = 64
N_Q_HEADS = 6
N_KV_HEADS = 2
GROUP = N_Q_HEADS // N_KV_HEADS
D_KV = N_KV_HEADS * HEAD_DIM
D_QKV = D_B + 2 * D_KV
D_IN = 2 * D_A + D_QKV + 3 * D_C
C_LRU = 8.0
WINDOW = 128
BLOCK = 128
ROPE_THETA = 500000.0
ROT_DIM = HEAD_DIM // 4
HY_EMB = 33
HY_BANDS = (HY_EMB - 1) // 2
HY_WIDTH = 64
HY_TARGET = 1e-2
HY_MAX_DECAY = math.log(HY_TARGET) / 0.3
HY_MIN_DECAY = math.log(HY_TARGET) / 1.5
D_FF = 4 * D_MODEL
EPS = 1e-6
NEG = -1e30
LOG2E = math.log2(math.e)

LANES = 128
SUBLANES = 8
VMEM_LIMIT = 56 * 1024 * 1024

SCAN_SEGS = SUBLANES
DFT_BLK = 128
DFT_NBLK = 64
DFT_SUB = 8
BF16_ROWS = 16
IN_PROJ_CHUNKS = 2


def _cparams(sem):
    return pltpu.CompilerParams(dimension_semantics=sem, vmem_limit_bytes=VMEM_LIMIT)


def _rms(x, g):
    return x * lax.rsqrt(jnp.mean(x * x, axis=-1, keepdims=True) + EPS) * g


def _in_proj_kernel(x_ref, g_ref, w_ref, rc_ref, rs_ref, oa_ref, oq_ref, ov_ref, oc_ref):
    n = D_B + D_KV
    half = ROT_DIM // 2
    tm = x_ref.shape[0]
    for c in range(IN_PROJ_CHUNKS):
        r = slice(c * tm // IN_PROJ_CHUNKS, (c + 1) * tm // IN_PROJ_CHUNKS)
        h = _rms(x_ref[r, :], g_ref[...]).astype(BF16)
        oa_ref[r, :] = jnp.dot(h, w_ref[:, :2 * D_A], preferred_element_type=F32).astype(BF16)
        oc_ref[r, :] = jnp.dot(h, w_ref[:, 2 * D_A + D_QKV:], preferred_element_type=F32).astype(BF16)
        qkv = jnp.dot(h, w_ref[:, 2 * D_A:2 * D_A + D_QKV], preferred_element_type=F32)
        qk = qkv[:, :n]
        lane = lax.broadcasted_iota(jnp.int32, qk.shape, 1) % HEAD_DIM
        swapped = jnp.where(lane < half, pltpu.roll(qk, n - half, axis=1), pltpu.roll(qk, half, axis=1))
        oq_ref[r, :] = (qk * rc_ref[r, :] + swapped * rs_ref[r, :]).astype(BF16)
        ov_ref[:, r] = qkv[:, n:].T.astype(BF16)


def _in_proj(x2, g, w_bf, rc, rs, seq, tm):
    n_tok = x2.shape[0]
    nrb = seq // tm
    return pl.pallas_call(
        _in_proj_kernel,
        grid=(n_tok // tm,),
        in_specs=[
            pl.BlockSpec((tm, D_MODEL), lambda i: (i, 0)),
            pl.BlockSpec((1, D_MODEL), lambda i: (0, 0)),
            pl.BlockSpec((D_MODEL, D_IN), lambda i: (0, 0)),
            pl.BlockSpec((tm, D_B + D_KV), lambda i: (i % nrb, 0)),
            pl.BlockSpec((tm, D_B + D_KV), lambda i: (i % nrb, 0)),
        ],
        out_specs=[
            pl.BlockSpec((tm, 2 * D_A), lambda i: (i, 0)),
            pl.BlockSpec((tm, D_B + D_KV), lambda i: (i, 0)),
            pl.BlockSpec((D_KV, tm), lambda i: (0, i)),
            pl.BlockSpec((tm, 3 * D_C), lambda i: (i, 0)),
        ],
        out_shape=[
            jax.ShapeDtypeStruct((n_tok, 2 * D_A), BF16),
            jax.ShapeDtypeStruct((n_tok, D_B + D_KV), BF16),
            jax.ShapeDtypeStruct((D_KV, n_tok), BF16),
            jax.ShapeDtypeStruct((n_tok, 3 * D_C), BF16),
        ],
        compiler_params=_cparams(("parallel",)),
        name="in_proj",
    )(x2, g, w_bf, rc, rs)


def _dwconv(pad_ref, r0, rows, w, b, pad_left):
    return _conv_taps(pad_ref[pl.ds(r0, rows + 2 * SUBLANES), :], rows, w, b, pad_left, SUBLANES)


def _conv_taps(win, rows, w, b, pad_left, halo):
    total = rows + 2 * halo
    acc = None
    for k in range(w.shape[0]):
        shift = (pad_left - k) % total
        tap = win if shift == 0 else pltpu.roll(win, shift, axis=0)
        term = tap[halo:halo + rows] * w[k:k + 1, :]
        acc = term if acc is None else acc + term
    return acc + b


def _dwconv_block(src_ref, j, nblk, w, b, pad_left):
    r0 = j * DFT_BLK
    halo = 2 * SUBLANES
    zeros = jnp.zeros((halo, src_ref.shape[1]), src_ref.dtype)
    top = zeros if j == 0 else src_ref[r0 - halo:r0, :]
    bot = zeros if j == nblk - 1 else src_ref[r0 + DFT_BLK:r0 + DFT_BLK + halo, :]
    win = jnp.concatenate([top, src_ref[r0:r0 + DFT_BLK, :], bot], axis=0).astype(F32)
    return _conv_taps(win, DFT_BLK, w, b, pad_left, halo)


def _fill_padded(pad_ref, src_ref, seq):
    zeros = jnp.zeros((SUBLANES, pad_ref.shape[1]), F32)
    pad_ref[0:SUBLANES, :] = zeros
    pad_ref[seq + SUBLANES:seq + 2 * SUBLANES, :] = zeros
    pad_ref[SUBLANES:seq + SUBLANES, :] = src_ref[...].astype(F32)


def _softplus(x):
    return jnp.maximum(x, 0.0) + jnp.log1p(jnp.exp(-jnp.abs(x)))


def _gelu_tanh(x):
    c = math.sqrt(2.0 / math.pi)
    return x * (0.5 * (1.0 + jnp.tanh(c * (x + 0.044715 * (x * x * x)))))


def _lru_kernel(u_ref, gate_ref, cw_ref, cb_ref, w_ref, bias_ref, lam_ref, o_ref,
                upad, xci, gg, af, bf, ar, br, pfs, hfs, prs, hrs, g0, g1, *, seq, rows, unroll):
    seg = seq // SCAN_SEGS
    tiles = rows // SCAN_SEGS
    nchunk = seq // rows
    per_seg = seg // rows
    _fill_padded(upad, u_ref, seq)
    cw = cw_ref[...]
    cb = cb_ref[...]
    w = w_ref[0] * 0.5
    bias = bias_ref[0] * 0.5
    hnsp = (-0.5 * C_LRU) * _softplus(-lam_ref[0])

    def conv(ci, carry):
        r0 = pl.multiple_of(ci * rows, rows)
        dst = pl.ds((ci % per_seg) * (rows * SCAN_SEGS) + ci // per_seg, rows, stride=SCAN_SEGS)
        xci[dst, :] = _dwconv(upad, r0, rows, cw, cb, 2)
        gg[dst, :] = _gelu_tanh(gate_ref[pl.ds(r0, rows), :].astype(F32))
        return carry

    lax.fori_loop(0, nchunk, conv, 0)

    grows = g0.shape[0]
    ngate = seq // grows

    def gate_matmul(ci, g_ref):
        r0 = pl.multiple_of(ci * grows, grows)
        g_ref[...] = jnp.dot(xci[pl.ds(r0, grows), :].astype(BF16), w, preferred_element_type=F32)

    def gate_math(ci, g_ref):
        r0 = pl.multiple_of(ci * grows, grows)
        xc = xci[pl.ds(r0, grows), :]
        t = jnp.tanh(g_ref[...] + bias)
        hxc = 0.5 * xc
        for d, (a_ref, b_ref) in enumerate(((af, bf), (ar, br))):
            tr = t[:, (2 * d) * LANES:(2 * d + 1) * LANES]
            ti = t[:, (2 * d + 1) * LANES:(2 * d + 2) * LANES]
            hn = hnsp[:, d * LANES:(d + 1) * LANES]
            log_a = tr * hn + hn
            a = jnp.exp(log_a)
            nem = (-1.0 - a * a) * jnp.tanh(log_a)
            a_ref[pl.ds(r0, grows), :] = a
            b_ref[pl.ds(r0, grows), :] = jnp.sqrt(nem) * (ti * hxc + hxc)

    def gates(k, carry):
        gate_matmul(2 * k + 1, g1)
        gate_math(2 * k, g0)
        gate_matmul(jnp.minimum(2 * k + 2, ngate - 1), g0)
        gate_math(2 * k + 1, g1)
        return carry

    gate_matmul(0, g0)
    lax.fori_loop(0, ngate // 2, gates, 0)

    def scan(it, carry):
        def two_steps(p, h, a_ref, b_ref, p_out, h_out, r0, r1):
            a0 = a_ref[pl.ds(r0, SCAN_SEGS), :]
            b0 = b_ref[pl.ds(r0, SCAN_SEGS), :]
            a1 = a_ref[pl.ds(r1, SCAN_SEGS), :]
            a01 = a1 * a0
            b01 = a1 * b0 + b_ref[pl.ds(r1, SCAN_SEGS), :]
            p_out[pl.ds(r0, SCAN_SEGS), :] = a0 * p
            h_out[pl.ds(r0, SCAN_SEGS), :] = a0 * h + b0
            p = a01 * p
            h = a01 * h + b01
            p_out[pl.ds(r1, SCAN_SEGS), :] = p
            h_out[pl.ds(r1, SCAN_SEGS), :] = h
            return p, h

        pf, hf, pr, hr = carry
        for u in range(0, unroll, 2):
            i = it * unroll + u
            row = lambda pos: pl.multiple_of(pos * SCAN_SEGS, SCAN_SEGS)
            pf, hf = two_steps(pf, hf, af, bf, pfs, hfs, row(i), row(i + 1))
            pr, hr = two_steps(pr, hr, ar, br, prs, hrs, row(seg - 1 - i), row(seg - 2 - i))
        return pf, hf, pr, hr

    one = jnp.ones((SCAN_SEGS, LANES), F32)
    zero = jnp.zeros((SCAN_SEGS, LANES), F32)
    pf, hf, pr, hr = lax.fori_loop(0, seg // unroll, scan, (one, zero, one, zero))

    sub = lax.broadcasted_iota(jnp.int32, (SCAN_SEGS, LANES), 0)
    cf = zero
    cr = zero
    for _ in range(SCAN_SEGS - 1):
        cf = jnp.where(sub == 0, 0.0, pltpu.roll(hf + pf * cf, 1, axis=0))
        cr = jnp.where(sub == SCAN_SEGS - 1, 0.0, pltpu.roll(hr + pr * cr, SCAN_SEGS - 1, axis=0))

    def combine(ci, carry):
        r0 = pl.multiple_of(ci * rows, rows)
        sl = pl.ds(r0, rows)
        tile3 = lambda ref: ref[sl, :].reshape(tiles, SCAN_SEGS, LANES)
        h = (tile3(hfs) + tile3(pfs) * cf[None]) + (tile3(hrs) + tile3(prs) * cr[None])
        out = h.reshape(rows, LANES) * gg[sl, :]
        for k in range(tiles):
            o_ref[pl.ds(ci * tiles + k, SCAN_SEGS, stride=seg), :] = (
                out[k * SCAN_SEGS:(k + 1) * SCAN_SEGS])
        return carry

    lax.fori_loop(0, nchunk, combine, 0)


def _lru(pa, cw, cb, wg, bias, lam, batch, seq, rows=128, gate_rows=512, unroll=16):
    n_tok = pa.shape[0]
    ng = D_A // LANES
    kern = functools.partial(_lru_kernel, seq=seq, rows=rows, unroll=unroll)
    return pl.pallas_call(
        kern,
        grid=(batch, ng),
        in_specs=[
            pl.BlockSpec((seq, LANES), lambda b, g: (b, g)),
            pl.BlockSpec((seq, LANES), lambda b, g: (b, ng + g)),
            pl.BlockSpec((cw.shape[0], LANES), lambda b, g: (0, g)),
            pl.BlockSpec((1, LANES), lambda b, g: (0, g)),
            pl.BlockSpec((1, LANES, 4 * LANES), lambda b, g: (g, 0, 0)),
            pl.BlockSpec((1, 1, 4 * LANES), lambda b, g: (g, 0, 0)),
            pl.BlockSpec((1, 1, 2 * LANES), lambda b, g: (g, 0, 0)),
        ],
        out_specs=pl.BlockSpec((seq, LANES), lambda b, g: (b, g)),
        out_shape=jax.ShapeDtypeStruct((n_tok, D_A), F32),
        scratch_shapes=[pltpu.VMEM((seq + 2 * SUBLANES, LANES), F32)]
        + [pltpu.VMEM((seq, LANES), F32) for _ in range(10)]
        + [pltpu.VMEM((gate_rows, 4 * LANES), F32) for _ in range(2)],
        compiler_params=_cparams(("parallel", "parallel")),
        name="rglru",
    )(pa, pa, cw, cb, wg, bias, lam)


_ATTN_STRAIGHT = (0, 2, 3, 5)
_ATTN_ROLLED = (1, 4)
_ONES_ROWS = 16


def _attn_kernel(sink_ref, bias_ref, q_ref, k_ref, vt_ref, o_ref, s_scr, p_scr, e_scr, *, seq):
    band = 3 * BLOCK
    nblk = seq // BLOCK
    lo_q = lax.broadcasted_iota(jnp.int32, (BLOCK, 2 * HEAD_DIM), 1) < HEAD_DIM
    ones = jnp.ones((_ONES_ROWS, band), BF16)
    nt = (((1,), (1,)), ((), ()))
    stack = _ATTN_STRAIGHT + _ATTN_ROLLED
    ns = len(_ATTN_STRAIGHT) * BLOCK

    def window(j):
        q0 = j * BLOCK
        return q0, pl.multiple_of(jnp.clip(q0 - BLOCK, 0, seq - band), BLOCK)

    def scores(j, slot):
        q0, k0 = window(j)
        rows = pl.ds(pl.multiple_of(q0, BLOCK), BLOCK)
        kb = k_ref[pl.ds(k0, band), :]
        kbs = pltpu.roll(kb, HEAD_DIM, axis=1)

        def own_half(h):
            qt = q_ref[rows, (h // 2) * 2 * HEAD_DIM:(h // 2 + 1) * 2 * HEAD_DIM]
            return jnp.where(lo_q, qt, 0.0) if h % 2 == 0 else jnp.where(lo_q, 0.0, qt)

        qa = jnp.concatenate([own_half(h) for h in _ATTN_STRAIGHT], axis=0)
        qb = jnp.concatenate([own_half(h) for h in _ATTN_ROLLED], axis=0)
        s_scr[slot, :, :ns] = lax.dot_general(kb, qa, nt, preferred_element_type=F32)
        s_scr[slot, :, ns:] = lax.dot_general(kbs, qb, nt, preferred_element_type=F32)

    def softmax(j, slot):
        q0, k0 = window(j)
        bias = bias_ref[(q0 - k0) // BLOCK]
        for h in range(N_Q_HEADS):
            src = stack.index(h) * BLOCK
            s = s_scr[slot, :, src:src + BLOCK] + bias
            sk = sink_ref[h] * LOG2E
            m = jnp.maximum(jnp.max(s, axis=0, keepdims=True), sk)
            p_scr[slot, :, h * BLOCK:(h + 1) * BLOCK] = jnp.exp2(s - m).astype(BF16)
            e_scr[slot, :, h * BLOCK:(h + 1) * BLOCK] = jnp.broadcast_to(
                jnp.exp2(sk - m), (SUBLANES, BLOCK))

    def values(j, slot):
        q0, k0 = window(j)
        vt = vt_ref[:, pl.ds(k0, band)]
        outs = []
        for kv in range(N_KV_HEADS):
            cols = slice(kv * GROUP * BLOCK, (kv + 1) * GROUP * BLOCK)
            lhs = jnp.concatenate([vt[kv * HEAD_DIM:(kv + 1) * HEAD_DIM], ones], axis=0)
            ov = jnp.dot(lhs, p_scr[slot, :, cols], preferred_element_type=F32)
            res = ov[:HEAD_DIM] / (ov[HEAD_DIM:HEAD_DIM + 1] + e_scr[slot, 0:1, cols])
            outs += [res[:, g * BLOCK:(g + 1) * BLOCK] for g in range(GROUP)]
        rows = pl.ds(pl.multiple_of(q0, BLOCK), BLOCK)
        o_ref[rows, :] = jnp.concatenate(outs, axis=0).T

    scores(0, 0)
    softmax(0, 0)
    scores(1, 1)

    def step(jj, carry):
        j = 2 * jj
        values(j, 0)
        softmax(j + 1, 1)
        scores(j + 2, 0)
        values(j + 1, 1)
        softmax(j + 2, 0)
        scores(j + 3, 1)
        return carry

    lax.fori_loop(0, (nblk - 2) // 2, step, 0)
    values(nblk - 2, 0)
    softmax(nblk - 1, 1)
    values(nblk - 1, 1)


def _attn_bias_table():
    ki = np.arange(3 * BLOCK)[None, :, None]
    qi = np.arange(BLOCK)[None, None, :]
    off = (np.arange(3) * BLOCK)[:, None, None]
    return jnp.asarray(np.where(np.abs(off + qi - ki) <= WINDOW, 0.0, NEG).astype(np.float32))


def _attention(pq, vt, sink, bias, batch, seq):
    n_tok = pq.shape[0]
    band = 3 * BLOCK
    kern = functools.partial(_attn_kernel, seq=seq)
    return pl.pallas_call(
        kern,
        grid=(batch,),
        in_specs=[
            pl.BlockSpec(memory_space=pltpu.SMEM),
            pl.BlockSpec(bias.shape, lambda b: (0, 0, 0)),
            pl.BlockSpec((seq, D_B), lambda b: (b, 0)),
            pl.BlockSpec((seq, D_KV), lambda b: (b, D_B // D_KV)),
            pl.BlockSpec((D_KV, seq), lambda b: (0, b)),
        ],
        out_specs=pl.BlockSpec((seq, D_B), lambda b: (b, 0)),
        out_shape=jax.ShapeDtypeStruct((n_tok, D_B), F32),
        scratch_shapes=[pltpu.VMEM((2, band, N_Q_HEADS * BLOCK), F32),
                        pltpu.VMEM((2, band, N_Q_HEADS * BLOCK), BF16),
                        pltpu.VMEM((2, SUBLANES, N_Q_HEADS * BLOCK), F32)],
        compiler_params=_cparams(("parallel",)),
        name="win_attn",
    )(sink, bias, pq, pq, vt)


def _hy_filter_kernel(z_ref, w1_ref, b1_ref, fr_ref, w2_ref, b2_ref, w3_ref, dec_ref, o_ref):
    hi = lax.Precision.HIGHEST
    fr = fr_ref[...]
    h = jnp.sin(fr * (jnp.dot(z_ref[...], w1_ref[...], preferred_element_type=F32, precision=hi)
                      + b1_ref[...]))
    h = jnp.sin(fr * (jnp.dot(h, w2_ref[...], preferred_element_type=F32, precision=hi) + b2_ref[...]))
    f = jnp.dot(h, w3_ref[...], preferred_element_type=F32, precision=hi) * dec_ref[...]
    for half in range(2):
        for d in range(2):
            c0 = (2 * half + d) * D_C
            o_ref[d, half] = f[:, c0:c0 + D_C]


def _hy_filter(zemb2, w1, b1, fr, w2, b2, w3, dec4, rows=512):
    n = zemb2.shape[0]
    full = lambda a: pl.BlockSpec(a.shape, lambda i: (0,) * a.ndim)
    return pl.pallas_call(
        _hy_filter_kernel,
        grid=(n // rows,),
        in_specs=[pl.BlockSpec((rows, zemb2.shape[1]), lambda i: (i, 0)),
                  full(w1), full(b1), full(fr), full(w2), full(b2), full(w3),
                  pl.BlockSpec((rows, 4 * D_C), lambda i: (i, 0))],
        out_specs=pl.BlockSpec((2, 2, rows, D_C), lambda i: (0, 0, i, 0)),
        out_shape=jax.ShapeDtypeStruct((2, 2, n, D_C), F32),
        compiler_params=_cparams(("parallel",)),
        name="hyena_filter",
    )(zemb2, w1, b1, fr, w2, b2, w3, dec4)


def _dft_outer_fwd(zs, ka_ref, o_ref):
    nj, _, lanes = zs.shape
    pair = BF16_ROWS // DFT_SUB
    for p in range(DFT_BLK // BF16_ROWS):
        parts = []
        for s in range(pair * p, pair * (p + 1)):
            xg = zs[:, s * DFT_SUB:(s + 1) * DFT_SUB, :].reshape(nj * DFT_SUB, lanes).astype(BF16)
            c = jnp.dot(ka_ref[...], xg, preferred_element_type=F32)
            parts.append(c.reshape(2, DFT_NBLK, DFT_SUB, lanes))
        o_ref[0, :, :, p * BF16_ROWS:(p + 1) * BF16_ROWS, :] = (
            jnp.concatenate(parts, axis=2).astype(BF16))


def _hy_fwd_data_kernel(x1_ref, v_ref, cw_ref, cb_ref, ka_ref, o_ref, zs, *, seq):
    nj = seq // DFT_BLK
    cw = cw_ref[...]
    cb = cb_ref[...]
    for j in range(nj):
        x1 = _dwconv_block(x1_ref, j, nj, cw[:, D_C:2 * D_C], cb[:, D_C:2 * D_C], 1)
        v = _dwconv_block(v_ref, j, nj, cw[:, 2 * D_C:], cb[:, 2 * D_C:], 1)
        zs[j] = v * x1
    _dft_outer_fwd(zs, ka_ref, o_ref)


def _hy_fwd_filt_kernel(f_ref, ka_ref, o_ref):
    _dft_outer_fwd(f_ref.at[0], ka_ref, o_ref)


def _hy_fwd_data(pc, cw, cb, ka, batch, seq):
    nj = seq // DFT_BLK
    kern = functools.partial(_hy_fwd_data_kernel, seq=seq)
    return pl.pallas_call(
        kern,
        grid=(batch,),
        in_specs=[
            pl.BlockSpec((seq, D_C), lambda b: (b, 1)),
            pl.BlockSpec((seq, D_C), lambda b: (b, 2)),
            pl.BlockSpec(cw.shape, lambda b: (0, 0)),
            pl.BlockSpec(cb.shape, lambda b: (0, 0)),
            pl.BlockSpec(ka.shape, lambda b: (0, 0)),
        ],
        out_specs=pl.BlockSpec((1, 2, DFT_NBLK, DFT_BLK, D_C), lambda b: (b, 0, 0, 0, 0)),
        out_shape=jax.ShapeDtypeStruct((batch, 2, DFT_NBLK, DFT_BLK, D_C), BF16),
        scratch_shapes=[pltpu.VMEM((nj, DFT_BLK, D_C), F32)],
        compiler_params=_cparams(("parallel",)),
        name="hyena_dft_outer",
    )(pc, pc, cw, cb, ka)


def _hy_fwd_filt(filt4, ka):
    ndir, nj = filt4.shape[0], filt4.shape[1]
    return pl.pallas_call(
        _hy_fwd_filt_kernel,
        grid=(ndir,),
        in_specs=[
            pl.BlockSpec((1, nj, DFT_BLK, D_C), lambda b: (b, 0, 0, 0)),
            pl.BlockSpec(ka.shape, lambda b: (0, 0)),
        ],
        out_specs=pl.BlockSpec((1, 2, DFT_NBLK, DFT_BLK, D_C), lambda b: (b, 0, 0, 0, 0)),
        out_shape=jax.ShapeDtypeStruct((ndir, 2, DFT_NBLK, DFT_BLK, D_C), BF16),
        compiler_params=_cparams(("parallel",)),
        name="hyena_dft_outer_filter",
    )(filt4, ka)


def _hy_inner_kernel(c_ref, f_ref, g_ref, gi_ref, o_ref, *, batch, kper):
    for q in range(kper):
        g = g_ref[q]
        gi = gi_ref[q]
        hf = jnp.dot(g, f_ref[0, :, q].reshape(2 * DFT_BLK, D_C), preferred_element_type=F32)
        hb = jnp.dot(g, f_ref[1, :, q].reshape(2 * DFT_BLK, D_C), preferred_element_type=F32)
        hre = hf[:DFT_BLK] + hb[:DFT_BLK]
        him = hf[DFT_BLK:] - hb[DFT_BLK:]
        for b in range(batch):
            x = jnp.dot(g, c_ref[b, :, q].reshape(2 * DFT_BLK, D_C), preferred_element_type=F32)
            xre, xim = x[:DFT_BLK], x[DFT_BLK:]
            y = jnp.concatenate([xre * hre - xim * him, xre * him + xim * hre], axis=0)
            d = jnp.dot(gi, y.astype(BF16), preferred_element_type=F32)
            o_ref[b, :, q] = d.reshape(2, DFT_BLK, D_C).astype(BF16)


def _hy_inner(cdata, cfilt, g, gi, kper=4):
    batch = cdata.shape[0]
    kern = functools.partial(_hy_inner_kernel, batch=batch, kper=kper)
    blk = lambda nb: pl.BlockSpec((nb, 2, kper, DFT_BLK, D_C), lambda k: (0, 0, k, 0, 0))
    return pl.pallas_call(
        kern,
        grid=(DFT_NBLK // kper,),
        in_specs=[blk(batch), blk(cfilt.shape[0]),
                  pl.BlockSpec((kper, 2 * DFT_BLK, 2 * DFT_BLK), lambda k: (k, 0, 0)),
                  pl.BlockSpec((kper, 2 * DFT_BLK, 2 * DFT_BLK), lambda k: (k, 0, 0))],
        out_specs=blk(batch),
        out_shape=jax.ShapeDtypeStruct(cdata.shape, BF16),
        compiler_params=_cparams(("parallel",)),
        name="hyena_dft_inner",
    )(cdata, cfilt, g, gi)


def _hy_out_kernel(d_ref, x0_ref, x1_ref, v_ref, cw_ref, cb_ref, hb_ref, kai_ref, o_ref, ys, *, seq):
    nj = seq // DFT_BLK
    pair = BF16_ROWS // DFT_SUB
    for p in range(DFT_BLK // BF16_ROWS):
        d = d_ref[0, :, :, p * BF16_ROWS:(p + 1) * BF16_ROWS, :].astype(F32)
        for s in range(pair):
            rhs = d[:, :, s * DFT_SUB:(s + 1) * DFT_SUB, :].reshape(2 * DFT_NBLK * DFT_SUB, D_C)
            y = jnp.dot(kai_ref[...], rhs.astype(BF16), preferred_element_type=F32)
            r0 = p * BF16_ROWS + s * DFT_SUB
            ys[:, r0:r0 + DFT_SUB, :] = y.reshape(nj, DFT_SUB, D_C)
    cw = cw_ref[...]
    cb = cb_ref[...]
    hb = hb_ref[...]
    for j in range(nj):
        x0 = _dwconv_block(x0_ref, j, nj, cw[:, :D_C], cb[:, :D_C], 1)
        x1 = _dwconv_block(x1_ref, j, nj, cw[:, D_C:2 * D_C], cb[:, D_C:2 * D_C], 1)
        v = _dwconv_block(v_ref, j, nj, cw[:, 2 * D_C:], cb[:, 2 * D_C:], 1)
        z = v * x1
        o_ref[j * DFT_BLK:(j + 1) * DFT_BLK, :] = (ys[j] + z * hb) * x0


def _hy_out(dd, pc, cw, cb, hb, kai, batch, seq):
    n_tok = pc.shape[0]
    nj = seq // DFT_BLK
    kern = functools.partial(_hy_out_kernel, seq=seq)
    whole = lambda a: pl.BlockSpec(a.shape, lambda b: (0, 0))
    return pl.pallas_call(
        kern,
        grid=(batch,),
        in_specs=[
            pl.BlockSpec((1, 2, DFT_NBLK, DFT_BLK, D_C), lambda b: (b, 0, 0, 0, 0)),
            pl.BlockSpec((seq, D_C), lambda b: (b, 0)),
            pl.BlockSpec((seq, D_C), lambda b: (b, 1)),
            pl.BlockSpec((seq, D_C), lambda b: (b, 2)),
            whole(cw), whole(cb), whole(hb), whole(kai),
        ],
        out_specs=pl.BlockSpec((seq, D_C), lambda b: (b, 0)),
        out_shape=jax.ShapeDtypeStruct((n_tok, D_C), F32),
        scratch_shapes=[pltpu.VMEM((nj, DFT_BLK, D_C), F32)],
        compiler_params=_cparams(("parallel",)),
        name="hyena_out",
    )(dd, pc, pc, pc, cw, cb, hb, kai)


def _mix_mlp_kernel(ya_ref, yb_ref, yc_ref, x_ref, ga_ref, gb_ref, gc_ref, wo_ref, g_ref, wu_ref,
                    wd_ref, gf_ref, o_ref, *, ff_chunk, final_norm):
    y = jnp.concatenate([_rms(ya_ref[...], ga_ref[...]), _rms(yb_ref[...], gb_ref[...]),
                         _rms(yc_ref[...], gc_ref[...])], axis=-1).astype(BF16)
    x = x_ref[...] + jnp.dot(y, wo_ref[...], preferred_element_type=F32)
    h = _rms(x, g_ref[...]).astype(BF16)
    acc = x
    for c in range(D_FF // ff_chunk):
        sl = slice(c * ff_chunk, (c + 1) * ff_chunk)
        u = jnp.maximum(jnp.dot(h, wu_ref[:, sl], preferred_element_type=F32), 0.0)
        acc = acc + jnp.dot((u * u).astype(BF16), wd_ref[sl, :], preferred_element_type=F32)
    if final_norm:
        acc = _rms(acc, gf_ref[...])
    o_ref[...] = acc


def _mix_mlp(ya, yb, yc, x2, ga, gb, gc, wo_bf, g, wu_bf, wd_bf, gf, tm, final_norm, ff_chunk=1024):
    n_tok = x2.shape[0]
    kern = functools.partial(_mix_mlp_kernel, ff_chunk=ff_chunk, final_norm=final_norm)
    row = lambda w: pl.BlockSpec((tm, w), lambda i: (i, 0))
    vec = lambda w: pl.BlockSpec((1, w), lambda i: (0, 0))
    resident = lambda r, c: pl.BlockSpec((r, c), lambda i: (0, 0), pipeline_mode=pl.Buffered(1))
    return pl.pallas_call(
        kern,
        grid=(n_tok // tm,),
        in_specs=[row(D_A), row(D_B), row(D_C), row(D_MODEL), vec(D_A), vec(D_B), vec(D_C),
                  resident(D_MODEL, D_MODEL), vec(D_MODEL), resident(D_MODEL, D_FF),
                  resident(D_FF, D_MODEL), vec(D_MODEL)],
        out_specs=row(D_MODEL),
        out_shape=jax.ShapeDtypeStruct((n_tok, D_MODEL), F32),
        compiler_params=_cparams(("parallel",)),
        name="mix_mlp",
    )(ya, yb, yc, x2, ga, gb, gc, wo_bf, g, wu_bf, wd_bf, gf)


def _rope_tables(seq):
    pos = np.arange(seq, dtype=np.float32)
    inv_freq = (np.float32(ROPE_THETA) ** (-np.arange(0, ROT_DIM, 2, dtype=np.float32) / ROT_DIM))
    ang = (pos[:, None] * inv_freq[None, :]).astype(np.float32)
    cos, sin = np.cos(ang).astype(np.float32), np.sin(ang).astype(np.float32)
    rest = HEAD_DIM - ROT_DIM
    c_head = np.concatenate([cos, cos, np.ones((seq, rest), np.float32)], axis=1)
    s_head = np.concatenate([-sin, sin, np.zeros((seq, rest), np.float32)], axis=1)
    scale = np.float32(HEAD_DIM ** -0.5 * LOG2E)
    c = np.concatenate([np.tile(c_head, (1, N_Q_HEADS)) * scale, np.tile(c_head, (1, N_KV_HEADS))], axis=1)
    s = np.concatenate([np.tile(s_head, (1, N_Q_HEADS)) * scale, np.tile(s_head, (1, N_KV_HEADS))], axis=1)
    return jnp.asarray(c), jnp.asarray(s)


def _hyena_position_tables(seq):
    t = np.linspace(0.0, 1.0, seq, dtype=np.float32)[:, None]
    w = (2.0 * math.pi * np.arange(seq, dtype=np.float32)[:, None] / seq).astype(np.float32)
    f = np.linspace(1e-4, HY_BANDS - 1, HY_BANDS, dtype=np.float32)[None, :]
    fw = (f * w).astype(np.float32)
    z = np.concatenate([t, np.cos(fw), -np.sin(fw)], axis=-1).astype(np.float32)
    deltas = np.abs(np.linspace(HY_MIN_DECAY, HY_MAX_DECAY, D_C, dtype=np.float32))
    decay = np.exp(-t * deltas[None, :]).astype(np.float32)
    decay_b = decay.copy()
    decay_b[0] = 0.0
    half = seq // 2
    zp = np.zeros((half, 2, HY_WIDTH), np.float32)
    zp[:, 0, :HY_EMB] = z[:half]
    zp[:, 1, :HY_EMB] = z[half:]
    dec4 = np.concatenate([decay[:half], decay_b[:half], decay[half:], decay_b[half:]], axis=1)
    return jnp.asarray(zp.reshape(half, 2 * HY_WIDTH)), jnp.asarray(dec4)


def _blockdiag2(a):
    z = jnp.zeros_like(a)
    return jnp.concatenate([jnp.concatenate([a, z], axis=1), jnp.concatenate([z, a], axis=1)], axis=0)


def _dft_tables(seq):
    n = 2 * seq
    nj = seq // DFT_BLK
    kk = np.arange(DFT_NBLK)
    ang = 2.0 * np.pi * np.outer(kk, np.arange(nj)) / DFT_NBLK
    eye = np.eye(DFT_SUB)
    ka = np.concatenate([np.kron(np.cos(ang), eye), np.kron(-np.sin(ang), eye)], axis=0)
    kai = np.concatenate([np.kron(np.cos(ang).T, eye), np.kron(-np.sin(ang).T, eye)], axis=1) / n
    m = np.arange(DFT_BLK)
    k = kk[:, None, None] + DFT_NBLK * np.arange(DFT_BLK)[None, :, None]
    ph = 2.0 * np.pi * ((k * m[None, None, :]) % n) / n
    gre, gim = np.cos(ph), -np.sin(ph)
    g = np.concatenate([np.concatenate([gre, -gim], axis=2), np.concatenate([gim, gre], axis=2)], axis=1)
    gi = np.transpose(g, (0, 2, 1))
    as_bf = lambda a: jnp.asarray(a.astype(np.float32)).astype(BF16)
    return as_bf(ka), as_bf(kai), as_bf(g), as_bf(gi)


def _lru_blockdiag(w):
    nb = w.shape[1] // 2
    w = w.reshape(2, nb, 2, w.shape[2], w.shape[3])
    z = jnp.zeros_like(w[:, :, 0])
    top = jnp.concatenate([w[:, :, 0], z], axis=-1)
    bot = jnp.concatenate([z, w[:, :, 1]], axis=-1)
    return jnp.concatenate([top, bot], axis=-2)


def kernel(x, norm_mix_g, w_in, conv_a_w, conv_a_b, lru_wa, lru_ba, lru_wx, lru_bx, lru_lambda,
           attn_sink, hy_conv_w, hy_conv_b, hy_w1, hy_b1, hy_freq, hy_w2, hy_b2, hy_w3, hy_bias,
           gnorm_a, gnorm_b, gnorm_c, w_out, norm_mlp_g, w_up, w_down, final_norm_g):
    batch, seq, _ = x.shape
    depth = w_in.shape[0]
    n_tok = batch * seq
    tm = 1024
    tm_mlp = 512
    ng = D_A // LANES

    rc, rs = _rope_tables(seq)
    attn_bias = _attn_bias_table()
    zemb, dec4 = _hyena_position_tables(seq)
    ka, kai, gtab, gitab = _dft_tables(seq)

    xs = x.reshape(n_tok, D_MODEL)
    for i in range(depth):
        pa, pq, vt, pc = _in_proj(xs, norm_mix_g[i][None], w_in[i].astype(BF16), rc, rs, seq, tm)

        wa, wx = _lru_blockdiag(lru_wa[i]), _lru_blockdiag(lru_wx[i])
        wg = jnp.concatenate([wa[0], wx[0], wa[1], wx[1]], axis=-1).astype(BF16)
        tile = lambda v: v.reshape(ng, 1, LANES)
        bias = jnp.concatenate([tile(lru_ba[i][0]), tile(lru_bx[i][0]),
                                tile(lru_ba[i][1]), tile(lru_bx[i][1])], axis=-1)
        lam = jnp.concatenate([tile(lru_lambda[i][0]), tile(lru_lambda[i][1])], axis=-1)
        y_a = _lru(pa, conv_a_w[i], conv_a_b[i][None], wg, bias, lam, batch, seq)

        y_b = _attention(pq, vt, attn_sink[i], attn_bias, batch, seq)

        w1p = jnp.zeros((HY_WIDTH, HY_WIDTH), F32).at[:HY_EMB].set(hy_w1[i])
        pair = lambda v: jnp.concatenate([v, v])[None]
        filt = _hy_filter(zemb, _blockdiag2(w1p), pair(hy_b1[i]), pair(hy_freq[i]),
                          _blockdiag2(hy_w2[i]), pair(hy_b2[i]), _blockdiag2(hy_w3[i]), dec4)
        cw, cb = hy_conv_w[i], hy_conv_b[i][None]
        cdata = _hy_fwd_data(pc, cw, cb, ka, batch, seq)
        cfilt = _hy_fwd_filt(filt.reshape(2, seq // DFT_BLK, DFT_BLK, D_C), ka)
        dd = _hy_inner(cdata, cfilt, gtab, gitab)
        y_c = _hy_out(dd, pc, cw, cb, hy_bias[i][None], kai, batch, seq)

        xs = _mix_mlp(y_a, y_b, y_c, xs, gnorm_a[i][None], gnorm_b[i][None], gnorm_c[i][None],
                      w_out[i].astype(BF16), norm_mlp_g[i][None], w_up[i].astype(BF16),
                      w_down[i].astype(BF16), final_norm_g[None], tm, final_norm=(i == depth - 1))
    return xs.reshape(batch, seq, D_MODEL)
```

```python
import functools
import math

import numpy as np
import jax
import jax.numpy as jnp
from jax import lax
from jax.experimental import pallas as pl
from jax.experimental.pallas import tpu as pltpu

F32 = jnp.float32
BF16 = jnp.bfloat16

D_MODEL = 1024
D_A = 384
D_B = 384
D_C = 256
HEAD_DIM = 64
N_Q_HEADS = 6
N_KV_HEADS = 2
GROUP = N_Q_HEADS // N_KV_HEADS
D_KV = N_KV_HEADS * HEAD_DIM
D_QKV = D_B + 2 * D_KV
D_IN = 2 * D_A + D_QKV + 3 * D_C
C_LRU = 8.0
WINDOW = 128
BLOCK = 128
ROPE_THETA = 500000.0
ROT_DIM = HEAD_DIM // 4
HY_EMB = 33
HY_BANDS = (HY_EMB - 1) // 2
HY_WIDTH = 64
HY_TARGET = 1e-2
HY_MAX_DECAY = math.log(HY_TARGET) / 0.3
HY_MIN_DECAY = math.log(HY_TARGET) / 1.5
D_FF = 4 * D_MODEL
EPS = 1e-6
NEG = -1e30
LOG2E = math.log2(math.e)

LANES = 128
SUBLANES = 8
VMEM_LIMIT = 56 * 1024 * 1024

SCAN_SEGS = SUBLANES
DFT_BLK = 128
DFT_NBLK = 64
DFT_SUB = 8
BF16_ROWS = 16
IN_PROJ_CHUNKS = 2


def _cparams(sem):
    return pltpu.CompilerParams(dimension_semantics=sem, vmem_limit_bytes=VMEM_LIMIT)


def _rms(x, g):
    return x * lax.rsqrt(jnp.mean(x * x, axis=-1, keepdims=True) + EPS) * g


def _in_proj_kernel(x_ref, g_ref, w_ref, rc_ref, rs_ref, oa_ref, oq_ref, ov_ref, oc_ref):
    n = D_B + D_KV
    half = ROT_DIM // 2
    tm = x_ref.shape[0]
    for c in range(IN_PROJ_CHUNKS):
        r = slice(c * tm // IN_PROJ_CHUNKS, (c + 1) * tm // IN_PROJ_CHUNKS)
        h = _rms(x_ref[r, :], g_ref[...]).astype(BF16)
        oa_ref[r, :] = jnp.dot(h, w_ref[:, :2 * D_A], preferred_element_type=F32).astype(BF16)
        oc_ref[r, :] = jnp.dot(h, w_ref[:, 2 * D_A + D_QKV:], preferred_element_type=F32).astype(BF16)
        qkv = jnp.dot(h, w_ref[:, 2 * D_A:2 * D_A + D_QKV], preferred_element_type=F32)
        qk = qkv[:, :n]
        lane = lax.broadcasted_iota(jnp.int32, qk.shape, 1) % HEAD_DIM
        swapped = jnp.where(lane < half, pltpu.roll(qk, n - half, axis=1), pltpu.roll(qk, half, axis=1))
        oq_ref[r, :] = (qk * rc_ref[r, :] + swapped * rs_ref[r, :]).astype(BF16)
        ov_ref[:, r] = qkv[:, n:].T.astype(BF16)


def _in_proj(x2, g, w_bf, layer, rc, rs, seq, tm):
    n_tok = x2.shape[0]
    nrb = seq // tm
    return pl.pallas_call(
        _in_proj_kernel,
        grid=(n_tok // tm,),
        in_specs=[
            pl.BlockSpec((tm, D_MODEL), lambda i: (i, 0)),
            pl.BlockSpec((1, D_MODEL), lambda i: (0, 0)),
            pl.BlockSpec((None, D_MODEL, D_IN), lambda i: (layer, 0, 0)),
            pl.BlockSpec((tm, D_B + D_KV), lambda i: (i % nrb, 0)),
            pl.BlockSpec((tm, D_B + D_KV), lambda i: (i % nrb, 0)),
        ],
        out_specs=[
            pl.BlockSpec((tm, 2 * D_A), lambda i: (i, 0)),
            pl.BlockSpec((tm, D_B + D_KV), lambda i: (i, 0)),
            pl.BlockSpec((D_KV, tm), lambda i: (0, i)),
            pl.BlockSpec((tm, 3 * D_C), lambda i: (i, 0)),
        ],
        out_shape=[
            jax.ShapeDtypeStruct((n_tok, 2 * D_A), BF16),
            jax.ShapeDtypeStruct((n_tok, D_B + D_KV), BF16),
            jax.ShapeDtypeStruct((D_KV, n_tok), BF16),
            jax.ShapeDtypeStruct((n_tok, 3 * D_C), BF16),
        ],
        compiler_params=_cparams(("parallel",)),
        name="in_proj",
    )(x2, g, w_bf, rc, rs)


def _dwconv_chunk(src_ref, ci, nchunk, rows, w, b, pad_left):
    halo = BF16_ROWS
    r0 = pl.multiple_of(ci * rows, rows)
    seq = nchunk * rows
    top = src_ref[pl.ds(pl.multiple_of(jnp.maximum(r0 - halo, 0), halo), halo), :]
    bot = src_ref[pl.ds(pl.multiple_of(jnp.minimum(r0 + rows, seq - halo), halo), halo), :]
    top = jnp.where(ci > 0, top, jnp.zeros_like(top))
    bot = jnp.where(ci < nchunk - 1, bot, jnp.zeros_like(bot))
    win = jnp.concatenate([top, src_ref[pl.ds(r0, rows), :], bot], axis=0).astype(F32)
    return _conv_taps(win, rows, w, b, pad_left, halo)


def _conv_taps(win, rows, w, b, pad_left, halo):
    total = rows + 2 * halo
    acc = None
    for k in range(w.shape[0]):
        shift = (pad_left - k) % total
        tap = win if shift == 0 else pltpu.roll(win, shift, axis=0)
        term = tap[halo:halo + rows] * w[k:k + 1, :]
        acc = term if acc is None else acc + term
    return acc + b


def _dwconv_block(src_ref, j, nblk, w, b, pad_left):
    r0 = j * DFT_BLK
    halo = 2 * SUBLANES
    zeros = jnp.zeros((halo, src_ref.shape[1]), src_ref.dtype)
    top = zeros if j == 0 else src_ref[r0 - halo:r0, :]
    bot = zeros if j == nblk - 1 else src_ref[r0 + DFT_BLK:r0 + DFT_BLK + halo, :]
    win = jnp.concatenate([top, src_ref[r0:r0 + DFT_BLK, :], bot], axis=0).astype(F32)
    return _conv_taps(win, DFT_BLK, w, b, pad_left, halo)


def _softplus(x):
    return jnp.maximum(x, 0.0) + jnp.log1p(jnp.exp(-jnp.abs(x)))


def _gelu_tanh(x):
    c = math.sqrt(2.0 / math.pi)
    return x * (0.5 * (1.0 + jnp.tanh(c * (x + 0.044715 * (x * x * x)))))


def _lru_kernel(u_ref, gate_ref, cw_ref, cb_ref, w_ref, bias_ref, lam_ref, o_ref,
                xci, gg, af, bf, ar, br, pfs, hfs, prs, hrs, g0, g1, *, seq, rows, unroll):
    seg = seq // SCAN_SEGS
    tiles = rows // SCAN_SEGS
    nchunk = seq // rows
    per_seg = seg // rows
    cw = cw_ref[...]
    cb = cb_ref[...]
    w = w_ref[0] * 0.5
    bias = bias_ref[0] * 0.5
    hnsp = (-0.5 * C_LRU) * _softplus(-lam_ref[0])

    def conv(ci, carry):
        r0 = pl.multiple_of(ci * rows, rows)
        dst = pl.ds((ci % per_seg) * (rows * SCAN_SEGS) + ci // per_seg, rows, stride=SCAN_SEGS)
        xci[dst, :] = _dwconv_chunk(u_ref, ci, nchunk, rows, cw, cb, 2)
        gg[dst, :] = _gelu_tanh(gate_ref[pl.ds(r0, rows), :].astype(F32))
        return carry

    lax.fori_loop(0, nchunk, conv, 0)

    grows = g0.shape[0]
    ngate = seq // grows

    def gate_matmul(ci, g_ref):
        r0 = pl.multiple_of(ci * grows, grows)
        g_ref[...] = jnp.dot(xci[pl.ds(r0, grows), :].astype(BF16), w, preferred_element_type=F32)

    def gate_math(ci, g_ref):
        r0 = pl.multiple_of(ci * grows, grows)
        xc = xci[pl.ds(r0, grows), :]
        t = jnp.tanh(g_ref[...] + bias)
        hxc = 0.5 * xc
        for d, (a_ref, b_ref) in enumerate(((af, bf), (ar, br))):
            tr = t[:, (2 * d) * LANES:(2 * d + 1) * LANES]
            ti = t[:, (2 * d + 1) * LANES:(2 * d + 2) * LANES]
            hn = hnsp[:, d * LANES:(d + 1) * LANES]
            log_a = tr * hn + hn
            a = jnp.exp(log_a)
            nem = (-1.0 - a * a) * jnp.tanh(log_a)
            a_ref[pl.ds(r0, grows), :] = a
            b_ref[pl.ds(r0, grows), :] = jnp.sqrt(nem) * (ti * hxc + hxc)

    def gates(k, carry):
        gate_matmul(2 * k + 1, g1)
        gate_math(2 * k, g0)
        gate_matmul(jnp.minimum(2 * k + 2, ngate - 1), g0)
        gate_math(2 * k + 1, g1)
        return carry

    gate_matmul(0, g0)
    lax.fori_loop(0, ngate // 2, gates, 0)

    def scan(it, carry):
        def two_steps(p, h, a_ref, b_ref, p_out, h_out, r0, r1):
            a0 = a_ref[pl.ds(r0, SCAN_SEGS), :]
            b0 = b_ref[pl.ds(r0, SCAN_SEGS), :]
            a1 = a_ref[pl.ds(r1, SCAN_SEGS), :]
            a01 = a1 * a0
            b01 = a1 * b0 + b_ref[pl.ds(r1, SCAN_SEGS), :]
            p_out[pl.ds(r0, SCAN_SEGS), :] = a0 * p
            h_out[pl.ds(r0, SCAN_SEGS), :] = a0 * h + b0
            p = a01 * p
            h = a01 * h + b01
            p_out[pl.ds(r1, SCAN_SEGS), :] = p
            h_out[pl.ds(r1, SCAN_SEGS), :] = h
            return p, h

        pf, hf, pr, hr = carry
        for u in range(0, unroll, 2):
            i = it * unroll + u
            row = lambda pos: pl.multiple_of(pos * SCAN_SEGS, SCAN_SEGS)
            pf, hf = two_steps(pf, hf, af, bf, pfs, hfs, row(i), row(i + 1))
            pr, hr = two_steps(pr, hr, ar, br, prs, hrs, row(seg - 1 - i), row(seg - 2 - i))
        return pf, hf, pr, hr

    one = jnp.ones((SCAN_SEGS, LANES), F32)
    zero = jnp.zeros((SCAN_SEGS, LANES), F32)
    pf, hf, pr, hr = lax.fori_loop(0, seg // unroll, scan, (one, zero, one, zero))

    sub = lax.broadcasted_iota(jnp.int32, (SCAN_SEGS, LANES), 0)
    cf = zero
    cr = zero
    for _ in range(SCAN_SEGS - 1):
        cf = jnp.where(sub == 0, 0.0, pltpu.roll(hf + pf * cf, 1, axis=0))
        cr = jnp.where(sub == SCAN_SEGS - 1, 0.0, pltpu.roll(hr + pr * cr, SCAN_SEGS - 1, axis=0))

    def combine(ci, carry):
        r0 = pl.multiple_of(ci * rows, rows)
        sl = pl.ds(r0, rows)
        tile3 = lambda ref: ref[sl, :].reshape(tiles, SCAN_SEGS, LANES)
        h = (tile3(hfs) + tile3(pfs) * cf[None]) + (tile3(hrs) + tile3(prs) * cr[None])
        out = h.reshape(rows, LANES) * gg[sl, :]
        for k in range(tiles):
            o_ref[pl.ds(ci * tiles + k, SCAN_SEGS, stride=seg), :] = (
                out[k * SCAN_SEGS:(k + 1) * SCAN_SEGS])
        return carry

    lax.fori_loop(0, nchunk, combine, 0)


def _lru(pa, cw, cb, wg, bias, lam, batch, seq, rows=128, gate_rows=512, unroll=16):
    n_tok = pa.shape[0]
    ng = D_A // LANES
    kern = functools.partial(_lru_kernel, seq=seq, rows=rows, unroll=unroll)
    return pl.pallas_call(
        kern,
        grid=(batch, ng),
        in_specs=[
            pl.BlockSpec((seq, LANES), lambda b, g: (b, g)),
            pl.BlockSpec((seq, LANES), lambda b, g: (b, ng + g)),
            pl.BlockSpec((cw.shape[0], LANES), lambda b, g: (0, g)),
            pl.BlockSpec((1, LANES), lambda b, g: (0, g)),
            pl.BlockSpec((1, LANES, 4 * LANES), lambda b, g: (g, 0, 0)),
            pl.BlockSpec((1, 1, 4 * LANES), lambda b, g: (g, 0, 0)),
            pl.BlockSpec((1, 1, 2 * LANES), lambda b, g: (g, 0, 0)),
        ],
        out_specs=pl.BlockSpec((seq, LANES), lambda b, g: (b, g)),
        out_shape=jax.ShapeDtypeStruct((n_tok, D_A), F32),
        scratch_shapes=[pltpu.VMEM((seq, LANES), F32) for _ in range(10)]
        + [pltpu.VMEM((gate_rows, 4 * LANES), F32) for _ in range(2)],
        compiler_params=_cparams(("parallel", "parallel")),
        name="rglru",
    )(pa, pa, cw, cb, wg, bias, lam)


_ATTN_STRAIGHT = (0, 2, 3, 5)
_ATTN_ROLLED = (1, 4)
_ONES_ROWS = 16


def _attn_kernel(sink_ref, bias_ref, q_ref, k_ref, vt_ref, o_ref, s_scr, p_scr, e_scr, *, seq):
    band = 3 * BLOCK
    nblk = seq // BLOCK
    lo_q = lax.broadcasted_iota(jnp.int32, (BLOCK, 2 * HEAD_DIM), 1) < HEAD_DIM
    ones = jnp.ones((_ONES_ROWS, band), BF16)
    nt = (((1,), (1,)), ((), ()))
    stack = _ATTN_STRAIGHT + _ATTN_ROLLED
    ns = len(_ATTN_STRAIGHT) * BLOCK

    def window(j):
        q0 = j * BLOCK
        return q0, pl.multiple_of(jnp.clip(q0 - BLOCK, 0, seq - band), BLOCK)

    def scores(j, slot):
        q0, k0 = window(j)
        rows = pl.ds(pl.multiple_of(q0, BLOCK), BLOCK)
        kb = k_ref[pl.ds(k0, band), :]
        kbs = pltpu.roll(kb, HEAD_DIM, axis=1)

        def own_half(h):
            qt = q_ref[rows, (h // 2) * 2 * HEAD_DIM:(h // 2 + 1) * 2 * HEAD_DIM]
            return jnp.where(lo_q, qt, 0.0) if h % 2 == 0 else jnp.where(lo_q, 0.0, qt)

        qa = jnp.concatenate([own_half(h) for h in _ATTN_STRAIGHT], axis=0)
        qb = jnp.concatenate([own_half(h) for h in _ATTN_ROLLED], axis=0)
        s_scr[slot, :, :ns] = lax.dot_general(kb, qa, nt, preferred_element_type=F32)
        s_scr[slot, :, ns:] = lax.dot_general(kbs, qb, nt, preferred_element_type=F32)

    def softmax(j, slot):
        q0, k0 = window(j)
        bias = bias_ref[(q0 - k0) // BLOCK]
        for h in range(N_Q_HEADS):
            src = stack.index(h) * BLOCK
            s = s_scr[slot, :, src:src + BLOCK] + bias
            sk = sink_ref[h] * LOG2E
            m = jnp.maximum(jnp.max(s, axis=0, keepdims=True), sk)
            p_scr[slot, :, h * BLOCK:(h + 1) * BLOCK] = jnp.exp2(s - m).astype(BF16)
            e_scr[slot, :, h * BLOCK:(h + 1) * BLOCK] = jnp.broadcast_to(
                jnp.exp2(sk - m), (SUBLANES, BLOCK))

    def values(j, slot):
        q0, k0 = window(j)
        vt = vt_ref[:, pl.ds(k0, band)]
        outs = []
        for kv in range(N_KV_HEADS):
            cols = slice(kv * GROUP * BLOCK, (kv + 1) * GROUP * BLOCK)
            lhs = jnp.concatenate([vt[kv * HEAD_DIM:(kv + 1) * HEAD_DIM], ones], axis=0)
            ov = jnp.dot(lhs, p_scr[slot, :, cols], preferred_element_type=F32)
            res = ov[:HEAD_DIM] / (ov[HEAD_DIM:HEAD_DIM + 1] + e_scr[slot, 0:1, cols])
            outs += [res[:, g * BLOCK:(g + 1) * BLOCK] for g in range(GROUP)]
        rows = pl.ds(pl.multiple_of(q0, BLOCK), BLOCK)
        o_ref[rows, :] = jnp.concatenate(outs, axis=0).T

    scores(0, 0)
    softmax(0, 0)
    scores(1, 1)

    def step(jj, carry):
        j = 2 * jj
        values(j, 0)
        softmax(j + 1, 1)
        scores(j + 2, 0)
        values(j + 1, 1)
        softmax(j + 2, 0)
        scores(j + 3, 1)
        return carry

    lax.fori_loop(0, (nblk - 2) // 2, step, 0)
    values(nblk - 2, 0)
    softmax(nblk - 1, 1)
    values(nblk - 1, 1)


def _attn_bias_table():
    ki = np.arange(3 * BLOCK)[None, :, None]
    qi = np.arange(BLOCK)[None, None, :]
    off = (np.arange(3) * BLOCK)[:, None, None]
    return jnp.asarray(np.where(np.abs(off + qi - ki) <= WINDOW, 0.0, NEG).astype(np.float32))


def _attention(pq, vt, sink, bias, batch, seq):
    n_tok = pq.shape[0]
    band = 3 * BLOCK
    kern = functools.partial(_attn_kernel, seq=seq)
    return pl.pallas_call(
        kern,
        grid=(batch,),
        in_specs=[
            pl.BlockSpec(memory_space=pltpu.SMEM),
            pl.BlockSpec(bias.shape, lambda b: (0, 0, 0)),
            pl.BlockSpec((seq, D_B), lambda b: (b, 0)),
            pl.BlockSpec((seq, D_KV), lambda b: (b, D_B // D_KV)),
            pl.BlockSpec((D_KV, seq), lambda b: (0, b)),
        ],
        out_specs=pl.BlockSpec((seq, D_B), lambda b: (b, 0)),
        out_shape=jax.ShapeDtypeStruct((n_tok, D_B), F32),
        scratch_shapes=[pltpu.VMEM((2, band, N_Q_HEADS * BLOCK), F32),
                        pltpu.VMEM((2, band, N_Q_HEADS * BLOCK), BF16),
                        pltpu.VMEM((2, SUBLANES, N_Q_HEADS * BLOCK), F32)],
        compiler_params=_cparams(("parallel",)),
        name="win_attn",
    )(sink, bias, pq, pq, vt)


def _hy_filter_kernel(z_ref, w1_ref, b1_ref, fr_ref, w2_ref, b2_ref, w3_ref, dec_ref, o_ref):
    hi = lax.Precision.HIGHEST
    fr = fr_ref[...]
    h = jnp.sin(fr * (jnp.dot(z_ref[...], w1_ref[...], preferred_element_type=F32, precision=hi)
                      + b1_ref[...]))
    h = jnp.sin(fr * (jnp.dot(h, w2_ref[...], preferred_element_type=F32, precision=hi) + b2_ref[...]))
    f = jnp.dot(h, w3_ref[...], preferred_element_type=F32, precision=hi) * dec_ref[...]
    for half in range(2):
        for d in range(2):
            c0 = (2 * half + d) * D_C
            o_ref[d, half] = f[:, c0:c0 + D_C]


def _hy_filter(zemb2, w1, b1, fr, w2, b2, w3, dec4, rows=512):
    n = zemb2.shape[0]
    full = lambda a: pl.BlockSpec(a.shape, lambda i: (0,) * a.ndim)
    return pl.pallas_call(
        _hy_filter_kernel,
        grid=(n // rows,),
        in_specs=[pl.BlockSpec((rows, zemb2.shape[1]), lambda i: (i, 0)),
                  full(w1), full(b1), full(fr), full(w2), full(b2), full(w3),
                  pl.BlockSpec((rows, 4 * D_C), lambda i: (i, 0))],
        out_specs=pl.BlockSpec((2, 2, rows, D_C), lambda i: (0, 0, i, 0)),
        out_shape=jax.ShapeDtypeStruct((2, 2, n, D_C), F32),
        compiler_params=_cparams(("parallel",)),
        name="hyena_filter",
    )(zemb2, w1, b1, fr, w2, b2, w3, dec4)


def _dft_outer_fwd(zs, ka_ref, o_ref):
    nj, _, lanes = zs.shape
    pair = BF16_ROWS // DFT_SUB
    for p in range(DFT_BLK // BF16_ROWS):
        parts = []
        for s in range(pair * p, pair * (p + 1)):
            xg = zs[:, s * DFT_SUB:(s + 1) * DFT_SUB, :].reshape(nj * DFT_SUB, lanes).astype(BF16)
            c = jnp.dot(ka_ref[...], xg, preferred_element_type=F32)
            parts.append(c.reshape(2, DFT_NBLK, DFT_SUB, lanes))
        o_ref[0, :, :, p * BF16_ROWS:(p + 1) * BF16_ROWS, :] = (
            jnp.concatenate(parts, axis=2).astype(BF16))


def _hy_fwd_data_kernel(x1_ref, v_ref, cw_ref, cb_ref, ka_ref, o_ref, zs, *, seq):
    nj = seq // DFT_BLK
    cw = cw_ref[...]
    cb = cb_ref[...]
    for j in range(nj):
        x1 = _dwconv_block(x1_ref, j, nj, cw[:, D_C:2 * D_C], cb[:, D_C:2 * D_C], 1)
        v = _dwconv_block(v_ref, j, nj, cw[:, 2 * D_C:], cb[:, 2 * D_C:], 1)
        zs[j] = v * x1
    _dft_outer_fwd(zs, ka_ref, o_ref)


def _hy_fwd_filt_kernel(f_ref, ka_ref, o_ref):
    _dft_outer_fwd(f_ref.at[0], ka_ref, o_ref)


def _hy_fwd_data(pc, cw, cb, ka, batch, seq):
    nj = seq // DFT_BLK
    kern = functools.partial(_hy_fwd_data_kernel, seq=seq)
    return pl.pallas_call(
        kern,
        grid=(batch,),
        in_specs=[
            pl.BlockSpec((seq, D_C), lambda b: (b, 1)),
            pl.BlockSpec((seq, D_C), lambda b: (b, 2)),
            pl.BlockSpec(cw.shape, lambda b: (0, 0)),
            pl.BlockSpec(cb.shape, lambda b: (0, 0)),
            pl.BlockSpec(ka.shape, lambda b: (0, 0)),
        ],
        out_specs=pl.BlockSpec((1, 2, DFT_NBLK, DFT_BLK, D_C), lambda b: (b, 0, 0, 0, 0)),
        out_shape=jax.ShapeDtypeStruct((batch, 2, DFT_NBLK, DFT_BLK, D_C), BF16),
        scratch_shapes=[pltpu.VMEM((nj, DFT_BLK, D_C), F32)],
        compiler_params=_cparams(("parallel",)),
        name="hyena_dft_outer",
    )(pc, pc, cw, cb, ka)


def _hy_fwd_filt(filt4, ka):
    ndir, nj = filt4.shape[0], filt4.shape[1]
    return pl.pallas_call(
        _hy_fwd_filt_kernel,
        grid=(ndir,),
        in_specs=[
            pl.BlockSpec((1, nj, DFT_BLK, D_C), lambda b: (b, 0, 0, 0)),
            pl.BlockSpec(ka.shape, lambda b: (0, 0)),
        ],
        out_specs=pl.BlockSpec((1, 2, DFT_NBLK, DFT_BLK, D_C), lambda b: (b, 0, 0, 0, 0)),
        out_shape=jax.ShapeDtypeStruct((ndir, 2, DFT_NBLK, DFT_BLK, D_C), BF16),
        compiler_params=_cparams(("parallel",)),
        name="hyena_dft_outer_filter",
    )(filt4, ka)


def _hy_inner_kernel(c_ref, f_ref, g_ref, gi_ref, o_ref, *, batch, kper):
    for q in range(kper):
        g = g_ref[q]
        gi = gi_ref[q]
        hf = jnp.dot(g, f_ref[0, :, q].reshape(2 * DFT_BLK, D_C), preferred_element_type=F32)
        hb = jnp.dot(g, f_ref[1, :, q].reshape(2 * DFT_BLK, D_C), preferred_element_type=F32)
        hre = hf[:DFT_BLK] + hb[:DFT_BLK]
        him = hf[DFT_BLK:] - hb[DFT_BLK:]
        for b in range(batch):
            x = jnp.dot(g, c_ref[b, :, q].reshape(2 * DFT_BLK, D_C), preferred_element_type=F32)
            xre, xim = x[:DFT_BLK], x[DFT_BLK:]
            y = jnp.concatenate([xre * hre - xim * him, xre * him + xim * hre], axis=0)
            d = jnp.dot(gi, y.astype(BF16), preferred_element_type=F32)
            o_ref[b, :, q] = d.reshape(2, DFT_BLK, D_C).astype(BF16)


def _hy_inner(cdata, cfilt, g, gi, kper=4):
    batch = cdata.shape[0]
    kern = functools.partial(_hy_inner_kernel, batch=batch, kper=kper)
    blk = lambda nb: pl.BlockSpec((nb, 2, kper, DFT_BLK, D_C), lambda k: (0, 0, k, 0, 0))
    return pl.pallas_call(
        kern,
        grid=(DFT_NBLK // kper,),
        in_specs=[blk(batch), blk(cfilt.shape[0]),
                  pl.BlockSpec((kper, 2 * DFT_BLK, 2 * DFT_BLK), lambda k: (k, 0, 0)),
                  pl.BlockSpec((kper, 2 * DFT_BLK, 2 * DFT_BLK), lambda k: (k, 0, 0))],
        out_specs=blk(batch),
        out_shape=jax.ShapeDtypeStruct(cdata.shape, BF16),
        compiler_params=_cparams(("parallel",)),
        name="hyena_dft_inner",
    )(cdata, cfilt, g, gi)


def _hy_out_kernel(d_ref, x0_ref, x1_ref, v_ref, cw_ref, cb_ref, hb_ref, kai_ref, o_ref, ys, *, seq):
    nj = seq // DFT_BLK
    pair = BF16_ROWS // DFT_SUB
    for p in range(DFT_BLK // BF16_ROWS):
        d = d_ref[0, :, :, p * BF16_ROWS:(p + 1) * BF16_ROWS, :].astype(F32)
        for s in range(pair):
            rhs = d[:, :, s * DFT_SUB:(s + 1) * DFT_SUB, :].reshape(2 * DFT_NBLK * DFT_SUB, D_C)
            y = jnp.dot(kai_ref[...], rhs.astype(BF16), preferred_element_type=F32)
            r0 = p * BF16_ROWS + s * DFT_SUB
            ys[:, r0:r0 + DFT_SUB, :] = y.reshape(nj, DFT_SUB, D_C)
    cw = cw_ref[...]
    cb = cb_ref[...]
    hb = hb_ref[...]
    for j in range(nj):
        x0 = _dwconv_block(x0_ref, j, nj, cw[:, :D_C], cb[:, :D_C], 1)
        x1 = _dwconv_block(x1_ref, j, nj, cw[:, D_C:2 * D_C], cb[:, D_C:2 * D_C], 1)
        v = _dwconv_block(v_ref, j, nj, cw[:, 2 * D_C:], cb[:, 2 * D_C:], 1)
        z = v * x1
        o_ref[j * DFT_BLK:(j + 1) * DFT_BLK, :] = (ys[j] + z * hb) * x0


def _hy_out(dd, pc, cw, cb, hb, kai, batch, seq):
    n_tok = pc.shape[0]
    nj = seq // DFT_BLK
    kern = functools.partial(_hy_out_kernel, seq=seq)
    whole = lambda a: pl.BlockSpec(a.shape, lambda b: (0, 0))
    return pl.pallas_call(
        kern,
        grid=(batch,),
        in_specs=[
            pl.BlockSpec((1, 2, DFT_NBLK, DFT_BLK, D_C), lambda b: (b, 0, 0, 0, 0)),
            pl.BlockSpec((seq, D_C), lambda b: (b, 0)),
            pl.BlockSpec((seq, D_C), lambda b: (b, 1)),
            pl.BlockSpec((seq, D_C), lambda b: (b, 2)),
            whole(cw), whole(cb), whole(hb), whole(kai),
        ],
        out_specs=pl.BlockSpec((seq, D_C), lambda b: (b, 0)),
        out_shape=jax.ShapeDtypeStruct((n_tok, D_C), F32),
        scratch_shapes=[pltpu.VMEM((nj, DFT_BLK, D_C), F32)],
        compiler_params=_cparams(("parallel",)),
        name="hyena_out",
    )(dd, pc, pc, pc, cw, cb, hb, kai)


def _mix_mlp_kernel(ya_ref, yb_ref, yc_ref, x_ref, ga_ref, gb_ref, gc_ref, wo_ref, g_ref, wu_ref,
                    wd_ref, gf_ref, o_ref, *, ff_chunk, final_norm):
    y = jnp.concatenate([_rms(ya_ref[...], ga_ref[...]), _rms(yb_ref[...], gb_ref[...]),
                         _rms(yc_ref[...], gc_ref[...])], axis=-1).astype(BF16)
    x = x_ref[...] + jnp.dot(y, wo_ref[...], preferred_element_type=F32)
    h = _rms(x, g_ref[...]).astype(BF16)
    acc = x
    for c in range(D_FF // ff_chunk):
        sl = slice(c * ff_chunk, (c + 1) * ff_chunk)
        u = jnp.maximum(jnp.dot(h, wu_ref[:, sl], preferred_element_type=F32), 0.0)
        acc = acc + jnp.dot((u * u).astype(BF16), wd_ref[sl, :], preferred_element_type=F32)
    if final_norm:
        acc = _rms(acc, gf_ref[...])
    o_ref[...] = acc


def _mix_mlp(ya, yb, yc, x2, ga, gb, gc, wo_bf, g, wu_bf, wd_bf, gf, layer, tm, final_norm,
             ff_chunk=1024):
    n_tok = x2.shape[0]
    kern = functools.partial(_mix_mlp_kernel, ff_chunk=ff_chunk, final_norm=final_norm)
    row = lambda w: pl.BlockSpec((tm, w), lambda i: (i, 0))
    vec = lambda w: pl.BlockSpec((1, w), lambda i: (0, 0))
    resident = lambda r, c: pl.BlockSpec((None, r, c), lambda i: (layer, 0, 0),
                                         pipeline_mode=pl.Buffered(1))
    return pl.pallas_call(
        kern,
        grid=(n_tok // tm,),
        in_specs=[row(D_A), row(D_B), row(D_C), row(D_MODEL), vec(D_A), vec(D_B), vec(D_C),
                  resident(D_MODEL, D_MODEL), vec(D_MODEL), resident(D_MODEL, D_FF),
                  resident(D_FF, D_MODEL), vec(D_MODEL)],
        out_specs=row(D_MODEL),
        out_shape=jax.ShapeDtypeStruct((n_tok, D_MODEL), F32),
        compiler_params=_cparams(("parallel",)),
        name="mix_mlp",
    )(ya, yb, yc, x2, ga, gb, gc, wo_bf, g, wu_bf, wd_bf, gf)


def _rope_tables(seq):
    pos = np.arange(seq, dtype=np.float32)
    inv_freq = (np.float32(ROPE_THETA) ** (-np.arange(0, ROT_DIM, 2, dtype=np.float32) / ROT_DIM))
    ang = (pos[:, None] * inv_freq[None, :]).astype(np.float32)
    cos, sin = np.cos(ang).astype(np.float32), np.sin(ang).astype(np.float32)
    rest = HEAD_DIM - ROT_DIM
    c_head = np.concatenate([cos, cos, np.ones((seq, rest), np.float32)], axis=1)
    s_head = np.concatenate([-sin, sin, np.zeros((seq, rest), np.float32)], axis=1)
    scale = np.float32(HEAD_DIM ** -0.5 * LOG2E)
    c = np.concatenate([np.tile(c_head, (1, N_Q_HEADS)) * scale, np.tile(c_head, (1, N_KV_HEADS))], axis=1)
    s = np.concatenate([np.tile(s_head, (1, N_Q_HEADS)) * scale, np.tile(s_head, (1, N_KV_HEADS))], axis=1)
    return jnp.asarray(c), jnp.asarray(s)


def _hyena_position_tables(seq):
    t = np.linspace(0.0, 1.0, seq, dtype=np.float32)[:, None]
    w = (2.0 * math.pi * np.arange(seq, dtype=np.float32)[:, None] / seq).astype(np.float32)
    f = np.linspace(1e-4, HY_BANDS - 1, HY_BANDS, dtype=np.float32)[None, :]
    fw = (f * w).astype(np.float32)
    z = np.concatenate([t, np.cos(fw), -np.sin(fw)], axis=-1).astype(np.float32)
    deltas = np.abs(np.linspace(HY_MIN_DECAY, HY_MAX_DECAY, D_C, dtype=np.float32))
    decay = np.exp(-t * deltas[None, :]).astype(np.float32)
    decay_b = decay.copy()
    decay_b[0] = 0.0
    half = seq // 2
    zp = np.zeros((half, 2, HY_WIDTH), np.float32)
    zp[:, 0, :HY_EMB] = z[:half]
    zp[:, 1, :HY_EMB] = z[half:]
    dec4 = np.concatenate([decay[:half], decay_b[:half], decay[half:], decay_b[half:]], axis=1)
    return jnp.asarray(zp.reshape(half, 2 * HY_WIDTH)), jnp.asarray(dec4)


def _blockdiag2(a):
    z = jnp.zeros_like(a)
    return jnp.concatenate([jnp.concatenate([a, z], axis=1), jnp.concatenate([z, a], axis=1)], axis=0)


def _dft_tables(seq):
    n = 2 * seq
    nj = seq // DFT_BLK
    kk = np.arange(DFT_NBLK)
    ang = 2.0 * np.pi * np.outer(kk, np.arange(nj)) / DFT_NBLK
    eye = np.eye(DFT_SUB)
    ka = np.concatenate([np.kron(np.cos(ang), eye), np.kron(-np.sin(ang), eye)], axis=0)
    kai = np.concatenate([np.kron(np.cos(ang).T, eye), np.kron(-np.sin(ang).T, eye)], axis=1) / n
    m = np.arange(DFT_BLK)
    k = kk[:, None, None] + DFT_NBLK * np.arange(DFT_BLK)[None, :, None]
    ph = 2.0 * np.pi * ((k * m[None, None, :]) % n) / n
    gre, gim = np.cos(ph), -np.sin(ph)
    g = np.concatenate([np.concatenate([gre, -gim], axis=2), np.concatenate([gim, gre], axis=2)], axis=1)
    gi = np.transpose(g, (0, 2, 1))
    as_bf = lambda a: jnp.asarray(a.astype(np.float32)).astype(BF16)
    return as_bf(ka), as_bf(kai), as_bf(g), as_bf(gi)


def _lru_blockdiag(w):
    nb = w.shape[1] // 2
    w = w.reshape(2, nb, 2, w.shape[2], w.shape[3])
    z = jnp.zeros_like(w[:, :, 0])
    top = jnp.concatenate([w[:, :, 0], z], axis=-1)
    bot = jnp.concatenate([z, w[:, :, 1]], axis=-1)
    return jnp.concatenate([top, bot], axis=-2)


def kernel(x, norm_mix_g, w_in, conv_a_w, conv_a_b, lru_wa, lru_ba, lru_wx, lru_bx, lru_lambda,
           attn_sink, hy_conv_w, hy_conv_b, hy_w1, hy_b1, hy_freq, hy_w2, hy_b2, hy_w3, hy_bias,
           gnorm_a, gnorm_b, gnorm_c, w_out, norm_mlp_g, w_up, w_down, final_norm_g):
    batch, seq, _ = x.shape
    depth = w_in.shape[0]
    n_tok = batch * seq
    tm = 1024
    ng = D_A // LANES

    rc, rs = _rope_tables(seq)
    attn_bias = _attn_bias_table()
    zemb, dec4 = _hyena_position_tables(seq)
    ka, kai, gtab, gitab = _dft_tables(seq)

    w_in_bf, w_out_bf = w_in.astype(BF16), w_out.astype(BF16)
    w_up_bf, w_down_bf = w_up.astype(BF16), w_down.astype(BF16)

    xs = x.reshape(n_tok, D_MODEL)
    for i in range(depth):
        pa, pq, vt, pc = _in_proj(xs, norm_mix_g[i][None], w_in_bf, i, rc, rs, seq, tm)

        wa, wx = _lru_blockdiag(lru_wa[i]), _lru_blockdiag(lru_wx[i])
        wg = jnp.concatenate([wa[0], wx[0], wa[1], wx[1]], axis=-1).astype(BF16)
        tile = lambda v: v.reshape(ng, 1, LANES)
        bias = jnp.concatenate([tile(lru_ba[i][0]), tile(lru_bx[i][0]),
                                tile(lru_ba[i][1]), tile(lru_bx[i][1])], axis=-1)
        lam = jnp.concatenate([tile(lru_lambda[i][0]), tile(lru_lambda[i][1])], axis=-1)
        y_a = _lru(pa, conv_a_w[i], conv_a_b[i][None], wg, bias, lam, batch, seq)

        y_b = _attention(pq, vt, attn_sink[i], attn_bias, batch, seq)

        w1p = jnp.zeros((HY_WIDTH, HY_WIDTH), F32).at[:HY_EMB].set(hy_w1[i])
        pair = lambda v: jnp.concatenate([v, v])[None]
        filt = _hy_filter(zemb, _blockdiag2(w1p), pair(hy_b1[i]), pair(hy_freq[i]),
                          _blockdiag2(hy_w2[i]), pair(hy_b2[i]), _blockdiag2(hy_w3[i]), dec4)
        cw, cb = hy_conv_w[i], hy_conv_b[i][None]
        cdata = _hy_fwd_data(pc, cw, cb, ka, batch, seq)
        cfilt = _hy_fwd_filt(filt.reshape(2, seq // DFT_BLK, DFT_BLK, D_C), ka)
        dd = _hy_inner(cdata, cfilt, gtab, gitab)
        y_c = _hy_out(dd, pc, cw, cb, hy_bias[i][None], kai, batch, seq)

        xs = _mix_mlp(y_a, y_b, y_c, xs, gnorm_a[i][None], gnorm_b[i][None], gnorm_c[i][None],
                      w_out_bf, norm_mlp_g[i][None], w_up_bf, w_down_bf, final_norm_g[None], i, tm,
                      final_norm=(i == depth - 1))
    return xs.reshape(batch, seq, D_MODEL)
```

```python
import functools
import math

import numpy as np
import jax
import jax.numpy as jnp
from jax import lax
from jax.experimental import pallas as pl
from jax.experimental.pallas import tpu as pltpu

F32 = jnp.float32
BF16 = jnp.bfloat16

D_MODEL = 1024
D_A = 384
D_B = 384
D_C = 256
HEAD_DIM = 64
N_Q_HEADS = 6
N_KV_HEADS = 2
GROUP = N_Q_HEADS // N_KV_HEADS
D_KV = N_KV_HEADS * HEAD_DIM
D_QKV = D_B + 2 * D_KV
D_IN = 2 * D_A + D_QKV + 3 * D_C
C_LRU = 8.0
WINDOW = 128
BLOCK = 128
ROPE_THETA = 500000.0
ROT_DIM = HEAD_DIM // 4
HY_EMB = 33
HY_BANDS = (HY_EMB - 1) // 2
HY_WIDTH = 64
HY_TARGET = 1e-2
HY_MAX_DECAY = math.log(HY_TARGET) / 0.3
HY_MIN_DECAY = math.log(HY_TARGET) / 1.5
D_FF = 4 * D_MODEL
EPS = 1e-6
NEG = -1e30
LOG2E = math.log2(math.e)

LANES = 128
SUBLANES = 8
VMEM_LIMIT = 56 * 1024 * 1024

SCAN_SEGS = SUBLANES
DFT_BLK = 128
DFT_NBLK = 64
DFT_SUB = 8
BF16_ROWS = 16
IN_PROJ_CHUNKS = 2


def _cparams(sem):
    return pltpu.CompilerParams(dimension_semantics=sem, vmem_limit_bytes=VMEM_LIMIT)


def _rms(x, g):
    return x * lax.rsqrt(jnp.mean(x * x, axis=-1, keepdims=True) + EPS) * g


def _in_proj_kernel(x_ref, g_ref, w_ref, rc_ref, rs_ref, oa_ref, oq_ref, ov_ref, oc_ref):
    n = D_B + D_KV
    half = ROT_DIM // 2
    tm = x_ref.shape[0]
    for c in range(IN_PROJ_CHUNKS):
        r = slice(c * tm // IN_PROJ_CHUNKS, (c + 1) * tm // IN_PROJ_CHUNKS)
        h = _rms(x_ref[r, :], g_ref[...]).astype(BF16)
        oa_ref[r, :] = jnp.dot(h, w_ref[:, :2 * D_A], preferred_element_type=F32).astype(BF16)
        oc_ref[r, :] = jnp.dot(h, w_ref[:, 2 * D_A + D_QKV:], preferred_element_type=F32).astype(BF16)
        qkv = jnp.dot(h, w_ref[:, 2 * D_A:2 * D_A + D_QKV], preferred_element_type=F32)
        qk = qkv[:, :n]
        lane = lax.broadcasted_iota(jnp.int32, qk.shape, 1) % HEAD_DIM
        swapped = jnp.where(lane < half, pltpu.roll(qk, n - half, axis=1), pltpu.roll(qk, half, axis=1))
        oq_ref[r, :] = (qk * rc_ref[r, :] + swapped * rs_ref[r, :]).astype(BF16)
        ov_ref[:, r] = qkv[:, n:].T.astype(BF16)


def _in_proj(x2, g, w_bf, layer, rc, rs, seq, tm):
    n_tok = x2.shape[0]
    nrb = seq // tm
    return pl.pallas_call(
        _in_proj_kernel,
        grid=(n_tok // tm,),
        in_specs=[
            pl.BlockSpec((tm, D_MODEL), lambda i: (i, 0)),
            pl.BlockSpec((1, D_MODEL), lambda i: (0, 0)),
            pl.BlockSpec((None, D_MODEL, D_IN), lambda i: (layer, 0, 0)),
            pl.BlockSpec((tm, D_B + D_KV), lambda i: (i % nrb, 0)),
            pl.BlockSpec((tm, D_B + D_KV), lambda i: (i % nrb, 0)),
        ],
        out_specs=[
            pl.BlockSpec((tm, 2 * D_A), lambda i: (i, 0)),
            pl.BlockSpec((tm, D_B + D_KV), lambda i: (i, 0)),
            pl.BlockSpec((D_KV, tm), lambda i: (0, i)),
            pl.BlockSpec((tm, 3 * D_C), lambda i: (i, 0)),
        ],
        out_shape=[
            jax.ShapeDtypeStruct((n_tok, 2 * D_A), BF16),
            jax.ShapeDtypeStruct((n_tok, D_B + D_KV), BF16),
            jax.ShapeDtypeStruct((D_KV, n_tok), BF16),
            jax.ShapeDtypeStruct((n_tok, 3 * D_C), BF16),
        ],
        compiler_params=_cparams(("parallel",)),
        name="in_proj",
    )(x2, g, w_bf, rc, rs)


def _dwconv_chunk(src_ref, ci, nchunk, rows, w, b, pad_left):
    halo = BF16_ROWS
    r0 = pl.multiple_of(ci * rows, rows)
    seq = nchunk * rows
    top = src_ref[pl.ds(pl.multiple_of(jnp.maximum(r0 - halo, 0), halo), halo), :]
    bot = src_ref[pl.ds(pl.multiple_of(jnp.minimum(r0 + rows, seq - halo), halo), halo), :]
    top = jnp.where(ci > 0, top, jnp.zeros_like(top))
    bot = jnp.where(ci < nchunk - 1, bot, jnp.zeros_like(bot))
    win = jnp.concatenate([top, src_ref[pl.ds(r0, rows), :], bot], axis=0).astype(F32)
    return _conv_taps(win, rows, w, b, pad_left, halo)


def _conv_taps(win, rows, w, b, pad_left, halo):
    total = rows + 2 * halo
    acc = None
    for k in range(w.shape[0]):
        shift = (pad_left - k) % total
        tap = win if shift == 0 else pltpu.roll(win, shift, axis=0)
        term = tap[halo:halo + rows] * w[k:k + 1, :]
        acc = term if acc is None else acc + term
    return acc + b


def _dwconv_block(src_ref, j, nblk, w, b, pad_left):
    r0 = j * DFT_BLK
    halo = 2 * SUBLANES
    zeros = jnp.zeros((halo, src_ref.shape[1]), src_ref.dtype)
    top = zeros if j == 0 else src_ref[r0 - halo:r0, :]
    bot = zeros if j == nblk - 1 else src_ref[r0 + DFT_BLK:r0 + DFT_BLK + halo, :]
    win = jnp.concatenate([top, src_ref[r0:r0 + DFT_BLK, :], bot], axis=0).astype(F32)
    return _conv_taps(win, DFT_BLK, w, b, pad_left, halo)


def _softplus(x):
    return jnp.maximum(x, 0.0) + jnp.log1p(jnp.exp(-jnp.abs(x)))


def _gelu_tanh(x):
    c = math.sqrt(2.0 / math.pi)
    return x * (0.5 * (1.0 + jnp.tanh(c * (x + 0.044715 * (x * x * x)))))


def _lru_kernel(u_ref, gate_ref, cw_ref, cb_ref, w_ref, bias_ref, lam_ref, o_ref,
                xci, gg, af, bf, ar, br, pfs, hfs, prs, hrs, g0, g1, *, seq, rows, unroll):
    seg = seq // SCAN_SEGS
    tiles = rows // SCAN_SEGS
    nchunk = seq // rows
    per_seg = seg // rows
    cw = cw_ref[...]
    cb = cb_ref[...]
    w = w_ref[0] * 0.5
    bias = bias_ref[0] * 0.5
    hnsp = (-0.5 * C_LRU) * _softplus(-lam_ref[0])

    def conv(ci, carry):
        r0 = pl.multiple_of(ci * rows, rows)
        dst = pl.ds((ci % per_seg) * (rows * SCAN_SEGS) + ci // per_seg, rows, stride=SCAN_SEGS)
        xci[dst, :] = _dwconv_chunk(u_ref, ci, nchunk, rows, cw, cb, 2)
        gg[dst, :] = _gelu_tanh(gate_ref[pl.ds(r0, rows), :].astype(F32))
        return carry

    lax.fori_loop(0, nchunk, conv, 0)

    grows = g0.shape[0]
    ngate = seq // grows

    def gate_matmul(ci, g_ref):
        r0 = pl.multiple_of(ci * grows, grows)
        g_ref[...] = jnp.dot(xci[pl.ds(r0, grows), :].astype(BF16), w, preferred_element_type=F32)

    def gate_math(ci, g_ref):
        r0 = pl.multiple_of(ci * grows, grows)
        xc = xci[pl.ds(r0, grows), :]
        t = jnp.tanh(g_ref[...] + bias)
        hxc = 0.5 * xc
        for d, (a_ref, b_ref) in enumerate(((af, bf), (ar, br))):
            tr = t[:, (2 * d) * LANES:(2 * d + 1) * LANES]
            ti = t[:, (2 * d + 1) * LANES:(2 * d + 2) * LANES]
            hn = hnsp[:, d * LANES:(d + 1) * LANES]
            log_a = tr * hn + hn
            a = jnp.exp(log_a)
            nem = (-1.0 - a * a) * jnp.tanh(log_a)
            a_ref[pl.ds(r0, grows), :] = a
            b_ref[pl.ds(r0, grows), :] = jnp.sqrt(nem) * (ti * hxc + hxc)

    def gates(k, carry):
        gate_matmul(2 * k + 1, g1)
        gate_math(2 * k, g0)
        gate_matmul(jnp.minimum(2 * k + 2, ngate - 1), g0)
        gate_math(2 * k + 1, g1)
        return carry

    gate_matmul(0, g0)
    lax.fori_loop(0, ngate // 2, gates, 0)

    def scan(it, carry):
        def two_steps(p, h, a_ref, b_ref, p_out, h_out, r0, r1):
            a0 = a_ref[pl.ds(r0, SCAN_SEGS), :]
            b0 = b_ref[pl.ds(r0, SCAN_SEGS), :]
            a1 = a_ref[pl.ds(r1, SCAN_SEGS), :]
            a01 = a1 * a0
            b01 = a1 * b0 + b_ref[pl.ds(r1, SCAN_SEGS), :]
            p_out[pl.ds(r0, SCAN_SEGS), :] = a0 * p
            h_out[pl.ds(r0, SCAN_SEGS), :] = a0 * h + b0
            p = a01 * p
            h = a01 * h + b01
            p_out[pl.ds(r1, SCAN_SEGS), :] = p
            h_out[pl.ds(r1, SCAN_SEGS), :] = h
            return p, h

        pf, hf, pr, hr = carry
        for u in range(0, unroll, 2):
            i = it * unroll + u
            row = lambda pos: pl.multiple_of(pos * SCAN_SEGS, SCAN_SEGS)
            pf, hf = two_steps(pf, hf, af, bf, pfs, hfs, row(i), row(i + 1))
            pr, hr = two_steps(pr, hr, ar, br, prs, hrs, row(seg - 1 - i), row(seg - 2 - i))
        return pf, hf, pr, hr

    one = jnp.ones((SCAN_SEGS, LANES), F32)
    zero = jnp.zeros((SCAN_SEGS, LANES), F32)
    pf, hf, pr, hr = lax.fori_loop(0, seg // unroll, scan, (one, zero, one, zero))

    sub = lax.broadcasted_iota(jnp.int32, (SCAN_SEGS, LANES), 0)
    cf = zero
    cr = zero
    for _ in range(SCAN_SEGS - 1):
        cf = jnp.where(sub == 0, 0.0, pltpu.roll(hf + pf * cf, 1, axis=0))
        cr = jnp.where(sub == SCAN_SEGS - 1, 0.0, pltpu.roll(hr + pr * cr, SCAN_SEGS - 1, axis=0))

    def combine(ci, carry):
        r0 = pl.multiple_of(ci * rows, rows)
        sl = pl.ds(r0, rows)
        tile3 = lambda ref: ref[sl, :].reshape(tiles, SCAN_SEGS, LANES)
        h = (tile3(hfs) + tile3(pfs) * cf[None]) + (tile3(hrs) + tile3(prs) * cr[None])
        out = h.reshape(rows, LANES) * gg[sl, :]
        for k in range(tiles):
            o_ref[pl.ds(ci * tiles + k, SCAN_SEGS, stride=seg), :] = (
                out[k * SCAN_SEGS:(k + 1) * SCAN_SEGS])
        return carry

    lax.fori_loop(0, nchunk, combine, 0)


def _lru(pa, cw, cb, wg, bias, lam, batch, seq, rows=128, gate_rows=512, unroll=16):
    n_tok = pa.shape[0]
    ng = D_A // LANES
    kern = functools.partial(_lru_kernel, seq=seq, rows=rows, unroll=unroll)
    return pl.pallas_call(
        kern,
        grid=(batch, ng),
        in_specs=[
            pl.BlockSpec((seq, LANES), lambda b, g: (b, g)),
            pl.BlockSpec((seq, LANES), lambda b, g: (b, ng + g)),
            pl.BlockSpec((cw.shape[0], LANES), lambda b, g: (0, g)),
            pl.BlockSpec((1, LANES), lambda b, g: (0, g)),
            pl.BlockSpec((1, LANES, 4 * LANES), lambda b, g: (g, 0, 0)),
            pl.BlockSpec((1, 1, 4 * LANES), lambda b, g: (g, 0, 0)),
            pl.BlockSpec((1, 1, 2 * LANES), lambda b, g: (g, 0, 0)),
        ],
        out_specs=pl.BlockSpec((seq, LANES), lambda b, g: (b, g)),
        out_shape=jax.ShapeDtypeStruct((n_tok, D_A), F32),
        scratch_shapes=[pltpu.VMEM((seq, LANES), F32) for _ in range(10)]
        + [pltpu.VMEM((gate_rows, 4 * LANES), F32) for _ in range(2)],
        compiler_params=_cparams(("parallel", "parallel")),
        name="rglru",
    )(pa, pa, cw, cb, wg, bias, lam)


_ATTN_STRAIGHT = (0, 2, 3, 5)
_ATTN_ROLLED = (1, 4)
_ONES_ROWS = 16


def _attn_kernel(sink_ref, bias_ref, q_ref, k_ref, vt_ref, o_ref, s_scr, p_scr, e_scr, *, seq):
    band = 3 * BLOCK
    nblk = seq // BLOCK
    lo_q = lax.broadcasted_iota(jnp.int32, (BLOCK, 2 * HEAD_DIM), 1) < HEAD_DIM
    ones = jnp.ones((_ONES_ROWS, band), BF16)
    nt = (((1,), (1,)), ((), ()))
    stack = _ATTN_STRAIGHT + _ATTN_ROLLED
    ns = len(_ATTN_STRAIGHT) * BLOCK

    def window(j):
        q0 = j * BLOCK
        return q0, pl.multiple_of(jnp.clip(q0 - BLOCK, 0, seq - band), BLOCK)

    def scores(j, slot):
        q0, k0 = window(j)
        rows = pl.ds(pl.multiple_of(q0, BLOCK), BLOCK)
        kb = k_ref[pl.ds(k0, band), :]
        kbs = pltpu.roll(kb, HEAD_DIM, axis=1)

        def own_half(h):
            qt = q_ref[rows, (h // 2) * 2 * HEAD_DIM:(h // 2 + 1) * 2 * HEAD_DIM]
            return jnp.where(lo_q, qt, 0.0) if h % 2 == 0 else jnp.where(lo_q, 0.0, qt)

        qa = jnp.concatenate([own_half(h) for h in _ATTN_STRAIGHT], axis=0)
        qb = jnp.concatenate([own_half(h) for h in _ATTN_ROLLED], axis=0)
        s_scr[slot, :, :ns] = lax.dot_general(kb, qa, nt, preferred_element_type=F32)
        s_scr[slot, :, ns:] = lax.dot_general(kbs, qb, nt, preferred_element_type=F32)

    def softmax(j, slot):
        q0, k0 = window(j)
        bias = bias_ref[(q0 - k0) // BLOCK]
        for h in range(N_Q_HEADS):
            src = stack.index(h) * BLOCK
            s = s_scr[slot, :, src:src + BLOCK] + bias
            sk = sink_ref[h] * LOG2E
            m = jnp.maximum(jnp.max(s, axis=0, keepdims=True), sk)
            p_scr[slot, :, h * BLOCK:(h + 1) * BLOCK] = jnp.exp2(s - m).astype(BF16)
            e_scr[slot, :, h * BLOCK:(h + 1) * BLOCK] = jnp.broadcast_to(
                jnp.exp2(sk - m), (SUBLANES, BLOCK))

    def values(j, slot):
        q0, k0 = window(j)
        vt = vt_ref[:, pl.ds(k0, band)]
        outs = []
        for kv in range(N_KV_HEADS):
            cols = slice(kv * GROUP * BLOCK, (kv + 1) * GROUP * BLOCK)
            lhs = jnp.concatenate([vt[kv * HEAD_DIM:(kv + 1) * HEAD_DIM], ones], axis=0)
            ov = jnp.dot(lhs, p_scr[slot, :, cols], preferred_element_type=F32)
            res = ov[:HEAD_DIM] / (ov[HEAD_DIM:HEAD_DIM + 1] + e_scr[slot, 0:1, cols])
            outs += [res[:, g * BLOCK:(g + 1) * BLOCK] for g in range(GROUP)]
        rows = pl.ds(pl.multiple_of(q0, BLOCK), BLOCK)
        o_ref[rows, :] = jnp.concatenate(outs, axis=0).T

    scores(0, 0)
    softmax(0, 0)
    scores(1, 1)

    def step(jj, carry):
        j = 2 * jj
        values(j, 0)
        softmax(j + 1, 1)
        scores(j + 2, 0)
        values(j + 1, 1)
        softmax(j + 2, 0)
        scores(j + 3, 1)
        return carry

    lax.fori_loop(0, (nblk - 2) // 2, step, 0)
    values(nblk - 2, 0)
    softmax(nblk - 1, 1)
    values(nblk - 1, 1)


def _attn_bias_table():
    ki = np.arange(3 * BLOCK)[None, :, None]
    qi = np.arange(BLOCK)[None, None, :]
    off = (np.arange(3) * BLOCK)[:, None, None]
    return jnp.asarray(np.where(np.abs(off + qi - ki) <= WINDOW, 0.0, NEG).astype(np.float32))


def _attention(pq, vt, sink, bias, batch, seq):
    n_tok = pq.shape[0]
    band = 3 * BLOCK
    kern = functools.partial(_attn_kernel, seq=seq)
    return pl.pallas_call(
        kern,
        grid=(batch,),
        in_specs=[
            pl.BlockSpec(memory_space=pltpu.SMEM),
            pl.BlockSpec(bias.shape, lambda b: (0, 0, 0)),
            pl.BlockSpec((seq, D_B), lambda b: (b, 0)),
            pl.BlockSpec((seq, D_KV), lambda b: (b, D_B // D_KV)),
            pl.BlockSpec((D_KV, seq), lambda b: (0, b)),
        ],
        out_specs=pl.BlockSpec((seq, D_B), lambda b: (b, 0)),
        out_shape=jax.ShapeDtypeStruct((n_tok, D_B), F32),
        scratch_shapes=[pltpu.VMEM((2, band, N_Q_HEADS * BLOCK), F32),
                        pltpu.VMEM((2, band, N_Q_HEADS * BLOCK), BF16),
                        pltpu.VMEM((2, SUBLANES, N_Q_HEADS * BLOCK), F32)],
        compiler_params=_cparams(("parallel",)),
        name="win_attn",
    )(sink, bias, pq, pq, vt)


def _hy_filter_kernel(z_ref, w1_ref, b1_ref, fr_ref, w2_ref, b2_ref, w3_ref, dec_ref, o_ref):
    hi = lax.Precision.HIGHEST
    fr = fr_ref[...]
    h = jnp.sin(fr * (jnp.dot(z_ref[...], w1_ref[...], preferred_element_type=F32, precision=hi)
                      + b1_ref[...]))
    h = jnp.sin(fr * (jnp.dot(h, w2_ref[...], preferred_element_type=F32, precision=hi) + b2_ref[...]))
    f = jnp.dot(h, w3_ref[...], preferred_element_type=F32, precision=hi) * dec_ref[...]
    for half in range(2):
        for d in range(2):
            c0 = (2 * half + d) * D_C
            o_ref[d, half] = f[:, c0:c0 + D_C]


def _hy_filter(zemb2, w1, b1, fr, w2, b2, w3, dec4, rows=512):
    n = zemb2.shape[0]
    full = lambda a: pl.BlockSpec(a.shape, lambda i: (0,) * a.ndim)
    return pl.pallas_call(
        _hy_filter_kernel,
        grid=(n // rows,),
        in_specs=[pl.BlockSpec((rows, zemb2.shape[1]), lambda i: (i, 0)),
                  full(w1), full(b1), full(fr), full(w2), full(b2), full(w3),
                  pl.BlockSpec((rows, 4 * D_C), lambda i: (i, 0))],
        out_specs=pl.BlockSpec((2, 2, rows, D_C), lambda i: (0, 0, i, 0)),
        out_shape=jax.ShapeDtypeStruct((2, 2, n, D_C), F32),
        compiler_params=_cparams(("parallel",)),
        name="hyena_filter",
    )(zemb2, w1, b1, fr, w2, b2, w3, dec4)


def _dft_outer_fwd(zs, ka_ref, o_ref):
    nj, _, lanes = zs.shape
    pair = BF16_ROWS // DFT_SUB
    for p in range(DFT_BLK // BF16_ROWS):
        parts = []
        for s in range(pair * p, pair * (p + 1)):
            xg = zs[:, s * DFT_SUB:(s + 1) * DFT_SUB, :].reshape(nj * DFT_SUB, lanes).astype(BF16)
            c = jnp.dot(ka_ref[...], xg, preferred_element_type=F32)
            parts.append(c.reshape(2, DFT_NBLK, DFT_SUB, lanes))
        o_ref[0, :, :, p * BF16_ROWS:(p + 1) * BF16_ROWS, :] = (
            jnp.concatenate(parts, axis=2).astype(BF16))


def _hy_fwd_data_kernel(x1_ref, v_ref, cw_ref, cb_ref, ka_ref, o_ref, zs, *, seq):
    nj = seq // DFT_BLK
    cw = cw_ref[...]
    cb = cb_ref[...]
    for j in range(nj):
        x1 = _dwconv_block(x1_ref, j, nj, cw[:, D_C:2 * D_C], cb[:, D_C:2 * D_C], 1)
        v = _dwconv_block(v_ref, j, nj, cw[:, 2 * D_C:], cb[:, 2 * D_C:], 1)
        zs[j] = v * x1
    _dft_outer_fwd(zs, ka_ref, o_ref)


def _hy_fwd_filt_kernel(f_ref, ka_ref, o_ref):
    _dft_outer_fwd(f_ref.at[0], ka_ref, o_ref)


def _hy_fwd_data(pc, cw, cb, ka, batch, seq):
    nj = seq // DFT_BLK
    kern = functools.partial(_hy_fwd_data_kernel, seq=seq)
    return pl.pallas_call(
        kern,
        grid=(batch,),
        in_specs=[
            pl.BlockSpec((seq, D_C), lambda b: (b, 1)),
            pl.BlockSpec((seq, D_C), lambda b: (b, 2)),
            pl.BlockSpec(cw.shape, lambda b: (0, 0)),
            pl.BlockSpec(cb.shape, lambda b: (0, 0)),
            pl.BlockSpec(ka.shape, lambda b: (0, 0)),
        ],
        out_specs=pl.BlockSpec((1, 2, DFT_NBLK, DFT_BLK, D_C), lambda b: (b, 0, 0, 0, 0)),
        out_shape=jax.ShapeDtypeStruct((batch, 2, DFT_NBLK, DFT_BLK, D_C), BF16),
        scratch_shapes=[pltpu.VMEM((nj, DFT_BLK, D_C), F32)],
        compiler_params=_cparams(("parallel",)),
        name="hyena_dft_outer",
    )(pc, pc, cw, cb, ka)


def _hy_fwd_filt(filt4, ka):
    ndir, nj = filt4.shape[0], filt4.shape[1]
    return pl.pallas_call(
        _hy_fwd_filt_kernel,
        grid=(ndir,),
        in_specs=[
            pl.BlockSpec((1, nj, DFT_BLK, D_C), lambda b: (b, 0, 0, 0)),
            pl.BlockSpec(ka.shape, lambda b: (0, 0)),
        ],
        out_specs=pl.BlockSpec((1, 2, DFT_NBLK, DFT_BLK, D_C), lambda b: (b, 0, 0, 0, 0)),
        out_shape=jax.ShapeDtypeStruct((ndir, 2, DFT_NBLK, DFT_BLK, D_C), BF16),
        compiler_params=_cparams(("parallel",)),
        name="hyena_dft_outer_filter",
    )(filt4, ka)


def _hy_inner_kernel(c_ref, f_ref, hbias_ref, g_ref, gi_ref, o_ref, *, batch, kper):
    hbias = hbias_ref[...]
    for q in range(kper):
        g = g_ref[q]
        gi = gi_ref[q]
        hf = jnp.dot(g, f_ref[0, :, q].reshape(2 * DFT_BLK, D_C), preferred_element_type=F32)
        hb = jnp.dot(g, f_ref[1, :, q].reshape(2 * DFT_BLK, D_C), preferred_element_type=F32)
        hre = hf[:DFT_BLK] + hb[:DFT_BLK] + hbias
        him = hf[DFT_BLK:] - hb[DFT_BLK:]
        for b in range(batch):
            x = jnp.dot(g, c_ref[b, :, q].reshape(2 * DFT_BLK, D_C), preferred_element_type=F32)
            xre, xim = x[:DFT_BLK], x[DFT_BLK:]
            y = jnp.concatenate([xre * hre - xim * him, xre * him + xim * hre], axis=0)
            d = jnp.dot(gi, y.astype(BF16), preferred_element_type=F32)
            o_ref[b, :, q] = d.reshape(2, DFT_BLK, D_C).astype(BF16)


def _hy_inner(cdata, cfilt, hbias, g, gi, kper=4):
    batch = cdata.shape[0]
    kern = functools.partial(_hy_inner_kernel, batch=batch, kper=kper)
    blk = lambda nb: pl.BlockSpec((nb, 2, kper, DFT_BLK, D_C), lambda k: (0, 0, k, 0, 0))
    return pl.pallas_call(
        kern,
        grid=(DFT_NBLK // kper,),
        in_specs=[blk(batch), blk(cfilt.shape[0]), pl.BlockSpec(hbias.shape, lambda k: (0, 0)),
                  pl.BlockSpec((kper, 2 * DFT_BLK, 2 * DFT_BLK), lambda k: (k, 0, 0)),
                  pl.BlockSpec((kper, 2 * DFT_BLK, 2 * DFT_BLK), lambda k: (k, 0, 0))],
        out_specs=blk(batch),
        out_shape=jax.ShapeDtypeStruct(cdata.shape, BF16),
        compiler_params=_cparams(("parallel",)),
        name="hyena_dft_inner",
    )(cdata, cfilt, hbias, g, gi)


def _hy_out_kernel(d_ref, x0_ref, cw_ref, cb_ref, kai_ref, o_ref, ys, *, seq):
    nj = seq // DFT_BLK
    pair = BF16_ROWS // DFT_SUB
    for p in range(DFT_BLK // BF16_ROWS):
        d = d_ref[0, :, :, p * BF16_ROWS:(p + 1) * BF16_ROWS, :].astype(F32)
        for s in range(pair):
            rhs = d[:, :, s * DFT_SUB:(s + 1) * DFT_SUB, :].reshape(2 * DFT_NBLK * DFT_SUB, D_C)
            y = jnp.dot(kai_ref[...], rhs.astype(BF16), preferred_element_type=F32)
            r0 = p * BF16_ROWS + s * DFT_SUB
            ys[:, r0:r0 + DFT_SUB, :] = y.reshape(nj, DFT_SUB, D_C)
    cw = cw_ref[...]
    cb = cb_ref[...]
    for j in range(nj):
        x0 = _dwconv_block(x0_ref, j, nj, cw[:, :D_C], cb[:, :D_C], 1)
        o_ref[j * DFT_BLK:(j + 1) * DFT_BLK, :] = ys[j] * x0


def _hy_out(dd, pc, cw, cb, kai, batch, seq):
    n_tok = pc.shape[0]
    nj = seq // DFT_BLK
    kern = functools.partial(_hy_out_kernel, seq=seq)
    whole = lambda a: pl.BlockSpec(a.shape, lambda b: (0, 0))
    return pl.pallas_call(
        kern,
        grid=(batch,),
        in_specs=[
            pl.BlockSpec((1, 2, DFT_NBLK, DFT_BLK, D_C), lambda b: (b, 0, 0, 0, 0)),
            pl.BlockSpec((seq, D_C), lambda b: (b, 0)),
            whole(cw), whole(cb), whole(kai),
        ],
        out_specs=pl.BlockSpec((seq, D_C), lambda b: (b, 0)),
        out_shape=jax.ShapeDtypeStruct((n_tok, D_C), F32),
        scratch_shapes=[pltpu.VMEM((nj, DFT_BLK, D_C), F32)],
        compiler_params=_cparams(("parallel",)),
        name="hyena_out",
    )(dd, pc, cw, cb, kai)


def _mix_mlp_kernel(ya_ref, yb_ref, yc_ref, x_ref, ga_ref, gb_ref, gc_ref, wo_ref, g_ref, wu_ref,
                    wd_ref, gf_ref, o_ref, *, ff_chunk, final_norm):
    y = jnp.concatenate([_rms(ya_ref[...], ga_ref[...]), _rms(yb_ref[...], gb_ref[...]),
                         _rms(yc_ref[...], gc_ref[...])], axis=-1).astype(BF16)
    x = x_ref[...] + jnp.dot(y, wo_ref[...], preferred_element_type=F32)
    h = _rms(x, g_ref[...]).astype(BF16)
    acc = x
    for c in range(D_FF // ff_chunk):
        sl = slice(c * ff_chunk, (c + 1) * ff_chunk)
        u = jnp.maximum(jnp.dot(h, wu_ref[:, sl], preferred_element_type=F32), 0.0)
        acc = acc + jnp.dot((u * u).astype(BF16), wd_ref[sl, :], preferred_element_type=F32)
    if final_norm:
        acc = _rms(acc, gf_ref[...])
    o_ref[...] = acc


def _mix_mlp(ya, yb, yc, x2, ga, gb, gc, wo_bf, g, wu_bf, wd_bf, gf, layer, tm, final_norm,
             ff_chunk=1024):
    n_tok = x2.shape[0]
    kern = functools.partial(_mix_mlp_kernel, ff_chunk=ff_chunk, final_norm=final_norm)
    row = lambda w: pl.BlockSpec((tm, w), lambda i: (i, 0))
    vec = lambda w: pl.BlockSpec((1, w), lambda i: (0, 0))
    resident = lambda r, c: pl.BlockSpec((None, r, c), lambda i: (layer, 0, 0),
                                         pipeline_mode=pl.Buffered(1))
    return pl.pallas_call(
        kern,
        grid=(n_tok // tm,),
        in_specs=[row(D_A), row(D_B), row(D_C), row(D_MODEL), vec(D_A), vec(D_B), vec(D_C),
                  resident(D_MODEL, D_MODEL), vec(D_MODEL), resident(D_MODEL, D_FF),
                  resident(D_FF, D_MODEL), vec(D_MODEL)],
        out_specs=row(D_MODEL),
        out_shape=jax.ShapeDtypeStruct((n_tok, D_MODEL), F32),
        compiler_params=_cparams(("parallel",)),
        name="mix_mlp",
    )(ya, yb, yc, x2, ga, gb, gc, wo_bf, g, wu_bf, wd_bf, gf)


def _rope_tables(seq):
    pos = np.arange(seq, dtype=np.float32)
    inv_freq = (np.float32(ROPE_THETA) ** (-np.arange(0, ROT_DIM, 2, dtype=np.float32) / ROT_DIM))
    ang = (pos[:, None] * inv_freq[None, :]).astype(np.float32)
    cos, sin = np.cos(ang).astype(np.float32), np.sin(ang).astype(np.float32)
    rest = HEAD_DIM - ROT_DIM
    c_head = np.concatenate([cos, cos, np.ones((seq, rest), np.float32)], axis=1)
    s_head = np.concatenate([-sin, sin, np.zeros((seq, rest), np.float32)], axis=1)
    scale = np.float32(HEAD_DIM ** -0.5 * LOG2E)
    c = np.concatenate([np.tile(c_head, (1, N_Q_HEADS)) * scale, np.tile(c_head, (1, N_KV_HEADS))], axis=1)
    s = np.concatenate([np.tile(s_head, (1, N_Q_HEADS)) * scale, np.tile(s_head, (1, N_KV_HEADS))], axis=1)
    return jnp.asarray(c), jnp.asarray(s)


def _hyena_position_tables(seq):
    t = np.linspace(0.0, 1.0, seq, dtype=np.float32)[:, None]
    w = (2.0 * math.pi * np.arange(seq, dtype=np.float32)[:, None] / seq).astype(np.float32)
    f = np.linspace(1e-4, HY_BANDS - 1, HY_BANDS, dtype=np.float32)[None, :]
    fw = (f * w).astype(np.float32)
    z = np.concatenate([t, np.cos(fw), -np.sin(fw)], axis=-1).astype(np.float32)
    deltas = np.abs(np.linspace(HY_MIN_DECAY, HY_MAX_DECAY, D_C, dtype=np.float32))
    decay = np.exp(-t * deltas[None, :]).astype(np.float32)
    decay_b = decay.copy()
    decay_b[0] = 0.0
    half = seq // 2
    zp = np.zeros((half, 2, HY_WIDTH), np.float32)
    zp[:, 0, :HY_EMB] = z[:half]
    zp[:, 1, :HY_EMB] = z[half:]
    dec4 = np.concatenate([decay[:half], decay_b[:half], decay[half:], decay_b[half:]], axis=1)
    return jnp.asarray(zp.reshape(half, 2 * HY_WIDTH)), jnp.asarray(dec4)


def _blockdiag2(a):
    z = jnp.zeros_like(a)
    return jnp.concatenate([jnp.concatenate([a, z], axis=1), jnp.concatenate([z, a], axis=1)], axis=0)


def _dft_tables(seq):
    n = 2 * seq
    nj = seq // DFT_BLK
    kk = np.arange(DFT_NBLK)
    ang = 2.0 * np.pi * np.outer(kk, np.arange(nj)) / DFT_NBLK
    eye = np.eye(DFT_SUB)
    ka = np.concatenate([np.kron(np.cos(ang), eye), np.kron(-np.sin(ang), eye)], axis=0)
    kai = np.concatenate([np.kron(np.cos(ang).T, eye), np.kron(-np.sin(ang).T, eye)], axis=1) / n
    m = np.arange(DFT_BLK)
    k = kk[:, None, None] + DFT_NBLK * np.arange(DFT_BLK)[None, :, None]
    ph = 2.0 * np.pi * ((k * m[None, None, :]) % n) / n
    gre, gim = np.cos(ph), -np.sin(ph)
    g = np.concatenate([np.concatenate([gre, -gim], axis=2), np.concatenate([gim, gre], axis=2)], axis=1)
    gi = np.transpose(g, (0, 2, 1))
    as_bf = lambda a: jnp.asarray(a.astype(np.float32)).astype(BF16)
    return as_bf(ka), as_bf(kai), as_bf(g), as_bf(gi)


def _lru_blockdiag(w):
    nb = w.shape[1] // 2
    w = w.reshape(2, nb, 2, w.shape[2], w.shape[3])
    z = jnp.zeros_like(w[:, :, 0])
    top = jnp.concatenate([w[:, :, 0], z], axis=-1)
    bot = jnp.concatenate([z, w[:, :, 1]], axis=-1)
    return jnp.concatenate([top, bot], axis=-2)


def kernel(x, norm_mix_g, w_in, conv_a_w, conv_a_b, lru_wa, lru_ba, lru_wx, lru_bx, lru_lambda,
           attn_sink, hy_conv_w, hy_conv_b, hy_w1, hy_b1, hy_freq, hy_w2, hy_b2, hy_w3, hy_bias,
           gnorm_a, gnorm_b, gnorm_c, w_out, norm_mlp_g, w_up, w_down, final_norm_g):
    batch, seq, _ = x.shape
    depth = w_in.shape[0]
    n_tok = batch * seq
    tm = 1024
    ng = D_A // LANES

    rc, rs = _rope_tables(seq)
    attn_bias = _attn_bias_table()
    zemb, dec4 = _hyena_position_tables(seq)
    ka, kai, gtab, gitab = _dft_tables(seq)

    w_in_bf, w_out_bf = w_in.astype(BF16), w_out.astype(BF16)
    w_up_bf, w_down_bf = w_up.astype(BF16), w_down.astype(BF16)

    xs = x.reshape(n_tok, D_MODEL)
    for i in range(depth):
        pa, pq, vt, pc = _in_proj(xs, norm_mix_g[i][None], w_in_bf, i, rc, rs, seq, tm)

        wa, wx = _lru_blockdiag(lru_wa[i]), _lru_blockdiag(lru_wx[i])
        wg = jnp.concatenate([wa[0], wx[0], wa[1], wx[1]], axis=-1).astype(BF16)
        tile = lambda v: v.reshape(ng, 1, LANES)
        bias = jnp.concatenate([tile(lru_ba[i][0]), tile(lru_bx[i][0]),
                                tile(lru_ba[i][1]), tile(lru_bx[i][1])], axis=-1)
        lam = jnp.concatenate([tile(lru_lambda[i][0]), tile(lru_lambda[i][1])], axis=-1)
        y_a = _lru(pa, conv_a_w[i], conv_a_b[i][None], wg, bias, lam, batch, seq)

        y_b = _attention(pq, vt, attn_sink[i], attn_bias, batch, seq)

        w1p = jnp.zeros((HY_WIDTH, HY_WIDTH), F32).at[:HY_EMB].set(hy_w1[i])
        pair = lambda v: jnp.concatenate([v, v])[None]
        filt = _hy_filter(zemb, _blockdiag2(w1p), pair(hy_b1[i]), pair(hy_freq[i]),
                          _blockdiag2(hy_w2[i]), pair(hy_b2[i]), _blockdiag2(hy_w3[i]), dec4)
        cw, cb = hy_conv_w[i], hy_conv_b[i][None]
        cdata = _hy_fwd_data(pc, cw, cb, ka, batch, seq)
        cfilt = _hy_fwd_filt(filt.reshape(2, seq // DFT_BLK, DFT_BLK, D_C), ka)
        dd = _hy_inner(cdata, cfilt, hy_bias[i][None], gtab, gitab)
        y_c = _hy_out(dd, pc, cw, cb, kai, batch, seq)

        xs = _mix_mlp(y_a, y_b, y_c, xs, gnorm_a[i][None], gnorm_b[i][None], gnorm_c[i][None],
                      w_out_bf, norm_mlp_g[i][None], w_up_bf, w_down_bf, final_norm_g[None], i, tm,
                      final_norm=(i == depth - 1))
    return xs.reshape(batch, seq, D_MODEL)
```

```python
import functools
import math

import numpy as np
import jax
import jax.numpy as jnp
from jax import lax
from jax.experimental import pallas as pl
from jax.experimental.pallas import tpu as pltpu

F32 = jnp.float32
BF16 = jnp.bfloat16

D_MODEL = 1024
D_A = 384
D_B = 384
D_C = 256
HEAD_DIM = 64
N_Q_HEADS = 6
N_KV_HEADS = 2
GROUP = N_Q_HEADS // N_KV_HEADS
D_KV = N_KV_HEADS * HEAD_DIM
D_QKV = D_B + 2 * D_KV
D_IN = 2 * D_A + D_QKV + 3 * D_C
C_LRU = 8.0
WINDOW = 128
BLOCK = 128
ROPE_THETA = 500000.0
ROT_DIM = HEAD_DIM // 4
HY_EMB = 33
HY_BANDS = (HY_EMB - 1) // 2
HY_WIDTH = 64
HY_TARGET = 1e-2
HY_MAX_DECAY = math.log(HY_TARGET) / 0.3
HY_MIN_DECAY = math.log(HY_TARGET) / 1.5
D_FF = 4 * D_MODEL
EPS = 1e-6
NEG = -1e30
LOG2E = math.log2(math.e)

LANES = 128
SUBLANES = 8
VMEM_LIMIT = 56 * 1024 * 1024

SCAN_SEGS = SUBLANES
DFT_BLK = 128
DFT_NBLK = 64
DFT_SUB = 8
BF16_ROWS = 16
IN_PROJ_CHUNKS = 2


def _cparams(sem):
    return pltpu.CompilerParams(dimension_semantics=sem, vmem_limit_bytes=VMEM_LIMIT)


def _rms(x, g):
    return x * lax.rsqrt(jnp.mean(x * x, axis=-1, keepdims=True) + EPS) * g


def _in_proj_kernel(x_ref, g_ref, w_ref, rc_ref, rs_ref, oa_ref, oq_ref, ov_ref, oc_ref):
    n = D_B + D_KV
    half = ROT_DIM // 2
    tm = x_ref.shape[0]
    for c in range(IN_PROJ_CHUNKS):
        r = slice(c * tm // IN_PROJ_CHUNKS, (c + 1) * tm // IN_PROJ_CHUNKS)
        h = _rms(x_ref[r, :], g_ref[...]).astype(BF16)
        oa_ref[r, :] = jnp.dot(h, w_ref[:, :2 * D_A], preferred_element_type=F32).astype(BF16)
        oc_ref[r, :] = jnp.dot(h, w_ref[:, 2 * D_A + D_QKV:], preferred_element_type=F32).astype(BF16)
        qkv = jnp.dot(h, w_ref[:, 2 * D_A:2 * D_A + D_QKV], preferred_element_type=F32)
        qk = qkv[:, :n]
        lane = lax.broadcasted_iota(jnp.int32, qk.shape, 1) % HEAD_DIM
        swapped = jnp.where(lane < half, pltpu.roll(qk, n - half, axis=1), pltpu.roll(qk, half, axis=1))
        wide = lambda t: jnp.concatenate([t[:, :2 * HEAD_DIM]] * (D_B // (2 * HEAD_DIM))
                                         + [t[:, 2 * HEAD_DIM:]], axis=1)
        oq_ref[r, :] = (qk * wide(rc_ref[r, :]) + swapped * wide(rs_ref[r, :])).astype(BF16)
        ov_ref[:, r] = qkv[:, n:].T.astype(BF16)


def _in_proj(x2, g, w_bf, layer, rc, rs, seq, tm):
    n_tok = x2.shape[0]
    nrb = seq // tm
    return pl.pallas_call(
        _in_proj_kernel,
        grid=(n_tok // tm,),
        in_specs=[
            pl.BlockSpec((tm, D_MODEL), lambda i: (i, 0)),
            pl.BlockSpec((1, D_MODEL), lambda i: (0, 0)),
            pl.BlockSpec((None, D_MODEL, D_IN), lambda i: (layer, 0, 0)),
            pl.BlockSpec((tm, 4 * HEAD_DIM), lambda i: (i % nrb, 0)),
            pl.BlockSpec((tm, 4 * HEAD_DIM), lambda i: (i % nrb, 0)),
        ],
        out_specs=[
            pl.BlockSpec((tm, 2 * D_A), lambda i: (i, 0)),
            pl.BlockSpec((tm, D_B + D_KV), lambda i: (i, 0)),
            pl.BlockSpec((D_KV, tm), lambda i: (0, i)),
            pl.BlockSpec((tm, 3 * D_C), lambda i: (i, 0)),
        ],
        out_shape=[
            jax.ShapeDtypeStruct((n_tok, 2 * D_A), BF16),
            jax.ShapeDtypeStruct((n_tok, D_B + D_KV), BF16),
            jax.ShapeDtypeStruct((D_KV, n_tok), BF16),
            jax.ShapeDtypeStruct((n_tok, 3 * D_C), BF16),
        ],
        compiler_params=_cparams(("parallel",)),
        name="in_proj",
    )(x2, g, w_bf, rc, rs)


def _dwconv_chunk(src_ref, ci, nchunk, rows, w, b, pad_left):
    halo = BF16_ROWS
    r0 = pl.multiple_of(ci * rows, rows)
    seq = nchunk * rows
    top = src_ref[pl.ds(pl.multiple_of(jnp.maximum(r0 - halo, 0), halo), halo), :]
    bot = src_ref[pl.ds(pl.multiple_of(jnp.minimum(r0 + rows, seq - halo), halo), halo), :]
    top = jnp.where(ci > 0, top, jnp.zeros_like(top))
    bot = jnp.where(ci < nchunk - 1, bot, jnp.zeros_like(bot))
    win = jnp.concatenate([top, src_ref[pl.ds(r0, rows), :], bot], axis=0).astype(F32)
    return _conv_taps(win, rows, w, b, pad_left, halo)


def _conv_taps(win, rows, w, b, pad_left, halo):
    total = rows + 2 * halo
    acc = None
    for k in range(w.shape[0]):
        shift = (pad_left - k) % total
        tap = win if shift == 0 else pltpu.roll(win, shift, axis=0)
        term = tap[halo:halo + rows] * w[k:k + 1, :]
        acc = term if acc is None else acc + term
    return acc + b


def _dwconv_block(src_ref, j, nblk, w, b, pad_left):
    r0 = j * DFT_BLK
    halo = 2 * SUBLANES
    zeros = jnp.zeros((halo, src_ref.shape[1]), src_ref.dtype)
    top = zeros if j == 0 else src_ref[r0 - halo:r0, :]
    bot = zeros if j == nblk - 1 else src_ref[r0 + DFT_BLK:r0 + DFT_BLK + halo, :]
    win = jnp.concatenate([top, src_ref[r0:r0 + DFT_BLK, :], bot], axis=0).astype(F32)
    return _conv_taps(win, DFT_BLK, w, b, pad_left, halo)


def _softplus(x):
    return jnp.maximum(x, 0.0) + jnp.log1p(jnp.exp(-jnp.abs(x)))


def _gelu_tanh(x):
    c = math.sqrt(2.0 / math.pi)
    return x * (0.5 * (1.0 + jnp.tanh(c * (x + 0.044715 * (x * x * x)))))


def _lru_kernel(u_ref, gate_ref, cw_ref, cb_ref, w_ref, bias_ref, lam_ref, o_ref,
                xci, gg, af, bf, ar, br, pfs, hfs, prs, hrs, g0, g1, *, seq, rows, unroll):
    seg = seq // SCAN_SEGS
    tiles = rows // SCAN_SEGS
    nchunk = seq // rows
    per_seg = seg // rows
    cw = cw_ref[...]
    cb = cb_ref[...]
    w = w_ref[0] * 0.5
    bias = bias_ref[0] * 0.5
    hnsp = (-0.5 * C_LRU) * _softplus(-lam_ref[0])

    def conv(ci, carry):
        r0 = pl.multiple_of(ci * rows, rows)
        dst = pl.ds((ci % per_seg) * (rows * SCAN_SEGS) + ci // per_seg, rows, stride=SCAN_SEGS)
        xci[dst, :] = _dwconv_chunk(u_ref, ci, nchunk, rows, cw, cb, 2)
        gg[dst, :] = _gelu_tanh(gate_ref[pl.ds(r0, rows), :].astype(F32))
        return carry

    lax.fori_loop(0, nchunk, conv, 0)

    grows = g0.shape[0]
    ngate = seq // grows

    def gate_matmul(ci, g_ref):
        r0 = pl.multiple_of(ci * grows, grows)
        g_ref[...] = jnp.dot(xci[pl.ds(r0, grows), :].astype(BF16), w, preferred_element_type=F32)

    def gate_math(ci, g_ref):
        r0 = pl.multiple_of(ci * grows, grows)
        xc = xci[pl.ds(r0, grows), :]
        t = jnp.tanh(g_ref[...] + bias)
        hxc = 0.5 * xc
        for d, (a_ref, b_ref) in enumerate(((af, bf), (ar, br))):
            tr = t[:, (2 * d) * LANES:(2 * d + 1) * LANES]
            ti = t[:, (2 * d + 1) * LANES:(2 * d + 2) * LANES]
            hn = hnsp[:, d * LANES:(d + 1) * LANES]
            log_a = tr * hn + hn
            a = jnp.exp(log_a)
            nem = (-1.0 - a * a) * jnp.tanh(log_a)
            a_ref[pl.ds(r0, grows), :] = a
            root = jnp.where(nem > 0.0, nem * lax.rsqrt(nem), 0.0)
            b_ref[pl.ds(r0, grows), :] = root * (ti * hxc + hxc)

    def gates(k, carry):
        gate_matmul(2 * k + 1, g1)
        gate_math(2 * k, g0)
        gate_matmul(jnp.minimum(2 * k + 2, ngate - 1), g0)
        gate_math(2 * k + 1, g1)
        return carry

    gate_matmul(0, g0)
    lax.fori_loop(0, ngate // 2, gates, 0)

    def scan(it, carry):
        def two_steps(p, h, a_ref, b_ref, p_out, h_out, r0, r1):
            a0 = a_ref[pl.ds(r0, SCAN_SEGS), :]
            b0 = b_ref[pl.ds(r0, SCAN_SEGS), :]
            a1 = a_ref[pl.ds(r1, SCAN_SEGS), :]
            a01 = a1 * a0
            b01 = a1 * b0 + b_ref[pl.ds(r1, SCAN_SEGS), :]
            p_out[pl.ds(r0, SCAN_SEGS), :] = a0 * p
            h_out[pl.ds(r0, SCAN_SEGS), :] = a0 * h + b0
            p = a01 * p
            h = a01 * h + b01
            p_out[pl.ds(r1, SCAN_SEGS), :] = p
            h_out[pl.ds(r1, SCAN_SEGS), :] = h
            return p, h

        pf, hf, pr, hr = carry
        for u in range(0, unroll, 2):
            i = it * unroll + u
            row = lambda pos: pl.multiple_of(pos * SCAN_SEGS, SCAN_SEGS)
            pf, hf = two_steps(pf, hf, af, bf, pfs, hfs, row(i), row(i + 1))
            pr, hr = two_steps(pr, hr, ar, br, prs, hrs, row(seg - 1 - i), row(seg - 2 - i))
        return pf, hf, pr, hr

    one = jnp.ones((SCAN_SEGS, LANES), F32)
    zero = jnp.zeros((SCAN_SEGS, LANES), F32)
    pf, hf, pr, hr = lax.fori_loop(0, seg // unroll, scan, (one, zero, one, zero))

    sub = lax.broadcasted_iota(jnp.int32, (SCAN_SEGS, LANES), 0)
    cf = zero
    cr = zero
    for _ in range(SCAN_SEGS - 1):
        cf = jnp.where(sub == 0, 0.0, pltpu.roll(hf + pf * cf, 1, axis=0))
        cr = jnp.where(sub == SCAN_SEGS - 1, 0.0, pltpu.roll(hr + pr * cr, SCAN_SEGS - 1, axis=0))

    def combine(ci, carry):
        r0 = pl.multiple_of(ci * rows, rows)
        sl = pl.ds(r0, rows)
        tile3 = lambda ref: ref[sl, :].reshape(tiles, SCAN_SEGS, LANES)
        h = (tile3(hfs) + tile3(pfs) * cf[None]) + (tile3(hrs) + tile3(prs) * cr[None])
        out = h.reshape(rows, LANES) * gg[sl, :]
        for k in range(tiles):
            o_ref[pl.ds(ci * tiles + k, SCAN_SEGS, stride=seg), :] = (
                out[k * SCAN_SEGS:(k + 1) * SCAN_SEGS])
        return carry

    lax.fori_loop(0, nchunk, combine, 0)


def _lru(pa, cw, cb, wg, bias, lam, batch, seq, rows=128, gate_rows=512, unroll=16):
    n_tok = pa.shape[0]
    ng = D_A // LANES
    kern = functools.partial(_lru_kernel, seq=seq, rows=rows, unroll=unroll)
    return pl.pallas_call(
        kern,
        grid=(batch, ng),
        in_specs=[
            pl.BlockSpec((seq, LANES), lambda b, g: (b, g)),
            pl.BlockSpec((seq, LANES), lambda b, g: (b, ng + g)),
            pl.BlockSpec((cw.shape[0], LANES), lambda b, g: (0, g)),
            pl.BlockSpec((1, LANES), lambda b, g: (0, g)),
            pl.BlockSpec((1, LANES, 4 * LANES), lambda b, g: (g, 0, 0)),
            pl.BlockSpec((1, 1, 4 * LANES), lambda b, g: (g, 0, 0)),
            pl.BlockSpec((1, 1, 2 * LANES), lambda b, g: (g, 0, 0)),
        ],
        out_specs=pl.BlockSpec((seq, LANES), lambda b, g: (b, g)),
        out_shape=jax.ShapeDtypeStruct((n_tok, D_A), F32),
        scratch_shapes=[pltpu.VMEM((seq, LANES), F32) for _ in range(10)]
        + [pltpu.VMEM((gate_rows, 4 * LANES), F32) for _ in range(2)],
        compiler_params=_cparams(("parallel", "parallel")),
        name="rglru",
    )(pa, pa, cw, cb, wg, bias, lam)


_ATTN_STRAIGHT = (0, 2, 3, 5)
_ATTN_ROLLED = (1, 4)
_ONES_ROWS = 16


def _attn_kernel(sink_ref, bias_ref, q_ref, k_ref, vt_ref, o_ref, s_scr, p_scr, e_scr, *, seq):
    band = 3 * BLOCK
    nblk = seq // BLOCK
    lo_q = lax.broadcasted_iota(jnp.int32, (BLOCK, 2 * HEAD_DIM), 1) < HEAD_DIM
    ones = jnp.ones((_ONES_ROWS, band), BF16)
    nt = (((1,), (1,)), ((), ()))
    stack = _ATTN_STRAIGHT + _ATTN_ROLLED
    ns = len(_ATTN_STRAIGHT) * BLOCK

    def window(j):
        q0 = j * BLOCK
        return q0, pl.multiple_of(jnp.clip(q0 - BLOCK, 0, seq - band), BLOCK)

    def scores(j, slot):
        q0, k0 = window(j)
        rows = pl.ds(pl.multiple_of(q0, BLOCK), BLOCK)
        kb = k_ref[pl.ds(k0, band), :]
        kbs = pltpu.roll(kb, HEAD_DIM, axis=1)

        def own_half(h):
            qt = q_ref[rows, (h // 2) * 2 * HEAD_DIM:(h // 2 + 1) * 2 * HEAD_DIM]
            return jnp.where(lo_q, qt, 0.0) if h % 2 == 0 else jnp.where(lo_q, 0.0, qt)

        qa = jnp.concatenate([own_half(h) for h in _ATTN_STRAIGHT], axis=0)
        qb = jnp.concatenate([own_half(h) for h in _ATTN_ROLLED], axis=0)
        s_scr[slot, :, :ns] = lax.dot_general(kb, qa, nt, preferred_element_type=F32)
        s_scr[slot, :, ns:] = lax.dot_general(kbs, qb, nt, preferred_element_type=F32)

    def softmax(j, slot):
        q0, k0 = window(j)
        bias = bias_ref[(q0 - k0) // BLOCK]
        for h in range(N_Q_HEADS):
            src = stack.index(h) * BLOCK
            s = s_scr[slot, :, src:src + BLOCK] + bias
            sk = sink_ref[h] * LOG2E
            m = jnp.maximum(jnp.max(s, axis=0, keepdims=True), sk)
            p_scr[slot, :, h * BLOCK:(h + 1) * BLOCK] = jnp.exp2(s - m).astype(BF16)
            e_scr[slot, :, h * BLOCK:(h + 1) * BLOCK] = jnp.broadcast_to(
                jnp.exp2(sk - m), (SUBLANES, BLOCK))

    def values(j, slot):
        q0, k0 = window(j)
        vt = vt_ref[:, pl.ds(k0, band)]
        outs = []
        for kv in range(N_KV_HEADS):
            cols = slice(kv * GROUP * BLOCK, (kv + 1) * GROUP * BLOCK)
            lhs = jnp.concatenate([vt[kv * HEAD_DIM:(kv + 1) * HEAD_DIM], ones], axis=0)
            ov = jnp.dot(lhs, p_scr[slot, :, cols], preferred_element_type=F32)
            res = ov[:HEAD_DIM] / (ov[HEAD_DIM:HEAD_DIM + 1] + e_scr[slot, 0:1, cols])
            outs += [res[:, g * BLOCK:(g + 1) * BLOCK] for g in range(GROUP)]
        rows = pl.ds(pl.multiple_of(q0, BLOCK), BLOCK)
        o_ref[rows, :] = jnp.concatenate(outs, axis=0).T

    scores(0, 0)
    softmax(0, 0)
    scores(1, 1)

    def step(jj, carry):
        j = 2 * jj
        values(j, 0)
        softmax(j + 1, 1)
        scores(j + 2, 0)
        values(j + 1, 1)
        softmax(j + 2, 0)
        scores(j + 3, 1)
        return carry

    lax.fori_loop(0, (nblk - 2) // 2, step, 0)
    values(nblk - 2, 0)
    softmax(nblk - 1, 1)
    values(nblk - 1, 1)


def _attn_bias_table():
    ki = np.arange(3 * BLOCK)[None, :, None]
    qi = np.arange(BLOCK)[None, None, :]
    off = (np.arange(3) * BLOCK)[:, None, None]
    return jnp.asarray(np.where(np.abs(off + qi - ki) <= WINDOW, 0.0, NEG).astype(np.float32))


def _attention(pq, vt, sink, bias, batch, seq):
    n_tok = pq.shape[0]
    band = 3 * BLOCK
    kern = functools.partial(_attn_kernel, seq=seq)
    return pl.pallas_call(
        kern,
        grid=(batch,),
        in_specs=[
            pl.BlockSpec(memory_space=pltpu.SMEM),
            pl.BlockSpec(bias.shape, lambda b: (0, 0, 0)),
            pl.BlockSpec((seq, D_B), lambda b: (b, 0)),
            pl.BlockSpec((seq, D_KV), lambda b: (b, D_B // D_KV)),
            pl.BlockSpec((D_KV, seq), lambda b: (0, b)),
        ],
        out_specs=pl.BlockSpec((seq, D_B), lambda b: (b, 0)),
        out_shape=jax.ShapeDtypeStruct((n_tok, D_B), F32),
        scratch_shapes=[pltpu.VMEM((2, band, N_Q_HEADS * BLOCK), F32),
                        pltpu.VMEM((2, band, N_Q_HEADS * BLOCK), BF16),
                        pltpu.VMEM((2, SUBLANES, N_Q_HEADS * BLOCK), F32)],
        compiler_params=_cparams(("parallel",)),
        name="win_attn",
    )(sink, bias, pq, pq, vt)


def _hy_filter_kernel(z_ref, w1_ref, b1_ref, fr_ref, w2_ref, b2_ref, w3_ref, dec_ref, o_ref):
    hi = lax.Precision.HIGHEST
    fr = fr_ref[...]
    h = jnp.sin(fr * (jnp.dot(z_ref[...], w1_ref[...], preferred_element_type=F32, precision=hi)
                      + b1_ref[...]))
    h = jnp.sin(fr * (jnp.dot(h, w2_ref[...], preferred_element_type=F32, precision=hi) + b2_ref[...]))
    f = jnp.dot(h, w3_ref[...], preferred_element_type=F32, precision=hi) * dec_ref[...]
    for half in range(2):
        for d in range(2):
            c0 = (2 * half + d) * D_C
            o_ref[d, half] = f[:, c0:c0 + D_C]


def _hy_filter(zemb2, w1, b1, fr, w2, b2, w3, dec4, rows=512):
    n = zemb2.shape[0]
    full = lambda a: pl.BlockSpec(a.shape, lambda i: (0,) * a.ndim)
    return pl.pallas_call(
        _hy_filter_kernel,
        grid=(n // rows,),
        in_specs=[pl.BlockSpec((rows, zemb2.shape[1]), lambda i: (i, 0)),
                  full(w1), full(b1), full(fr), full(w2), full(b2), full(w3),
                  pl.BlockSpec((rows, 4 * D_C), lambda i: (i, 0))],
        out_specs=pl.BlockSpec((2, 2, rows, D_C), lambda i: (0, 0, i, 0)),
        out_shape=jax.ShapeDtypeStruct((2, 2, n, D_C), F32),
        compiler_params=_cparams(("parallel",)),
        name="hyena_filter",
    )(zemb2, w1, b1, fr, w2, b2, w3, dec4)


def _dft_outer_fwd(zs, ka_ref, o_ref):
    nj, _, lanes = zs.shape
    pair = BF16_ROWS // DFT_SUB
    for p in range(DFT_BLK // BF16_ROWS):
        parts = []
        for s in range(pair * p, pair * (p + 1)):
            xg = zs[:, s * DFT_SUB:(s + 1) * DFT_SUB, :].reshape(nj * DFT_SUB, lanes).astype(BF16)
            c = jnp.dot(ka_ref[...], xg, preferred_element_type=F32)
            parts.append(c.reshape(2, DFT_NBLK, DFT_SUB, lanes))
        o_ref[0, :, :, p * BF16_ROWS:(p + 1) * BF16_ROWS, :] = (
            jnp.concatenate(parts, axis=2).astype(BF16))


def _hy_fwd_data_kernel(x1_ref, v_ref, cw_ref, cb_ref, ka_ref, o_ref, zs, *, seq):
    nj = seq // DFT_BLK
    cw = cw_ref[...]
    cb = cb_ref[...]
    for j in range(nj):
        x1 = _dwconv_block(x1_ref, j, nj, cw[:, D_C:2 * D_C], cb[:, D_C:2 * D_C], 1)
        v = _dwconv_block(v_ref, j, nj, cw[:, 2 * D_C:], cb[:, 2 * D_C:], 1)
        zs[j] = v * x1
    _dft_outer_fwd(zs, ka_ref, o_ref)


def _hy_fwd_filt_kernel(f_ref, ka_ref, o_ref):
    _dft_outer_fwd(f_ref.at[0], ka_ref, o_ref)


def _hy_fwd_data(pc, cw, cb, ka, batch, seq):
    nj = seq // DFT_BLK
    kern = functools.partial(_hy_fwd_data_kernel, seq=seq)
    return pl.pallas_call(
        kern,
        grid=(batch,),
        in_specs=[
            pl.BlockSpec((seq, D_C), lambda b: (b, 1)),
            pl.BlockSpec((seq, D_C), lambda b: (b, 2)),
            pl.BlockSpec(cw.shape, lambda b: (0, 0)),
            pl.BlockSpec(cb.shape, lambda b: (0, 0)),
            pl.BlockSpec(ka.shape, lambda b: (0, 0)),
        ],
        out_specs=pl.BlockSpec((1, 2, DFT_NBLK, DFT_BLK, D_C), lambda b: (b, 0, 0, 0, 0)),
        out_shape=jax.ShapeDtypeStruct((batch, 2, DFT_NBLK, DFT_BLK, D_C), BF16),
        scratch_shapes=[pltpu.VMEM((nj, DFT_BLK, D_C), F32)],
        compiler_params=_cparams(("parallel",)),
        name="hyena_dft_outer",
    )(pc, pc, cw, cb, ka)


def _hy_fwd_filt(filt4, ka):
    ndir, nj = filt4.shape[0], filt4.shape[1]
    return pl.pallas_call(
        _hy_fwd_filt_kernel,
        grid=(ndir,),
        in_specs=[
            pl.BlockSpec((1, nj, DFT_BLK, D_C), lambda b: (b, 0, 0, 0)),
            pl.BlockSpec(ka.shape, lambda b: (0, 0)),
        ],
        out_specs=pl.BlockSpec((1, 2, DFT_NBLK, DFT_BLK, D_C), lambda b: (b, 0, 0, 0, 0)),
        out_shape=jax.ShapeDtypeStruct((ndir, 2, DFT_NBLK, DFT_BLK, D_C), BF16),
        compiler_params=_cparams(("parallel",)),
        name="hyena_dft_outer_filter",
    )(filt4, ka)


def _hy_inner_kernel(c_ref, f_ref, hbias_ref, g_ref, gi_ref, o_ref, *, batch, kper):
    hbias = hbias_ref[...]
    for q in range(kper):
        g = g_ref[q]
        gi = gi_ref[q]
        hf = jnp.dot(g, f_ref[0, :, q].reshape(2 * DFT_BLK, D_C), preferred_element_type=F32)
        hb = jnp.dot(g, f_ref[1, :, q].reshape(2 * DFT_BLK, D_C), preferred_element_type=F32)
        hre = hf[:DFT_BLK] + hb[:DFT_BLK] + hbias
        him = hf[DFT_BLK:] - hb[DFT_BLK:]
        for b in range(batch):
            x = jnp.dot(g, c_ref[b, :, q].reshape(2 * DFT_BLK, D_C), preferred_element_type=F32)
            xre, xim = x[:DFT_BLK], x[DFT_BLK:]
            y = jnp.concatenate([xre * hre - xim * him, xre * him + xim * hre], axis=0)
            d = jnp.dot(gi, y.astype(BF16), preferred_element_type=F32)
            o_ref[b, :, q] = d.reshape(2, DFT_BLK, D_C).astype(BF16)


def _hy_inner(cdata, cfilt, hbias, g, gi, kper=8):
    batch = cdata.shape[0]
    kern = functools.partial(_hy_inner_kernel, batch=batch, kper=kper)
    blk = lambda nb: pl.BlockSpec((nb, 2, kper, DFT_BLK, D_C), lambda k: (0, 0, k, 0, 0))
    return pl.pallas_call(
        kern,
        grid=(DFT_NBLK // kper,),
        in_specs=[blk(batch), blk(cfilt.shape[0]), pl.BlockSpec(hbias.shape, lambda k: (0, 0)),
                  pl.BlockSpec((kper, 2 * DFT_BLK, 2 * DFT_BLK), lambda k: (k, 0, 0)),
                  pl.BlockSpec((kper, 2 * DFT_BLK, 2 * DFT_BLK), lambda k: (k, 0, 0))],
        out_specs=blk(batch),
        out_shape=jax.ShapeDtypeStruct(cdata.shape, BF16),
        compiler_params=_cparams(("parallel",)),
        name="hyena_dft_inner",
    )(cdata, cfilt, hbias, g, gi)


def _hy_out_kernel(d_ref, x0_ref, cw_ref, cb_ref, kai_ref, o_ref, ys, *, seq):
    nj = seq // DFT_BLK
    pair = BF16_ROWS // DFT_SUB
    for p in range(DFT_BLK // BF16_ROWS):
        d = d_ref[0, :, :, p * BF16_ROWS:(p + 1) * BF16_ROWS, :].astype(F32)
        for s in range(pair):
            rhs = d[:, :, s * DFT_SUB:(s + 1) * DFT_SUB, :].reshape(2 * DFT_NBLK * DFT_SUB, D_C)
            y = jnp.dot(kai_ref[...], rhs.astype(BF16), preferred_element_type=F32)
            r0 = p * BF16_ROWS + s * DFT_SUB
            ys[:, r0:r0 + DFT_SUB, :] = y.reshape(nj, DFT_SUB, D_C)
    cw = cw_ref[...]
    cb = cb_ref[...]
    for j in range(nj):
        x0 = _dwconv_block(x0_ref, j, nj, cw[:, :D_C], cb[:, :D_C], 1)
        o_ref[j * DFT_BLK:(j + 1) * DFT_BLK, :] = ys[j] * x0


def _hy_out(dd, pc, cw, cb, kai, batch, seq):
    n_tok = pc.shape[0]
    nj = seq // DFT_BLK
    kern = functools.partial(_hy_out_kernel, seq=seq)
    whole = lambda a: pl.BlockSpec(a.shape, lambda b: (0, 0))
    return pl.pallas_call(
        kern,
        grid=(batch,),
        in_specs=[
            pl.BlockSpec((1, 2, DFT_NBLK, DFT_BLK, D_C), lambda b: (b, 0, 0, 0, 0)),
            pl.BlockSpec((seq, D_C), lambda b: (b, 0)),
            whole(cw), whole(cb), whole(kai),
        ],
        out_specs=pl.BlockSpec((seq, D_C), lambda b: (b, 0)),
        out_shape=jax.ShapeDtypeStruct((n_tok, D_C), F32),
        scratch_shapes=[pltpu.VMEM((nj, DFT_BLK, D_C), F32)],
        compiler_params=_cparams(("parallel",)),
        name="hyena_out",
    )(dd, pc, cw, cb, kai)


def _mix_mlp_kernel(ya_ref, yb_ref, yc_ref, x_ref, ga_ref, gb_ref, gc_ref, wo_ref, g_ref, wu_ref,
                    wd_ref, gf_ref, o_ref, *, ff_chunk, final_norm):
    y = jnp.concatenate([_rms(ya_ref[...], ga_ref[...]), _rms(yb_ref[...], gb_ref[...]),
                         _rms(yc_ref[...], gc_ref[...])], axis=-1).astype(BF16)
    x = x_ref[...] + jnp.dot(y, wo_ref[...], preferred_element_type=F32)
    h = _rms(x, g_ref[...]).astype(BF16)
    acc = x
    for c in range(D_FF // ff_chunk):
        sl = slice(c * ff_chunk, (c + 1) * ff_chunk)
        u = jnp.maximum(jnp.dot(h, wu_ref[:, sl], preferred_element_type=F32), 0.0)
        acc = acc + jnp.dot((u * u).astype(BF16), wd_ref[sl, :], preferred_element_type=F32)
    if final_norm:
        acc = _rms(acc, gf_ref[...])
    o_ref[...] = acc


def _mix_mlp(ya, yb, yc, x2, ga, gb, gc, wo_bf, g, wu_bf, wd_bf, gf, layer, tm, final_norm,
             ff_chunk=1024):
    n_tok = x2.shape[0]
    kern = functools.partial(_mix_mlp_kernel, ff_chunk=ff_chunk, final_norm=final_norm)
    row = lambda w: pl.BlockSpec((tm, w), lambda i: (i, 0))
    vec = lambda w: pl.BlockSpec((1, w), lambda i: (0, 0))
    resident = lambda r, c: pl.BlockSpec((None, r, c), lambda i: (layer, 0, 0),
                                         pipeline_mode=pl.Buffered(1))
    return pl.pallas_call(
        kern,
        grid=(n_tok // tm,),
        in_specs=[row(D_A), row(D_B), row(D_C), row(D_MODEL), vec(D_A), vec(D_B), vec(D_C),
                  resident(D_MODEL, D_MODEL), vec(D_MODEL), resident(D_MODEL, D_FF),
                  resident(D_FF, D_MODEL), vec(D_MODEL)],
        out_specs=row(D_MODEL),
        out_shape=jax.ShapeDtypeStruct((n_tok, D_MODEL), F32),
        compiler_params=_cparams(("parallel",)),
        name="mix_mlp",
    )(ya, yb, yc, x2, ga, gb, gc, wo_bf, g, wu_bf, wd_bf, gf)


def _rope_tables(seq):
    pos = np.arange(seq, dtype=np.float32)
    inv_freq = (np.float32(ROPE_THETA) ** (-np.arange(0, ROT_DIM, 2, dtype=np.float32) / ROT_DIM))
    ang = (pos[:, None] * inv_freq[None, :]).astype(np.float32)
    cos, sin = np.cos(ang).astype(np.float32), np.sin(ang).astype(np.float32)
    rest = HEAD_DIM - ROT_DIM
    c_head = np.concatenate([cos, cos, np.ones((seq, rest), np.float32)], axis=1)
    s_head = np.concatenate([-sin, sin, np.zeros((seq, rest), np.float32)], axis=1)
    scale = np.float32(HEAD_DIM ** -0.5 * LOG2E)
    pair = lambda t: np.tile(t, (1, 2))
    c = np.concatenate([pair(c_head) * scale, pair(c_head)], axis=1)
    s = np.concatenate([pair(s_head) * scale, pair(s_head)], axis=1)
    return jnp.asarray(c), jnp.asarray(s)


def _hyena_position_tables(seq):
    t = np.linspace(0.0, 1.0, seq, dtype=np.float32)[:, None]
    w = (2.0 * math.pi * np.arange(seq, dtype=np.float32)[:, None] / seq).astype(np.float32)
    f = np.linspace(1e-4, HY_BANDS - 1, HY_BANDS, dtype=np.float32)[None, :]
    fw = (f * w).astype(np.float32)
    z = np.concatenate([t, np.cos(fw), -np.sin(fw)], axis=-1).astype(np.float32)
    deltas = np.abs(np.linspace(HY_MIN_DECAY, HY_MAX_DECAY, D_C, dtype=np.float32))
    decay = np.exp(-t * deltas[None, :]).astype(np.float32)
    decay_b = decay.copy()
    decay_b[0] = 0.0
    half = seq // 2
    zp = np.zeros((half, 2, HY_WIDTH), np.float32)
    zp[:, 0, :HY_EMB] = z[:half]
    zp[:, 1, :HY_EMB] = z[half:]
    dec4 = np.concatenate([decay[:half], decay_b[:half], decay[half:], decay_b[half:]], axis=1)
    return jnp.asarray(zp.reshape(half, 2 * HY_WIDTH)), jnp.asarray(dec4)


def _blockdiag2(a):
    z = jnp.zeros_like(a)
    return jnp.concatenate([jnp.concatenate([a, z], axis=1), jnp.concatenate([z, a], axis=1)], axis=0)


def _dft_tables(seq):
    n = 2 * seq
    nj = seq // DFT_BLK
    kk = np.arange(DFT_NBLK)
    ang = 2.0 * np.pi * np.outer(kk, np.arange(nj)) / DFT_NBLK
    eye = np.eye(DFT_SUB)
    ka = np.concatenate([np.kron(np.cos(ang), eye), np.kron(-np.sin(ang), eye)], axis=0)
    kai = np.concatenate([np.kron(np.cos(ang).T, eye), np.kron(-np.sin(ang).T, eye)], axis=1) / n
    m = np.arange(DFT_BLK)
    k = kk[:, None, None] + DFT_NBLK * np.arange(DFT_BLK)[None, :, None]
    ph = 2.0 * np.pi * ((k * m[None, None, :]) % n) / n
    gre, gim = np.cos(ph), -np.sin(ph)
    g = np.concatenate([np.concatenate([gre, -gim], axis=2), np.concatenate([gim, gre], axis=2)], axis=1)
    gi = np.transpose(g, (0, 2, 1))
    as_bf = lambda a: jnp.asarray(a.astype(np.float32)).astype(BF16)
    return as_bf(ka), as_bf(kai), as_bf(g), as_bf(gi)


def _lru_blockdiag(w):
    nb = w.shape[1] // 2
    w = w.reshape(2, nb, 2, w.shape[2], w.shape[3])
    z = jnp.zeros_like(w[:, :, 0])
    top = jnp.concatenate([w[:, :, 0], z], axis=-1)
    bot = jnp.concatenate([z, w[:, :, 1]], axis=-1)
    return jnp.concatenate([top, bot], axis=-2)


def kernel(x, norm_mix_g, w_in, conv_a_w, conv_a_b, lru_wa, lru_ba, lru_wx, lru_bx, lru_lambda,
           attn_sink, hy_conv_w, hy_conv_b, hy_w1, hy_b1, hy_freq, hy_w2, hy_b2, hy_w3, hy_bias,
           gnorm_a, gnorm_b, gnorm_c, w_out, norm_mlp_g, w_up, w_down, final_norm_g):
    batch, seq, _ = x.shape
    depth = w_in.shape[0]
    n_tok = batch * seq
    tm = 1024
    ng = D_A // LANES

    rc, rs = _rope_tables(seq)
    attn_bias = _attn_bias_table()
    zemb, dec4 = _hyena_position_tables(seq)
    ka, kai, gtab, gitab = _dft_tables(seq)

    w_in_bf, w_out_bf = w_in.astype(BF16), w_out.astype(BF16)
    w_up_bf, w_down_bf = w_up.astype(BF16), w_down.astype(BF16)

    xs = x.reshape(n_tok, D_MODEL)
    for i in range(depth):
        pa, pq, vt, pc = _in_proj(xs, norm_mix_g[i][None], w_in_bf, i, rc, rs, seq, tm)

        wa, wx = _lru_blockdiag(lru_wa[i]), _lru_blockdiag(lru_wx[i])
        wg = jnp.concatenate([wa[0], wx[0], wa[1], wx[1]], axis=-1).astype(BF16)
        tile = lambda v: v.reshape(ng, 1, LANES)
        bias = jnp.concatenate([tile(lru_ba[i][0]), tile(lru_bx[i][0]),
                                tile(lru_ba[i][1]), tile(lru_bx[i][1])], axis=-1)
        lam = jnp.concatenate([tile(lru_lambda[i][0]), tile(lru_lambda[i][1])], axis=-1)
        y_a = _lru(pa, conv_a_w[i], conv_a_b[i][None], wg, bias, lam, batch, seq)

        y_b = _attention(pq, vt, attn_sink[i], attn_bias, batch, seq)

        w1p = jnp.zeros((HY_WIDTH, HY_WIDTH), F32).at[:HY_EMB].set(hy_w1[i])
        pair = lambda v: jnp.concatenate([v, v])[None]
        filt = _hy_filter(zemb, _blockdiag2(w1p), pair(hy_b1[i]), pair(hy_freq[i]),
                          _blockdiag2(hy_w2[i]), pair(hy_b2[i]), _blockdiag2(hy_w3[i]), dec4)
        cw, cb = hy_conv_w[i], hy_conv_b[i][None]
        cdata = _hy_fwd_data(pc, cw, cb, ka, batch, seq)
        cfilt = _hy_fwd_filt(filt.reshape(2, seq // DFT_BLK, DFT_BLK, D_C), ka)
        dd = _hy_inner(cdata, cfilt, hy_bias[i][None], gtab, gitab)
        y_c = _hy_out(dd, pc, cw, cb, kai, batch, seq)

        xs = _mix_mlp(y_a, y_b, y_c, xs, gnorm_a[i][None], gnorm_b[i][None], gnorm_c[i][None],
                      w_out_bf, norm_mlp_g[i][None], w_up_bf, w_down_bf, final_norm_g[None], i, tm,
                      final_norm=(i == depth - 1))
    return xs.reshape(batch, seq, D_MODEL)
```

```python
import functools
import math

import numpy as np
import jax
import jax.numpy as jnp
from jax import lax
from jax.experimental import pallas as pl
from jax.experimental.pallas import tpu as pltpu

F32 = jnp.float32
BF16 = jnp.bfloat16

D_MODEL = 1024
D_A = 384
D_B = 384
D_C = 256
HEAD_DIM = 64
N_Q_HEADS = 6
N_KV_HEADS = 2
GROUP = N_Q_HEADS // N_KV_HEADS
D_KV = N_KV_HEADS * HEAD_DIM
D_QKV = D_B + 2 * D_KV
D_IN = 2 * D_A + D_QKV + 3 * D_C
C_LRU = 8.0
WINDOW = 128
BLOCK = 128
ROPE_THETA = 500000.0
ROT_DIM = HEAD_DIM // 4
HY_EMB = 33
HY_BANDS = (HY_EMB - 1) // 2
HY_WIDTH = 64
HY_TARGET = 1e-2
HY_MAX_DECAY = math.log(HY_TARGET) / 0.3
HY_MIN_DECAY = math.log(HY_TARGET) / 1.5
D_FF = 4 * D_MODEL
EPS = 1e-6
NEG = -1e30
LOG2E = math.log2(math.e)

LANES = 128
SUBLANES = 8
VMEM_LIMIT = 56 * 1024 * 1024

SCAN_SEGS = SUBLANES
DFT_BLK = 128
DFT_NBLK = 64
DFT_SUB = 8
BF16_ROWS = 16
IN_PROJ_CHUNKS = 2


def _cparams(sem):
    return pltpu.CompilerParams(dimension_semantics=sem, vmem_limit_bytes=VMEM_LIMIT)


def _rms(x, g):
    return x * lax.rsqrt(jnp.mean(x * x, axis=-1, keepdims=True) + EPS) * g


def _in_proj_kernel(x_ref, g_ref, w_ref, rc_ref, rs_ref, oa_ref, oq_ref, ov_ref, oc_ref):
    n = D_B + D_KV
    half = ROT_DIM // 2
    tm = x_ref.shape[0]
    for c in range(IN_PROJ_CHUNKS):
        r = slice(c * tm // IN_PROJ_CHUNKS, (c + 1) * tm // IN_PROJ_CHUNKS)
        h = _rms(x_ref[r, :], g_ref[...]).astype(BF16)
        oa_ref[r, :] = jnp.dot(h, w_ref[:, :2 * D_A], preferred_element_type=F32).astype(BF16)
        oc_ref[r, :] = jnp.dot(h, w_ref[:, 2 * D_A + D_QKV:], preferred_element_type=F32).astype(BF16)
        qkv = jnp.dot(h, w_ref[:, 2 * D_A:2 * D_A + D_QKV], preferred_element_type=F32)
        qk = qkv[:, :n]
        lane = lax.broadcasted_iota(jnp.int32, qk.shape, 1) % HEAD_DIM
        swapped = jnp.where(lane < half, pltpu.roll(qk, n - half, axis=1), pltpu.roll(qk, half, axis=1))
        wide = lambda t: jnp.concatenate([t[:, :2 * HEAD_DIM]] * (D_B // (2 * HEAD_DIM))
                                         + [t[:, 2 * HEAD_DIM:]], axis=1)
        oq_ref[r, :] = (qk * wide(rc_ref[r, :]) + swapped * wide(rs_ref[r, :])).astype(BF16)
        ov_ref[:, r] = qkv[:, n:].T.astype(BF16)


def _in_proj(x2, g, w_bf, layer, rc, rs, seq, tm):
    n_tok = x2.shape[0]
    nrb = seq // tm
    return pl.pallas_call(
        _in_proj_kernel,
        grid=(n_tok // tm,),
        in_specs=[
            pl.BlockSpec((tm, D_MODEL), lambda i: (i, 0)),
            pl.BlockSpec((1, D_MODEL), lambda i: (0, 0)),
            pl.BlockSpec((None, D_MODEL, D_IN), lambda i: (layer, 0, 0)),
            pl.BlockSpec((tm, 4 * HEAD_DIM), lambda i: (i % nrb, 0)),
            pl.BlockSpec((tm, 4 * HEAD_DIM), lambda i: (i % nrb, 0)),
        ],
        out_specs=[
            pl.BlockSpec((tm, 2 * D_A), lambda i: (i, 0)),
            pl.BlockSpec((tm, D_B + D_KV), lambda i: (i, 0)),
            pl.BlockSpec((D_KV, tm), lambda i: (0, i)),
            pl.BlockSpec((tm, 3 * D_C), lambda i: (i, 0)),
        ],
        out_shape=[
            jax.ShapeDtypeStruct((n_tok, 2 * D_A), BF16),
            jax.ShapeDtypeStruct((n_tok, D_B + D_KV), BF16),
            jax.ShapeDtypeStruct((D_KV, n_tok), BF16),
            jax.ShapeDtypeStruct((n_tok, 3 * D_C), BF16),
        ],
        compiler_params=_cparams(("parallel",)),
        name="in_proj",
    )(x2, g, w_bf, rc, rs)


def _dwconv_chunk(src_ref, ci, nchunk, rows, w, b, pad_left):
    halo = BF16_ROWS
    r0 = pl.multiple_of(ci * rows, rows)
    seq = nchunk * rows
    top = src_ref[pl.ds(pl.multiple_of(jnp.maximum(r0 - halo, 0), halo), halo), :]
    bot = src_ref[pl.ds(pl.multiple_of(jnp.minimum(r0 + rows, seq - halo), halo), halo), :]
    top = jnp.where(ci > 0, top, jnp.zeros_like(top))
    bot = jnp.where(ci < nchunk - 1, bot, jnp.zeros_like(bot))
    win = jnp.concatenate([top, src_ref[pl.ds(r0, rows), :], bot], axis=0).astype(F32)
    return _conv_taps(win, rows, w, b, pad_left, halo)


def _conv_taps(win, rows, w, b, pad_left, halo):
    total = rows + 2 * halo
    acc = None
    for k in range(w.shape[0]):
        shift = (pad_left - k) % total
        tap = win if shift == 0 else pltpu.roll(win, shift, axis=0)
        term = tap[halo:halo + rows] * w[k:k + 1, :]
        acc = term if acc is None else acc + term
    return acc + b


def _dwconv_block(src_ref, j, nblk, w, b, pad_left):
    r0 = j * DFT_BLK
    halo = 2 * SUBLANES
    zeros = jnp.zeros((halo, src_ref.shape[1]), src_ref.dtype)
    top = zeros if j == 0 else src_ref[r0 - halo:r0, :]
    bot = zeros if j == nblk - 1 else src_ref[r0 + DFT_BLK:r0 + DFT_BLK + halo, :]
    win = jnp.concatenate([top, src_ref[r0:r0 + DFT_BLK, :], bot], axis=0).astype(F32)
    return _conv_taps(win, DFT_BLK, w, b, pad_left, halo)


def _softplus(x):
    return jnp.maximum(x, 0.0) + jnp.log1p(jnp.exp(-jnp.abs(x)))


def _gelu_tanh(x):
    c = math.sqrt(2.0 / math.pi)
    return x * (0.5 * (1.0 + jnp.tanh(c * (x + 0.044715 * (x * x * x)))))


def _lru_kernel(u_ref, gate_ref, cw_ref, cb_ref, w_ref, bias_ref, lam_ref, o_ref,
                xci, gg, af, bf, ar, br, pfs, hfs, prs, hrs, g0, g1, *, seq, rows, unroll):
    seg = seq // SCAN_SEGS
    tiles = rows // SCAN_SEGS
    nchunk = seq // rows
    per_seg = seg // rows
    cw = cw_ref[...]
    cb = cb_ref[...]
    w = w_ref[0] * 0.5
    bias = bias_ref[0] * 0.5
    hnsp = (-0.5 * C_LRU) * _softplus(-lam_ref[0])

    def conv(ci, carry):
        r0 = pl.multiple_of(ci * rows, rows)
        dst = pl.ds((ci % per_seg) * (rows * SCAN_SEGS) + ci // per_seg, rows, stride=SCAN_SEGS)
        xci[dst, :] = _dwconv_chunk(u_ref, ci, nchunk, rows, cw, cb, 2)
        gg[dst, :] = _gelu_tanh(gate_ref[pl.ds(r0, rows), :].astype(F32))
        return carry

    lax.fori_loop(0, nchunk, conv, 0)

    grows = g0.shape[0]
    ngate = seq // grows

    def gate_matmul(ci, g_ref):
        r0 = pl.multiple_of(ci * grows, grows)
        g_ref[...] = jnp.dot(xci[pl.ds(r0, grows), :].astype(BF16), w, preferred_element_type=F32)

    def gate_math(ci, g_ref):
        r0 = pl.multiple_of(ci * grows, grows)
        xc = xci[pl.ds(r0, grows), :]
        t = jnp.tanh(g_ref[...] + bias)
        hxc = 0.5 * xc
        for d, (a_ref, b_ref) in enumerate(((af, bf), (ar, br))):
            tr = t[:, (2 * d) * LANES:(2 * d + 1) * LANES]
            ti = t[:, (2 * d + 1) * LANES:(2 * d + 2) * LANES]
            hn = hnsp[:, d * LANES:(d + 1) * LANES]
            log_a = tr * hn + hn
            a = jnp.exp(log_a)
            nem = (-1.0 - a * a) * jnp.tanh(log_a)
            a_ref[pl.ds(r0, grows), :] = a
            root = jnp.where(nem > 0.0, nem * lax.rsqrt(nem), 0.0)
            b_ref[pl.ds(r0, grows), :] = root * (ti * hxc + hxc)

    def gates(k, carry):
        gate_matmul(2 * k + 1, g1)
        gate_math(2 * k, g0)
        gate_matmul(jnp.minimum(2 * k + 2, ngate - 1), g0)
        gate_math(2 * k + 1, g1)
        return carry

    gate_matmul(0, g0)
    lax.fori_loop(0, ngate // 2, gates, 0)

    def scan(it, carry):
        span = unroll * SCAN_SEGS

        def run(p, h, a_ref, b_ref, p_out, h_out, base, order):
            rows = pl.ds(pl.multiple_of(base, span), span)
            a_blk, b_blk = a_ref[rows, :], b_ref[rows, :]
            tile = lambda blk, u: blk[u * SCAN_SEGS:(u + 1) * SCAN_SEGS]
            ps, hs = [None] * unroll, [None] * unroll
            for u0, u1 in zip(order[0::2], order[1::2]):
                a0, b0, a1, b1 = tile(a_blk, u0), tile(b_blk, u0), tile(a_blk, u1), tile(b_blk, u1)
                a01 = a1 * a0
                b01 = a1 * b0 + b1
                ps[u0], hs[u0] = a0 * p, a0 * h + b0
                p = a01 * p
                h = a01 * h + b01
                ps[u1], hs[u1] = p, h
            p_out[rows, :] = jnp.concatenate(ps, axis=0)
            h_out[rows, :] = jnp.concatenate(hs, axis=0)
            return p, h

        pf, hf, pr, hr = carry
        fwd = list(range(unroll))
        pf, hf = run(pf, hf, af, bf, pfs, hfs, it * span, fwd)
        pr, hr = run(pr, hr, ar, br, prs, hrs, (seg // unroll - 1 - it) * span, fwd[::-1])
        return pf, hf, pr, hr

    one = jnp.ones((SCAN_SEGS, LANES), F32)
    zero = jnp.zeros((SCAN_SEGS, LANES), F32)
    pf, hf, pr, hr = lax.fori_loop(0, seg // unroll, scan, (one, zero, one, zero))

    sub = lax.broadcasted_iota(jnp.int32, (SCAN_SEGS, LANES), 0)
    cf = zero
    cr = zero
    for _ in range(SCAN_SEGS - 1):
        cf = jnp.where(sub == 0, 0.0, pltpu.roll(hf + pf * cf, 1, axis=0))
        cr = jnp.where(sub == SCAN_SEGS - 1, 0.0, pltpu.roll(hr + pr * cr, SCAN_SEGS - 1, axis=0))

    def combine(ci, carry):
        r0 = pl.multiple_of(ci * rows, rows)
        sl = pl.ds(r0, rows)
        tile3 = lambda ref: ref[sl, :].reshape(tiles, SCAN_SEGS, LANES)
        h = (tile3(hfs) + tile3(pfs) * cf[None]) + (tile3(hrs) + tile3(prs) * cr[None])
        out = h.reshape(rows, LANES) * gg[sl, :]
        for k in range(tiles):
            o_ref[pl.ds(ci * tiles + k, SCAN_SEGS, stride=seg), :] = (
                out[k * SCAN_SEGS:(k + 1) * SCAN_SEGS])
        return carry

    lax.fori_loop(0, nchunk, combine, 0)


def _lru(pa, cw, cb, wg, bias, lam, batch, seq, rows=128, gate_rows=512, unroll=16):
    n_tok = pa.shape[0]
    ng = D_A // LANES
    kern = functools.partial(_lru_kernel, seq=seq, rows=rows, unroll=unroll)
    return pl.pallas_call(
        kern,
        grid=(batch, ng),
        in_specs=[
            pl.BlockSpec((seq, LANES), lambda b, g: (b, g)),
            pl.BlockSpec((seq, LANES), lambda b, g: (b, ng + g)),
            pl.BlockSpec((cw.shape[0], LANES), lambda b, g: (0, g)),
            pl.BlockSpec((1, LANES), lambda b, g: (0, g)),
            pl.BlockSpec((1, LANES, 4 * LANES), lambda b, g: (g, 0, 0)),
            pl.BlockSpec((1, 1, 4 * LANES), lambda b, g: (g, 0, 0)),
            pl.BlockSpec((1, 1, 2 * LANES), lambda b, g: (g, 0, 0)),
        ],
        out_specs=pl.BlockSpec((seq, LANES), lambda b, g: (b, g)),
        out_shape=jax.ShapeDtypeStruct((n_tok, D_A), F32),
        scratch_shapes=[pltpu.VMEM((seq, LANES), F32) for _ in range(10)]
        + [pltpu.VMEM((gate_rows, 4 * LANES), F32) for _ in range(2)],
        compiler_params=_cparams(("parallel", "parallel")),
        name="rglru",
    )(pa, pa, cw, cb, wg, bias, lam)


_ATTN_STRAIGHT = (0, 2, 3, 5)
_ATTN_ROLLED = (1, 4)
_ONES_ROWS = 16


def _attn_kernel(sink_ref, bias_ref, q_ref, k_ref, vt_ref, o_ref, s_scr, p_scr, e_scr, *, seq):
    band = 3 * BLOCK
    nblk = seq // BLOCK
    lo_q = lax.broadcasted_iota(jnp.int32, (BLOCK, 2 * HEAD_DIM), 1) < HEAD_DIM
    ones = jnp.ones((_ONES_ROWS, band), BF16)
    nt = (((1,), (1,)), ((), ()))
    stack = _ATTN_STRAIGHT + _ATTN_ROLLED
    ns = len(_ATTN_STRAIGHT) * BLOCK

    def window(j):
        q0 = j * BLOCK
        return q0, pl.multiple_of(jnp.clip(q0 - BLOCK, 0, seq - band), BLOCK)

    def scores(j, slot):
        q0, k0 = window(j)
        rows = pl.ds(pl.multiple_of(q0, BLOCK), BLOCK)
        kb = k_ref[pl.ds(k0, band), :]
        kbs = pltpu.roll(kb, HEAD_DIM, axis=1)

        def own_half(h):
            qt = q_ref[rows, (h // 2) * 2 * HEAD_DIM:(h // 2 + 1) * 2 * HEAD_DIM]
            return jnp.where(lo_q, qt, 0.0) if h % 2 == 0 else jnp.where(lo_q, 0.0, qt)

        qa = jnp.concatenate([own_half(h) for h in _ATTN_STRAIGHT], axis=0)
        qb = jnp.concatenate([own_half(h) for h in _ATTN_ROLLED], axis=0)
        s_scr[slot, :, :ns] = lax.dot_general(kb, qa, nt, preferred_element_type=F32)
        s_scr[slot, :, ns:] = lax.dot_general(kbs, qb, nt, preferred_element_type=F32)

    def softmax(j, slot):
        q0, k0 = window(j)
        bias = bias_ref[(q0 - k0) // BLOCK]
        for h in range(N_Q_HEADS):
            src = stack.index(h) * BLOCK
            s = s_scr[slot, :, src:src + BLOCK] + bias
            sk = sink_ref[h] * LOG2E
            m = jnp.maximum(jnp.max(s, axis=0, keepdims=True), sk)
            p_scr[slot, :, h * BLOCK:(h + 1) * BLOCK] = jnp.exp2(s - m).astype(BF16)
            e_scr[slot, :, h * BLOCK:(h + 1) * BLOCK] = jnp.broadcast_to(
                jnp.exp2(sk - m), (SUBLANES, BLOCK))

    def values(j, slot):
        q0, k0 = window(j)
        vt = vt_ref[:, pl.ds(k0, band)]
        outs = []
        for kv in range(N_KV_HEADS):
            cols = slice(kv * GROUP * BLOCK, (kv + 1) * GROUP * BLOCK)
            lhs = jnp.concatenate([vt[kv * HEAD_DIM:(kv + 1) * HEAD_DIM], ones], axis=0)
            ov = jnp.dot(lhs, p_scr[slot, :, cols], preferred_element_type=F32)
            res = ov[:HEAD_DIM] / (ov[HEAD_DIM:HEAD_DIM + 1] + e_scr[slot, 0:1, cols])
            outs += [res[:, g * BLOCK:(g + 1) * BLOCK] for g in range(GROUP)]
        rows = pl.ds(pl.multiple_of(q0, BLOCK), BLOCK)
        o_ref[rows, :] = jnp.concatenate(outs, axis=0).T

    scores(0, 0)
    softmax(0, 0)
    scores(1, 1)

    def step(jj, carry):
        j = 2 * jj
        values(j, 0)
        softmax(j + 1, 1)
        scores(j + 2, 0)
        values(j + 1, 1)
        softmax(j + 2, 0)
        scores(j + 3, 1)
        return carry

    lax.fori_loop(0, (nblk - 2) // 2, step, 0)
    values(nblk - 2, 0)
    softmax(nblk - 1, 1)
    values(nblk - 1, 1)


def _attn_bias_table():
    ki = np.arange(3 * BLOCK)[None, :, None]
    qi = np.arange(BLOCK)[None, None, :]
    off = (np.arange(3) * BLOCK)[:, None, None]
    return jnp.asarray(np.where(np.abs(off + qi - ki) <= WINDOW, 0.0, NEG).astype(np.float32))


def _attention(pq, vt, sink, bias, batch, seq):
    n_tok = pq.shape[0]
    band = 3 * BLOCK
    kern = functools.partial(_attn_kernel, seq=seq)
    return pl.pallas_call(
        kern,
        grid=(batch,),
        in_specs=[
            pl.BlockSpec(memory_space=pltpu.SMEM),
            pl.BlockSpec(bias.shape, lambda b: (0, 0, 0)),
            pl.BlockSpec((seq, D_B), lambda b: (b, 0)),
            pl.BlockSpec((seq, D_KV), lambda b: (b, D_B // D_KV)),
            pl.BlockSpec((D_KV, seq), lambda b: (0, b)),
        ],
        out_specs=pl.BlockSpec((seq, D_B), lambda b: (b, 0)),
        out_shape=jax.ShapeDtypeStruct((n_tok, D_B), F32),
        scratch_shapes=[pltpu.VMEM((2, band, N_Q_HEADS * BLOCK), F32),
                        pltpu.VMEM((2, band, N_Q_HEADS * BLOCK), BF16),
                        pltpu.VMEM((2, SUBLANES, N_Q_HEADS * BLOCK), F32)],
        compiler_params=_cparams(("parallel",)),
        name="win_attn",
    )(sink, bias, pq, pq, vt)


def _hy_filter_kernel(z_ref, w1_ref, b1_ref, fr_ref, w2_ref, b2_ref, w3_ref, dec_ref, o_ref):
    hi = lax.Precision.HIGHEST
    fr = fr_ref[...]
    h = jnp.sin(fr * (jnp.dot(z_ref[...], w1_ref[...], preferred_element_type=F32, precision=hi)
                      + b1_ref[...]))
    h = jnp.sin(fr * (jnp.dot(h, w2_ref[...], preferred_element_type=F32, precision=hi) + b2_ref[...]))
    f = jnp.dot(h, w3_ref[...], preferred_element_type=F32, precision=hi) * dec_ref[...]
    for half in range(2):
        for d in range(2):
            c0 = (2 * half + d) * D_C
            o_ref[d, half] = f[:, c0:c0 + D_C]


def _hy_filter(zemb2, w1, b1, fr, w2, b2, w3, dec4, rows=512):
    n = zemb2.shape[0]
    full = lambda a: pl.BlockSpec(a.shape, lambda i: (0,) * a.ndim)
    return pl.pallas_call(
        _hy_filter_kernel,
        grid=(n // rows,),
        in_specs=[pl.BlockSpec((rows, zemb2.shape[1]), lambda i: (i, 0)),
                  full(w1), full(b1), full(fr), full(w2), full(b2), full(w3),
                  pl.BlockSpec((rows, 4 * D_C), lambda i: (i, 0))],
        out_specs=pl.BlockSpec((2, 2, rows, D_C), lambda i: (0, 0, i, 0)),
        out_shape=jax.ShapeDtypeStruct((2, 2, n, D_C), F32),
        compiler_params=_cparams(("parallel",)),
        name="hyena_filter",
    )(zemb2, w1, b1, fr, w2, b2, w3, dec4)


def _dft_outer_fwd(zs, ka_ref, o_ref):
    nj, _, lanes = zs.shape
    pair = BF16_ROWS // DFT_SUB
    for p in range(DFT_BLK // BF16_ROWS):
        parts = []
        for s in range(pair * p, pair * (p + 1)):
            xg = zs[:, s * DFT_SUB:(s + 1) * DFT_SUB, :].reshape(nj * DFT_SUB, lanes).astype(BF16)
            c = jnp.dot(ka_ref[...], xg, preferred_element_type=F32)
            parts.append(c.reshape(2, DFT_NBLK, DFT_SUB, lanes))
        o_ref[0, :, :, p * BF16_ROWS:(p + 1) * BF16_ROWS, :] = (
            jnp.concatenate(parts, axis=2).astype(BF16))


def _hy_fwd_data_kernel(x1_ref, v_ref, cw_ref, cb_ref, ka_ref, o_ref, zs, *, seq):
    nj = seq // DFT_BLK
    cw = cw_ref[...]
    cb = cb_ref[...]
    for j in range(nj):
        x1 = _dwconv_block(x1_ref, j, nj, cw[:, D_C:2 * D_C], cb[:, D_C:2 * D_C], 1)
        v = _dwconv_block(v_ref, j, nj, cw[:, 2 * D_C:], cb[:, 2 * D_C:], 1)
        zs[j] = v * x1
    _dft_outer_fwd(zs, ka_ref, o_ref)


def _hy_fwd_filt_kernel(f_ref, ka_ref, o_ref):
    _dft_outer_fwd(f_ref.at[0], ka_ref, o_ref)


def _hy_fwd_data(pc, cw, cb, ka, batch, seq):
    nj = seq // DFT_BLK
    kern = functools.partial(_hy_fwd_data_kernel, seq=seq)
    return pl.pallas_call(
        kern,
        grid=(batch,),
        in_specs=[
            pl.BlockSpec((seq, D_C), lambda b: (b, 1)),
            pl.BlockSpec((seq, D_C), lambda b: (b, 2)),
            pl.BlockSpec(cw.shape, lambda b: (0, 0)),
            pl.BlockSpec(cb.shape, lambda b: (0, 0)),
            pl.BlockSpec(ka.shape, lambda b: (0, 0)),
        ],
        out_specs=pl.BlockSpec((1, 2, DFT_NBLK, DFT_BLK, D_C), lambda b: (b, 0, 0, 0, 0)),
        out_shape=jax.ShapeDtypeStruct((batch, 2, DFT_NBLK, DFT_BLK, D_C), BF16),
        scratch_shapes=[pltpu.VMEM((nj, DFT_BLK, D_C), F32)],
        compiler_params=_cparams(("parallel",)),
        name="hyena_dft_outer",
    )(pc, pc, cw, cb, ka)


def _hy_fwd_filt(filt4, ka):
    ndir, nj = filt4.shape[0], filt4.shape[1]
    return pl.pallas_call(
        _hy_fwd_filt_kernel,
        grid=(ndir,),
        in_specs=[
            pl.BlockSpec((1, nj, DFT_BLK, D_C), lambda b: (b, 0, 0, 0)),
            pl.BlockSpec(ka.shape, lambda b: (0, 0)),
        ],
        out_specs=pl.BlockSpec((1, 2, DFT_NBLK, DFT_BLK, D_C), lambda b: (b, 0, 0, 0, 0)),
        out_shape=jax.ShapeDtypeStruct((ndir, 2, DFT_NBLK, DFT_BLK, D_C), BF16),
        compiler_params=_cparams(("parallel",)),
        name="hyena_dft_outer_filter",
    )(filt4, ka)


def _hy_inner_kernel(c_ref, f_ref, hbias_ref, g_ref, gi_ref, o_ref, *, batch, kper):
    hbias = hbias_ref[...]
    for q in range(kper):
        cols = [f_ref[d, :, q].reshape(2 * DFT_BLK, D_C) for d in range(2)]
        cols += [c_ref[b, :, q].reshape(2 * DFT_BLK, D_C) for b in range(batch)]
        x = jnp.dot(g_ref[q], jnp.concatenate(cols, axis=1), preferred_element_type=F32)
        hf, hb = x[:, :D_C], x[:, D_C:2 * D_C]
        hre = hf[:DFT_BLK] + hb[:DFT_BLK] + hbias
        him = hf[DFT_BLK:] - hb[DFT_BLK:]
        ys = []
        for b in range(batch):
            xb = x[:, (2 + b) * D_C:(3 + b) * D_C]
            xre, xim = xb[:DFT_BLK], xb[DFT_BLK:]
            ys.append(jnp.concatenate([xre * hre - xim * him, xre * him + xim * hre], axis=0))
        d = jnp.dot(gi_ref[q], jnp.concatenate(ys, axis=1).astype(BF16), preferred_element_type=F32)
        for b in range(batch):
            o_ref[b, :, q] = d[:, b * D_C:(b + 1) * D_C].reshape(2, DFT_BLK, D_C).astype(BF16)


def _hy_inner(cdata, cfilt, hbias, g, gi, kper=8):
    batch = cdata.shape[0]
    kern = functools.partial(_hy_inner_kernel, batch=batch, kper=kper)
    blk = lambda nb: pl.BlockSpec((nb, 2, kper, DFT_BLK, D_C), lambda k: (0, 0, k, 0, 0))
    return pl.pallas_call(
        kern,
        grid=(DFT_NBLK // kper,),
        in_specs=[blk(batch), blk(cfilt.shape[0]), pl.BlockSpec(hbias.shape, lambda k: (0, 0)),
                  pl.BlockSpec((kper, 2 * DFT_BLK, 2 * DFT_BLK), lambda k: (k, 0, 0)),
                  pl.BlockSpec((kper, 2 * DFT_BLK, 2 * DFT_BLK), lambda k: (k, 0, 0))],
        out_specs=blk(batch),
        out_shape=jax.ShapeDtypeStruct(cdata.shape, BF16),
        compiler_params=_cparams(("parallel",)),
        name="hyena_dft_inner",
    )(cdata, cfilt, hbias, g, gi)


def _hy_out_kernel(d_ref, x0_ref, cw_ref, cb_ref, kai_ref, o_ref, ys, *, seq):
    nj = seq // DFT_BLK
    pair = BF16_ROWS // DFT_SUB
    for p in range(DFT_BLK // BF16_ROWS):
        d = d_ref[0, :, :, p * BF16_ROWS:(p + 1) * BF16_ROWS, :].astype(F32)
        for s in range(pair):
            rhs = d[:, :, s * DFT_SUB:(s + 1) * DFT_SUB, :].reshape(2 * DFT_NBLK * DFT_SUB, D_C)
            y = jnp.dot(kai_ref[...], rhs.astype(BF16), preferred_element_type=F32)
            r0 = p * BF16_ROWS + s * DFT_SUB
            ys[:, r0:r0 + DFT_SUB, :] = y.reshape(nj, DFT_SUB, D_C)
    cw = cw_ref[...]
    cb = cb_ref[...]
    for j in range(nj):
        x0 = _dwconv_block(x0_ref, j, nj, cw[:, :D_C], cb[:, :D_C], 1)
        o_ref[j * DFT_BLK:(j + 1) * DFT_BLK, :] = ys[j] * x0


def _hy_out(dd, pc, cw, cb, kai, batch, seq):
    n_tok = pc.shape[0]
    nj = seq // DFT_BLK
    kern = functools.partial(_hy_out_kernel, seq=seq)
    whole = lambda a: pl.BlockSpec(a.shape, lambda b: (0, 0))
    return pl.pallas_call(
        kern,
        grid=(batch,),
        in_specs=[
            pl.BlockSpec((1, 2, DFT_NBLK, DFT_BLK, D_C), lambda b: (b, 0, 0, 0, 0)),
            pl.BlockSpec((seq, D_C), lambda b: (b, 0)),
            whole(cw), whole(cb), whole(kai),
        ],
        out_specs=pl.BlockSpec((seq, D_C), lambda b: (b, 0)),
        out_shape=jax.ShapeDtypeStruct((n_tok, D_C), F32),
        scratch_shapes=[pltpu.VMEM((nj, DFT_BLK, D_C), F32)],
        compiler_params=_cparams(("parallel",)),
        name="hyena_out",
    )(dd, pc, cw, cb, kai)


def _mix_mlp_kernel(ya_ref, yb_ref, yc_ref, x_ref, ga_ref, gb_ref, gc_ref, wo_ref, g_ref, wu_ref,
                    wd_ref, gf_ref, o_ref, *, ff_chunk, final_norm):
    y = jnp.concatenate([_rms(ya_ref[...], ga_ref[...]), _rms(yb_ref[...], gb_ref[...]),
                         _rms(yc_ref[...], gc_ref[...])], axis=-1).astype(BF16)
    x = x_ref[...] + jnp.dot(y, wo_ref[...], preferred_element_type=F32)
    h = _rms(x, g_ref[...]).astype(BF16)
    acc = x
    for c in range(D_FF // ff_chunk):
        sl = slice(c * ff_chunk, (c + 1) * ff_chunk)
        u = jnp.maximum(jnp.dot(h, wu_ref[:, sl], preferred_element_type=F32), 0.0)
        acc = acc + jnp.dot((u * u).astype(BF16), wd_ref[sl, :], preferred_element_type=F32)
    if final_norm:
        acc = _rms(acc, gf_ref[...])
    o_ref[...] = acc


def _mix_mlp(ya, yb, yc, x2, ga, gb, gc, wo_bf, g, wu_bf, wd_bf, gf, layer, tm, final_norm,
             ff_chunk=1024):
    n_tok = x2.shape[0]
    kern = functools.partial(_mix_mlp_kernel, ff_chunk=ff_chunk, final_norm=final_norm)
    row = lambda w: pl.BlockSpec((tm, w), lambda i: (i, 0))
    vec = lambda w: pl.BlockSpec((1, w), lambda i: (0, 0))
    resident = lambda r, c: pl.BlockSpec((None, r, c), lambda i: (layer, 0, 0),
                                         pipeline_mode=pl.Buffered(1))
    return pl.pallas_call(
        kern,
        grid=(n_tok // tm,),
        in_specs=[row(D_A), row(D_B), row(D_C), row(D_MODEL), vec(D_A), vec(D_B), vec(D_C),
                  resident(D_MODEL, D_MODEL), vec(D_MODEL), resident(D_MODEL, D_FF),
                  resident(D_FF, D_MODEL), vec(D_MODEL)],
        out_specs=row(D_MODEL),
        out_shape=jax.ShapeDtypeStruct((n_tok, D_MODEL), F32),
        compiler_params=_cparams(("parallel",)),
        name="mix_mlp",
    )(ya, yb, yc, x2, ga, gb, gc, wo_bf, g, wu_bf, wd_bf, gf)


def _rope_tables(seq):
    pos = np.arange(seq, dtype=np.float32)
    inv_freq = (np.float32(ROPE_THETA) ** (-np.arange(0, ROT_DIM, 2, dtype=np.float32) / ROT_DIM))
    ang = (pos[:, None] * inv_freq[None, :]).astype(np.float32)
    cos, sin = np.cos(ang).astype(np.float32), np.sin(ang).astype(np.float32)
    rest = HEAD_DIM - ROT_DIM
    c_head = np.concatenate([cos, cos, np.ones((seq, rest), np.float32)], axis=1)
    s_head = np.concatenate([-sin, sin, np.zeros((seq, rest), np.float32)], axis=1)
    scale = np.float32(HEAD_DIM ** -0.5 * LOG2E)
    pair = lambda t: np.tile(t, (1, 2))
    c = np.concatenate([pair(c_head) * scale, pair(c_head)], axis=1)
    s = np.concatenate([pair(s_head) * scale, pair(s_head)], axis=1)
    return jnp.asarray(c), jnp.asarray(s)


def _hyena_position_tables(seq):
    t = np.linspace(0.0, 1.0, seq, dtype=np.float32)[:, None]
    w = (2.0 * math.pi * np.arange(seq, dtype=np.float32)[:, None] / seq).astype(np.float32)
    f = np.linspace(1e-4, HY_BANDS - 1, HY_BANDS, dtype=np.float32)[None, :]
    fw = (f * w).astype(np.float32)
    z = np.concatenate([t, np.cos(fw), -np.sin(fw)], axis=-1).astype(np.float32)
    deltas = np.abs(np.linspace(HY_MIN_DECAY, HY_MAX_DECAY, D_C, dtype=np.float32))
    decay = np.exp(-t * deltas[None, :]).astype(np.float32)
    decay_b = decay.copy()
    decay_b[0] = 0.0
    half = seq // 2
    zp = np.zeros((half, 2, HY_WIDTH), np.float32)
    zp[:, 0, :HY_EMB] = z[:half]
    zp[:, 1, :HY_EMB] = z[half:]
    dec4 = np.concatenate([decay[:half], decay_b[:half], decay[half:], decay_b[half:]], axis=1)
    return jnp.asarray(zp.reshape(half, 2 * HY_WIDTH)), jnp.asarray(dec4)


def _blockdiag2(a):
    z = jnp.zeros_like(a)
    return jnp.concatenate([jnp.concatenate([a, z], axis=1), jnp.concatenate([z, a], axis=1)], axis=0)


def _dft_tables(seq):
    n = 2 * seq
    nj = seq // DFT_BLK
    kk = np.arange(DFT_NBLK)
    ang = 2.0 * np.pi * np.outer(kk, np.arange(nj)) / DFT_NBLK
    eye = np.eye(DFT_SUB)
    ka = np.concatenate([np.kron(np.cos(ang), eye), np.kron(-np.sin(ang), eye)], axis=0)
    kai = np.concatenate([np.kron(np.cos(ang).T, eye), np.kron(-np.sin(ang).T, eye)], axis=1) / n
    m = np.arange(DFT_BLK)
    k = kk[:, None, None] + DFT_NBLK * np.arange(DFT_BLK)[None, :, None]
    ph = 2.0 * np.pi * ((k * m[None, None, :]) % n) / n
    gre, gim = np.cos(ph), -np.sin(ph)
    g = np.concatenate([np.concatenate([gre, -gim], axis=2), np.concatenate([gim, gre], axis=2)], axis=1)
    gi = np.transpose(g, (0, 2, 1))
    as_bf = lambda a: jnp.asarray(a.astype(np.float32)).astype(BF16)
    return as_bf(ka), as_bf(kai), as_bf(g), as_bf(gi)


def _lru_blockdiag(w):
    nb = w.shape[1] // 2
    w = w.reshape(2, nb, 2, w.shape[2], w.shape[3])
    z = jnp.zeros_like(w[:, :, 0])
    top = jnp.concatenate([w[:, :, 0], z], axis=-1)
    bot = jnp.concatenate([z, w[:, :, 1]], axis=-1)
    return jnp.concatenate([top, bot], axis=-2)


def kernel(x, norm_mix_g, w_in, conv_a_w, conv_a_b, lru_wa, lru_ba, lru_wx, lru_bx, lru_lambda,
           attn_sink, hy_conv_w, hy_conv_b, hy_w1, hy_b1, hy_freq, hy_w2, hy_b2, hy_w3, hy_bias,
           gnorm_a, gnorm_b, gnorm_c, w_out, norm_mlp_g, w_up, w_down, final_norm_g):
    batch, seq, _ = x.shape
    depth = w_in.shape[0]
    n_tok = batch * seq
    tm = 1024
    ng = D_A // LANES

    rc, rs = _rope_tables(seq)
    attn_bias = _attn_bias_table()
    zemb, dec4 = _hyena_position_tables(seq)
    ka, kai, gtab, gitab = _dft_tables(seq)

    w_in_bf, w_out_bf = w_in.astype(BF16), w_out.astype(BF16)
    w_up_bf, w_down_bf = w_up.astype(BF16), w_down.astype(BF16)

    xs = x.reshape(n_tok, D_MODEL)
    for i in range(depth):
        pa, pq, vt, pc = _in_proj(xs, norm_mix_g[i][None], w_in_bf, i, rc, rs, seq, tm)

        wa, wx = _lru_blockdiag(lru_wa[i]), _lru_blockdiag(lru_wx[i])
        wg = jnp.concatenate([wa[0], wx[0], wa[1], wx[1]], axis=-1).astype(BF16)
        tile = lambda v: v.reshape(ng, 1, LANES)
        bias = jnp.concatenate([tile(lru_ba[i][0]), tile(lru_bx[i][0]),
                                tile(lru_ba[i][1]), tile(lru_bx[i][1])], axis=-1)
        lam = jnp.concatenate([tile(lru_lambda[i][0]), tile(lru_lambda[i][1])], axis=-1)
        y_a = _lru(pa, conv_a_w[i], conv_a_b[i][None], wg, bias, lam, batch, seq)

        y_b = _attention(pq, vt, attn_sink[i], attn_bias, batch, seq)

        w1p = jnp.zeros((HY_WIDTH, HY_WIDTH), F32).at[:HY_EMB].set(hy_w1[i])
        pair = lambda v: jnp.concatenate([v, v])[None]
        filt = _hy_filter(zemb, _blockdiag2(w1p), pair(hy_b1[i]), pair(hy_freq[i]),
                          _blockdiag2(hy_w2[i]), pair(hy_b2[i]), _blockdiag2(hy_w3[i]), dec4)
        cw, cb = hy_conv_w[i], hy_conv_b[i][None]
        cdata = _hy_fwd_data(pc, cw, cb, ka, batch, seq)
        cfilt = _hy_fwd_filt(filt.reshape(2, seq // DFT_BLK, DFT_BLK, D_C), ka)
        dd = _hy_inner(cdata, cfilt, hy_bias[i][None], gtab, gitab)
        y_c = _hy_out(dd, pc, cw, cb, kai, batch, seq)

        xs = _mix_mlp(y_a, y_b, y_c, xs, gnorm_a[i][None], gnorm_b[i][None], gnorm_c[i][None],
                      w_out_bf, norm_mlp_g[i][None], w_up_bf, w_down_bf, final_norm_g[None], i, tm,
                      final_norm=(i == depth - 1))
    return xs.reshape(batch, seq, D_MODEL)
```

```python
import functools
import math

import numpy as np
import jax
import jax.numpy as jnp
from jax import lax
from jax.experimental import pallas as pl
from jax.experimental.pallas import tpu as pltpu

F32 = jnp.float32
BF16 = jnp.bfloat16

D_MODEL = 1024
D_A = 384
D_B = 384
D_C = 256
HEAD_DIM = 64
N_Q_HEADS = 6
N_KV_HEADS = 2
GROUP = N_Q_HEADS // N_KV_HEADS
D_KV = N_KV_HEADS * HEAD_DIM
D_QKV = D_B + 2 * D_KV
D_IN = 2 * D_A + D_QKV + 3 * D_C
C_LRU = 8.0
WINDOW = 128
BLOCK = 128
ROPE_THETA = 500000.0
ROT_DIM = HEAD_DIM // 4
HY_EMB = 33
HY_BANDS = (HY_EMB - 1) // 2
HY_WIDTH = 64
HY_TARGET = 1e-2
HY_MAX_DECAY = math.log(HY_TARGET) / 0.3
HY_MIN_DECAY = math.log(HY_TARGET) / 1.5
D_FF = 4 * D_MODEL
EPS = 1e-6
NEG = -1e30
LOG2E = math.log2(math.e)

LANES = 128
SUBLANES = 8
VMEM_LIMIT = 56 * 1024 * 1024

SCAN_SEGS = SUBLANES
DFT_BLK = 128
DFT_NBLK = 64
DFT_SUB = 8
BF16_ROWS = 16
IN_PROJ_CHUNKS = 4


def _cparams(sem):
    return pltpu.CompilerParams(dimension_semantics=sem, vmem_limit_bytes=VMEM_LIMIT)


def _rms(x, g):
    return x * lax.rsqrt(jnp.mean(x * x, axis=-1, keepdims=True) + EPS) * g


def _in_proj_kernel(x_ref, g_ref, w_ref, rc_ref, rs_ref, oa_ref, oq_ref, ov_ref, oc_ref):
    n = D_B + D_KV
    half = ROT_DIM // 2
    tm = x_ref.shape[0]
    for c in range(IN_PROJ_CHUNKS):
        r = slice(c * tm // IN_PROJ_CHUNKS, (c + 1) * tm // IN_PROJ_CHUNKS)
        h = _rms(x_ref[r, :], g_ref[...]).astype(BF16)
        oa_ref[r, :] = jnp.dot(h, w_ref[:, :2 * D_A], preferred_element_type=F32).astype(BF16)
        oc_ref[r, :] = jnp.dot(h, w_ref[:, 2 * D_A + D_QKV:], preferred_element_type=F32).astype(BF16)
        qkv = jnp.dot(h, w_ref[:, 2 * D_A:2 * D_A + D_QKV], preferred_element_type=F32)
        qk = qkv[:, :n]
        lane = lax.broadcasted_iota(jnp.int32, qk.shape, 1) % HEAD_DIM
        swapped = jnp.where(lane < half, pltpu.roll(qk, n - half, axis=1), pltpu.roll(qk, half, axis=1))
        wide = lambda t: jnp.concatenate([t[:, :2 * HEAD_DIM]] * (D_B // (2 * HEAD_DIM))
                                         + [t[:, 2 * HEAD_DIM:]], axis=1)
        oq_ref[r, :] = (qk * wide(rc_ref[r, :]) + swapped * wide(rs_ref[r, :])).astype(BF16)
        ov_ref[:, r] = qkv[:, n:].T.astype(BF16)


def _in_proj(x2, g, w_bf, layer, rc, rs, seq, tm):
    n_tok = x2.shape[0]
    nrb = seq // tm
    return pl.pallas_call(
        _in_proj_kernel,
        grid=(n_tok // tm,),
        in_specs=[
            pl.BlockSpec((tm, D_MODEL), lambda i: (i, 0)),
            pl.BlockSpec((1, D_MODEL), lambda i: (0, 0)),
            pl.BlockSpec((None, D_MODEL, D_IN), lambda i: (layer, 0, 0)),
            pl.BlockSpec((tm, 4 * HEAD_DIM), lambda i: (i % nrb, 0)),
            pl.BlockSpec((tm, 4 * HEAD_DIM), lambda i: (i % nrb, 0)),
        ],
        out_specs=[
            pl.BlockSpec((tm, 2 * D_A), lambda i: (i, 0)),
            pl.BlockSpec((tm, D_B + D_KV), lambda i: (i, 0)),
            pl.BlockSpec((D_KV, tm), lambda i: (0, i)),
            pl.BlockSpec((tm, 3 * D_C), lambda i: (i, 0)),
        ],
        out_shape=[
            jax.ShapeDtypeStruct((n_tok, 2 * D_A), BF16),
            jax.ShapeDtypeStruct((n_tok, D_B + D_KV), BF16),
            jax.ShapeDtypeStruct((D_KV, n_tok), BF16),
            jax.ShapeDtypeStruct((n_tok, 3 * D_C), BF16),
        ],
        compiler_params=_cparams(("parallel",)),
        name="in_proj",
    )(x2, g, w_bf, rc, rs)


def _dwconv_chunk(src_ref, ci, nchunk, rows, w, b, pad_left):
    halo = BF16_ROWS
    r0 = pl.multiple_of(ci * rows, rows)
    seq = nchunk * rows
    top = src_ref[pl.ds(pl.multiple_of(jnp.maximum(r0 - halo, 0), halo), halo), :]
    bot = src_ref[pl.ds(pl.multiple_of(jnp.minimum(r0 + rows, seq - halo), halo), halo), :]
    top = jnp.where(ci > 0, top, jnp.zeros_like(top))
    bot = jnp.where(ci < nchunk - 1, bot, jnp.zeros_like(bot))
    win = jnp.concatenate([top, src_ref[pl.ds(r0, rows), :], bot], axis=0).astype(F32)
    return _conv_taps(win, rows, w, b, pad_left, halo)


def _conv_taps(win, rows, w, b, pad_left, halo):
    total = rows + 2 * halo
    acc = None
    for k in range(w.shape[0]):
        shift = (pad_left - k) % total
        tap = win if shift == 0 else pltpu.roll(win, shift, axis=0)
        term = tap[halo:halo + rows] * w[k:k + 1, :]
        acc = term if acc is None else acc + term
    return acc + b


def _dwconv_block(src_ref, j, nblk, w, b, pad_left):
    r0 = j * DFT_BLK
    halo = 2 * SUBLANES
    zeros = jnp.zeros((halo, src_ref.shape[1]), src_ref.dtype)
    top = zeros if j == 0 else src_ref[r0 - halo:r0, :]
    bot = zeros if j == nblk - 1 else src_ref[r0 + DFT_BLK:r0 + DFT_BLK + halo, :]
    win = jnp.concatenate([top, src_ref[r0:r0 + DFT_BLK, :], bot], axis=0).astype(F32)
    return _conv_taps(win, DFT_BLK, w, b, pad_left, halo)


def _softplus(x):
    return jnp.maximum(x, 0.0) + jnp.log1p(jnp.exp(-jnp.abs(x)))


def _gelu_tanh(x):
    c = math.sqrt(2.0 / math.pi)
    return x * (0.5 * (1.0 + jnp.tanh(c * (x + 0.044715 * (x * x * x)))))


def _lru_kernel(u_ref, gate_ref, cw_ref, cb_ref, w_ref, bias_ref, lam_ref, o_ref,
                xci, gg, af, bf, ar, br, pfs, hfs, prs, hrs, g0, g1, *, seq, rows, unroll):
    seg = seq // SCAN_SEGS
    tiles = rows // SCAN_SEGS
    nchunk = seq // rows
    per_seg = seg // rows
    cw = cw_ref[...]
    cb = cb_ref[...]
    w = w_ref[0] * 0.5
    bias = bias_ref[0] * 0.5
    hnsp = (-0.5 * C_LRU) * _softplus(-lam_ref[0])

    def conv(ci, carry):
        r0 = pl.multiple_of(ci * rows, rows)
        dst = pl.ds((ci % per_seg) * (rows * SCAN_SEGS) + ci // per_seg, rows, stride=SCAN_SEGS)
        xci[dst, :] = _dwconv_chunk(u_ref, ci, nchunk, rows, cw, cb, 2)
        gg[dst, :] = _gelu_tanh(gate_ref[pl.ds(r0, rows), :].astype(F32))
        return carry

    lax.fori_loop(0, nchunk, conv, 0)

    grows = g0.shape[0]
    ngate = seq // grows

    def gate_matmul(ci, g_ref):
        r0 = pl.multiple_of(ci * grows, grows)
        g_ref[...] = jnp.dot(xci[pl.ds(r0, grows), :].astype(BF16), w, preferred_element_type=F32)

    def gate_math(ci, g_ref):
        r0 = pl.multiple_of(ci * grows, grows)
        xc = xci[pl.ds(r0, grows), :]
        t = jnp.tanh(g_ref[...] + bias)
        hxc = 0.5 * xc
        for d, (a_ref, b_ref) in enumerate(((af, bf), (ar, br))):
            tr = t[:, (2 * d) * LANES:(2 * d + 1) * LANES]
            ti = t[:, (2 * d + 1) * LANES:(2 * d + 2) * LANES]
            hn = hnsp[:, d * LANES:(d + 1) * LANES]
            log_a = tr * hn + hn
            a = jnp.exp(log_a)
            nem = (-1.0 - a * a) * jnp.tanh(log_a)
            a_ref[pl.ds(r0, grows), :] = a
            root = jnp.where(nem > 0.0, nem * lax.rsqrt(nem), 0.0)
            b_ref[pl.ds(r0, grows), :] = root * (ti * hxc + hxc)

    def gates(k, carry):
        gate_matmul(2 * k + 1, g1)
        gate_math(2 * k, g0)
        gate_matmul(jnp.minimum(2 * k + 2, ngate - 1), g0)
        gate_math(2 * k + 1, g1)
        return carry

    gate_matmul(0, g0)
    lax.fori_loop(0, ngate // 2, gates, 0)

    def scan(it, carry):
        span = unroll * SCAN_SEGS

        def run(p, h, a_ref, b_ref, p_out, h_out, base, order):
            rows = pl.ds(pl.multiple_of(base, span), span)
            a_blk, b_blk = a_ref[rows, :], b_ref[rows, :]
            tile = lambda blk, u: blk[u * SCAN_SEGS:(u + 1) * SCAN_SEGS]
            ps, hs = [None] * unroll, [None] * unroll
            for u0, u1 in zip(order[0::2], order[1::2]):
                a0, b0, a1, b1 = tile(a_blk, u0), tile(b_blk, u0), tile(a_blk, u1), tile(b_blk, u1)
                a01 = a1 * a0
                b01 = a1 * b0 + b1
                ps[u0], hs[u0] = a0 * p, a0 * h + b0
                p = a01 * p
                h = a01 * h + b01
                ps[u1], hs[u1] = p, h
            p_out[rows, :] = jnp.concatenate(ps, axis=0)
            h_out[rows, :] = jnp.concatenate(hs, axis=0)
            return p, h

        pf, hf, pr, hr = carry
        fwd = list(range(unroll))
        pf, hf = run(pf, hf, af, bf, pfs, hfs, it * span, fwd)
        pr, hr = run(pr, hr, ar, br, prs, hrs, (seg // unroll - 1 - it) * span, fwd[::-1])
        return pf, hf, pr, hr

    one = jnp.ones((SCAN_SEGS, LANES), F32)
    zero = jnp.zeros((SCAN_SEGS, LANES), F32)
    pf, hf, pr, hr = lax.fori_loop(0, seg // unroll, scan, (one, zero, one, zero))

    sub = lax.broadcasted_iota(jnp.int32, (SCAN_SEGS, LANES), 0)
    cf = zero
    cr = zero
    for _ in range(SCAN_SEGS - 1):
        cf = jnp.where(sub == 0, 0.0, pltpu.roll(hf + pf * cf, 1, axis=0))
        cr = jnp.where(sub == SCAN_SEGS - 1, 0.0, pltpu.roll(hr + pr * cr, SCAN_SEGS - 1, axis=0))

    def combine(ci, carry):
        r0 = pl.multiple_of(ci * rows, rows)
        sl = pl.ds(r0, rows)
        tile3 = lambda ref: ref[sl, :].reshape(tiles, SCAN_SEGS, LANES)
        h = (tile3(hfs) + tile3(pfs) * cf[None]) + (tile3(hrs) + tile3(prs) * cr[None])
        out = h.reshape(rows, LANES) * gg[sl, :]
        for k in range(tiles):
            o_ref[pl.ds(ci * tiles + k, SCAN_SEGS, stride=seg), :] = (
                out[k * SCAN_SEGS:(k + 1) * SCAN_SEGS])
        return carry

    lax.fori_loop(0, nchunk, combine, 0)


def _lru(pa, cw, cb, wg, bias, lam, batch, seq, rows=512, gate_rows=512, unroll=16):
    n_tok = pa.shape[0]
    ng = D_A // LANES
    kern = functools.partial(_lru_kernel, seq=seq, rows=rows, unroll=unroll)
    return pl.pallas_call(
        kern,
        grid=(batch, ng),
        in_specs=[
            pl.BlockSpec((seq, LANES), lambda b, g: (b, g)),
            pl.BlockSpec((seq, LANES), lambda b, g: (b, ng + g)),
            pl.BlockSpec((cw.shape[0], LANES), lambda b, g: (0, g)),
            pl.BlockSpec((1, LANES), lambda b, g: (0, g)),
            pl.BlockSpec((1, LANES, 4 * LANES), lambda b, g: (g, 0, 0)),
            pl.BlockSpec((1, 1, 4 * LANES), lambda b, g: (g, 0, 0)),
            pl.BlockSpec((1, 1, 2 * LANES), lambda b, g: (g, 0, 0)),
        ],
        out_specs=pl.BlockSpec((seq, LANES), lambda b, g: (b, g)),
        out_shape=jax.ShapeDtypeStruct((n_tok, D_A), F32),
        scratch_shapes=[pltpu.VMEM((seq, LANES), F32) for _ in range(10)]
        + [pltpu.VMEM((gate_rows, 4 * LANES), F32) for _ in range(2)],
        compiler_params=_cparams(("parallel", "parallel")),
        name="rglru",
    )(pa, pa, cw, cb, wg, bias, lam)


_ATTN_STRAIGHT = (0, 2, 3, 5)
_ATTN_ROLLED = (1, 4)
_ONES_ROWS = 16


def _attn_kernel(sink_ref, bias_ref, q_ref, k_ref, vt_ref, o_ref, s_scr, p_scr, e_scr, *, seq):
    band = 3 * BLOCK
    nblk = seq // BLOCK
    lo_q = lax.broadcasted_iota(jnp.int32, (BLOCK, 2 * HEAD_DIM), 1) < HEAD_DIM
    ones = jnp.ones((_ONES_ROWS, band), BF16)
    nt = (((1,), (1,)), ((), ()))
    stack = _ATTN_STRAIGHT + _ATTN_ROLLED
    ns = len(_ATTN_STRAIGHT) * BLOCK

    def window(j):
        q0 = j * BLOCK
        return q0, pl.multiple_of(jnp.clip(q0 - BLOCK, 0, seq - band), BLOCK)

    def scores(j, slot):
        q0, k0 = window(j)
        rows = pl.ds(pl.multiple_of(q0, BLOCK), BLOCK)
        kb = k_ref[pl.ds(k0, band), :]
        kbs = pltpu.roll(kb, HEAD_DIM, axis=1)

        def own_half(h):
            qt = q_ref[rows, (h // 2) * 2 * HEAD_DIM:(h // 2 + 1) * 2 * HEAD_DIM]
            return jnp.where(lo_q, qt, 0.0) if h % 2 == 0 else jnp.where(lo_q, 0.0, qt)

        qa = jnp.concatenate([own_half(h) for h in _ATTN_STRAIGHT], axis=0)
        qb = jnp.concatenate([own_half(h) for h in _ATTN_ROLLED], axis=0)
        s_scr[slot, :, :ns] = lax.dot_general(kb, qa, nt, preferred_element_type=F32)
        s_scr[slot, :, ns:] = lax.dot_general(kbs, qb, nt, preferred_element_type=F32)

    def softmax(j, slot):
        q0, k0 = window(j)
        bias = bias_ref[(q0 - k0) // BLOCK]
        for h in range(N_Q_HEADS):
            src = stack.index(h) * BLOCK
            s = s_scr[slot, :, src:src + BLOCK] + bias
            sk = sink_ref[h] * LOG2E
            m = jnp.maximum(jnp.max(s, axis=0, keepdims=True), sk)
            p_scr[slot, :, h * BLOCK:(h + 1) * BLOCK] = jnp.exp2(s - m).astype(BF16)
            e_scr[slot, :, h * BLOCK:(h + 1) * BLOCK] = jnp.broadcast_to(
                jnp.exp2(sk - m), (SUBLANES, BLOCK))

    def values(j, slot):
        q0, k0 = window(j)
        vt = vt_ref[:, pl.ds(k0, band)]
        outs = []
        for kv in range(N_KV_HEADS):
            cols = slice(kv * GROUP * BLOCK, (kv + 1) * GROUP * BLOCK)
            lhs = jnp.concatenate([vt[kv * HEAD_DIM:(kv + 1) * HEAD_DIM], ones], axis=0)
            ov = jnp.dot(lhs, p_scr[slot, :, cols], preferred_element_type=F32)
            res = ov[:HEAD_DIM] / (ov[HEAD_DIM:HEAD_DIM + 1] + e_scr[slot, 0:1, cols])
            outs += [res[:, g * BLOCK:(g + 1) * BLOCK] for g in range(GROUP)]
        rows = pl.ds(pl.multiple_of(q0, BLOCK), BLOCK)
        o_ref[rows, :] = jnp.concatenate(outs, axis=0).T

    scores(0, 0)
    softmax(0, 0)
    scores(1, 1)

    def step(jj, carry):
        j = 2 * jj
        values(j, 0)
        softmax(j + 1, 1)
        scores(j + 2, 0)
        values(j + 1, 1)
        softmax(j + 2, 0)
        scores(j + 3, 1)
        return carry

    lax.fori_loop(0, (nblk - 2) // 2, step, 0)
    values(nblk - 2, 0)
    softmax(nblk - 1, 1)
    values(nblk - 1, 1)


def _attn_bias_table():
    ki = np.arange(3 * BLOCK)[None, :, None]
    qi = np.arange(BLOCK)[None, None, :]
    off = (np.arange(3) * BLOCK)[:, None, None]
    return jnp.asarray(np.where(np.abs(off + qi - ki) <= WINDOW, 0.0, NEG).astype(np.float32))


def _attention(pq, vt, sink, bias, batch, seq):
    n_tok = pq.shape[0]
    band = 3 * BLOCK
    kern = functools.partial(_attn_kernel, seq=seq)
    return pl.pallas_call(
        kern,
        grid=(batch,),
        in_specs=[
            pl.BlockSpec(memory_space=pltpu.SMEM),
            pl.BlockSpec(bias.shape, lambda b: (0, 0, 0)),
            pl.BlockSpec((seq, D_B), lambda b: (b, 0)),
            pl.BlockSpec((seq, D_KV), lambda b: (b, D_B // D_KV)),
            pl.BlockSpec((D_KV, seq), lambda b: (0, b)),
        ],
        out_specs=pl.BlockSpec((seq, D_B), lambda b: (b, 0)),
        out_shape=jax.ShapeDtypeStruct((n_tok, D_B), F32),
        scratch_shapes=[pltpu.VMEM((2, band, N_Q_HEADS * BLOCK), F32),
                        pltpu.VMEM((2, band, N_Q_HEADS * BLOCK), BF16),
                        pltpu.VMEM((2, SUBLANES, N_Q_HEADS * BLOCK), F32)],
        compiler_params=_cparams(("parallel",)),
        name="win_attn",
    )(sink, bias, pq, pq, vt)


def _hy_filter_kernel(z_ref, w1_ref, b1_ref, fr_ref, w2_ref, b2_ref, w3_ref, dec_ref, o_ref):
    hi = lax.Precision.HIGHEST
    fr = fr_ref[...]
    h = jnp.sin(fr * (jnp.dot(z_ref[...], w1_ref[...], preferred_element_type=F32, precision=hi)
                      + b1_ref[...]))
    h = jnp.sin(fr * (jnp.dot(h, w2_ref[...], preferred_element_type=F32, precision=hi) + b2_ref[...]))
    f = jnp.dot(h, w3_ref[...], preferred_element_type=F32, precision=hi) * dec_ref[...]
    for half in range(2):
        for d in range(2):
            c0 = (2 * half + d) * D_C
            o_ref[d, half] = f[:, c0:c0 + D_C]


def _hy_filter(zemb2, w1, b1, fr, w2, b2, w3, dec4, rows=512):
    n = zemb2.shape[0]
    full = lambda a: pl.BlockSpec(a.shape, lambda i: (0,) * a.ndim)
    return pl.pallas_call(
        _hy_filter_kernel,
        grid=(n // rows,),
        in_specs=[pl.BlockSpec((rows, zemb2.shape[1]), lambda i: (i, 0)),
                  full(w1), full(b1), full(fr), full(w2), full(b2), full(w3),
                  pl.BlockSpec((rows, 4 * D_C), lambda i: (i, 0))],
        out_specs=pl.BlockSpec((2, 2, rows, D_C), lambda i: (0, 0, i, 0)),
        out_shape=jax.ShapeDtypeStruct((2, 2, n, D_C), F32),
        compiler_params=_cparams(("parallel",)),
        name="hyena_filter",
    )(zemb2, w1, b1, fr, w2, b2, w3, dec4)


def _dft_outer_fwd(zs, ka_ref, o_ref):
    nj, _, lanes = zs.shape
    pair = BF16_ROWS // DFT_SUB
    for p in range(DFT_BLK // BF16_ROWS):
        parts = []
        for s in range(pair * p, pair * (p + 1)):
            xg = zs[:, s * DFT_SUB:(s + 1) * DFT_SUB, :].reshape(nj * DFT_SUB, lanes).astype(BF16)
            c = jnp.dot(ka_ref[...], xg, preferred_element_type=F32)
            parts.append(c.reshape(2, DFT_NBLK, DFT_SUB, lanes))
        o_ref[0, :, :, p * BF16_ROWS:(p + 1) * BF16_ROWS, :] = (
            jnp.concatenate(parts, axis=2).astype(BF16))


def _hy_fwd_data_kernel(x1_ref, v_ref, cw_ref, cb_ref, ka_ref, o_ref, zs, *, seq):
    nj = seq // DFT_BLK
    cw = cw_ref[...]
    cb = cb_ref[...]
    for j in range(nj):
        x1 = _dwconv_block(x1_ref, j, nj, cw[:, D_C:2 * D_C], cb[:, D_C:2 * D_C], 1)
        v = _dwconv_block(v_ref, j, nj, cw[:, 2 * D_C:], cb[:, 2 * D_C:], 1)
        zs[j] = v * x1
    _dft_outer_fwd(zs, ka_ref, o_ref)


def _hy_fwd_filt_kernel(f_ref, ka_ref, o_ref):
    _dft_outer_fwd(f_ref.at[0], ka_ref, o_ref)


def _hy_fwd_data(pc, cw, cb, ka, batch, seq):
    nj = seq // DFT_BLK
    kern = functools.partial(_hy_fwd_data_kernel, seq=seq)
    return pl.pallas_call(
        kern,
        grid=(batch,),
        in_specs=[
            pl.BlockSpec((seq, D_C), lambda b: (b, 1)),
            pl.BlockSpec((seq, D_C), lambda b: (b, 2)),
            pl.BlockSpec(cw.shape, lambda b: (0, 0)),
            pl.BlockSpec(cb.shape, lambda b: (0, 0)),
            pl.BlockSpec(ka.shape, lambda b: (0, 0)),
        ],
        out_specs=pl.BlockSpec((1, 2, DFT_NBLK, DFT_BLK, D_C), lambda b: (b, 0, 0, 0, 0)),
        out_shape=jax.ShapeDtypeStruct((batch, 2, DFT_NBLK, DFT_BLK, D_C), BF16),
        scratch_shapes=[pltpu.VMEM((nj, DFT_BLK, D_C), F32)],
        compiler_params=_cparams(("parallel",)),
        name="hyena_dft_outer",
    )(pc, pc, cw, cb, ka)


def _hy_fwd_filt(filt4, ka):
    ndir, nj = filt4.shape[0], filt4.shape[1]
    return pl.pallas_call(
        _hy_fwd_filt_kernel,
        grid=(ndir,),
        in_specs=[
            pl.BlockSpec((1, nj, DFT_BLK, D_C), lambda b: (b, 0, 0, 0)),
            pl.BlockSpec(ka.shape, lambda b: (0, 0)),
        ],
        out_specs=pl.BlockSpec((1, 2, DFT_NBLK, DFT_BLK, D_C), lambda b: (b, 0, 0, 0, 0)),
        out_shape=jax.ShapeDtypeStruct((ndir, 2, DFT_NBLK, DFT_BLK, D_C), BF16),
        compiler_params=_cparams(("parallel",)),
        name="hyena_dft_outer_filter",
    )(filt4, ka)


def _hy_inner_kernel(c_ref, f_ref, hbias_ref, g_ref, gi_ref, o_ref, *, batch, kper):
    hbias = hbias_ref[...]
    for q in range(kper):
        cols = [f_ref[d, :, q].reshape(2 * DFT_BLK, D_C) for d in range(2)]
        cols += [c_ref[b, :, q].reshape(2 * DFT_BLK, D_C) for b in range(batch)]
        x = jnp.dot(g_ref[q], jnp.concatenate(cols, axis=1), preferred_element_type=F32)
        hf, hb = x[:, :D_C], x[:, D_C:2 * D_C]
        hre = hf[:DFT_BLK] + hb[:DFT_BLK] + hbias
        him = hf[DFT_BLK:] - hb[DFT_BLK:]
        ys = []
        for b in range(batch):
            xb = x[:, (2 + b) * D_C:(3 + b) * D_C]
            xre, xim = xb[:DFT_BLK], xb[DFT_BLK:]
            ys.append(jnp.concatenate([xre * hre - xim * him, xre * him + xim * hre], axis=0))
        d = jnp.dot(gi_ref[q], jnp.concatenate(ys, axis=1).astype(BF16), preferred_element_type=F32)
        for b in range(batch):
            o_ref[b, :, q] = d[:, b * D_C:(b + 1) * D_C].reshape(2, DFT_BLK, D_C).astype(BF16)


def _hy_inner(cdata, cfilt, hbias, g, gi, kper=8):
    batch = cdata.shape[0]
    kern = functools.partial(_hy_inner_kernel, batch=batch, kper=kper)
    blk = lambda nb: pl.BlockSpec((nb, 2, kper, DFT_BLK, D_C), lambda k: (0, 0, k, 0, 0))
    return pl.pallas_call(
        kern,
        grid=(DFT_NBLK // kper,),
        in_specs=[blk(batch), blk(cfilt.shape[0]), pl.BlockSpec(hbias.shape, lambda k: (0, 0)),
                  pl.BlockSpec((kper, 2 * DFT_BLK, 2 * DFT_BLK), lambda k: (k, 0, 0)),
                  pl.BlockSpec((kper, 2 * DFT_BLK, 2 * DFT_BLK), lambda k: (k, 0, 0))],
        out_specs=blk(batch),
        out_shape=jax.ShapeDtypeStruct(cdata.shape, BF16),
        compiler_params=_cparams(("parallel",)),
        name="hyena_dft_inner",
    )(cdata, cfilt, hbias, g, gi)


def _hy_out_kernel(d_ref, x0_ref, cw_ref, cb_ref, kai_ref, o_ref, ys, *, seq):
    nj = seq // DFT_BLK
    pair = BF16_ROWS // DFT_SUB
    for p in range(DFT_BLK // BF16_ROWS):
        d = d_ref[0, :, :, p * BF16_ROWS:(p + 1) * BF16_ROWS, :].astype(F32)
        for s in range(pair):
            rhs = d[:, :, s * DFT_SUB:(s + 1) * DFT_SUB, :].reshape(2 * DFT_NBLK * DFT_SUB, D_C)
            y = jnp.dot(kai_ref[...], rhs.astype(BF16), preferred_element_type=F32)
            r0 = p * BF16_ROWS + s * DFT_SUB
            ys[:, r0:r0 + DFT_SUB, :] = y.reshape(nj, DFT_SUB, D_C)
    cw = cw_ref[...]
    cb = cb_ref[...]
    for j in range(nj):
        x0 = _dwconv_block(x0_ref, j, nj, cw[:, :D_C], cb[:, :D_C], 1)
        o_ref[j * DFT_BLK:(j + 1) * DFT_BLK, :] = ys[j] * x0


def _hy_out(dd, pc, cw, cb, kai, batch, seq):
    n_tok = pc.shape[0]
    nj = seq // DFT_BLK
    kern = functools.partial(_hy_out_kernel, seq=seq)
    whole = lambda a: pl.BlockSpec(a.shape, lambda b: (0, 0))
    return pl.pallas_call(
        kern,
        grid=(batch,),
        in_specs=[
            pl.BlockSpec((1, 2, DFT_NBLK, DFT_BLK, D_C), lambda b: (b, 0, 0, 0, 0)),
            pl.BlockSpec((seq, D_C), lambda b: (b, 0)),
            whole(cw), whole(cb), whole(kai),
        ],
        out_specs=pl.BlockSpec((seq, D_C), lambda b: (b, 0)),
        out_shape=jax.ShapeDtypeStruct((n_tok, D_C), F32),
        scratch_shapes=[pltpu.VMEM((nj, DFT_BLK, D_C), F32)],
        compiler_params=_cparams(("parallel",)),
        name="hyena_out",
    )(dd, pc, cw, cb, kai)


def _mix_mlp_kernel(ya_ref, yb_ref, yc_ref, x_ref, ga_ref, gb_ref, gc_ref, wo_ref, g_ref, wu_ref,
                    wd_ref, gf_ref, o_ref, *, ff_chunk, final_norm):
    y = jnp.concatenate([_rms(ya_ref[...], ga_ref[...]), _rms(yb_ref[...], gb_ref[...]),
                         _rms(yc_ref[...], gc_ref[...])], axis=-1).astype(BF16)
    x = x_ref[...] + jnp.dot(y, wo_ref[...], preferred_element_type=F32)
    h = _rms(x, g_ref[...]).astype(BF16)
    acc = x
    for c in range(D_FF // ff_chunk):
        sl = slice(c * ff_chunk, (c + 1) * ff_chunk)
        u = jnp.maximum(jnp.dot(h, wu_ref[:, sl], preferred_element_type=F32), 0.0)
        acc = acc + jnp.dot((u * u).astype(BF16), wd_ref[sl, :], preferred_element_type=F32)
    if final_norm:
        acc = _rms(acc, gf_ref[...])
    o_ref[...] = acc


def _mix_mlp(ya, yb, yc, x2, ga, gb, gc, wo_bf, g, wu_bf, wd_bf, gf, layer, tm, final_norm,
             ff_chunk=1024):
    n_tok = x2.shape[0]
    kern = functools.partial(_mix_mlp_kernel, ff_chunk=ff_chunk, final_norm=final_norm)
    row = lambda w: pl.BlockSpec((tm, w), lambda i: (i, 0))
    vec = lambda w: pl.BlockSpec((1, w), lambda i: (0, 0))
    resident = lambda r, c: pl.BlockSpec((None, r, c), lambda i: (layer, 0, 0),
                                         pipeline_mode=pl.Buffered(1))
    return pl.pallas_call(
        kern,
        grid=(n_tok // tm,),
        in_specs=[row(D_A), row(D_B), row(D_C), row(D_MODEL), vec(D_A), vec(D_B), vec(D_C),
                  resident(D_MODEL, D_MODEL), vec(D_MODEL), resident(D_MODEL, D_FF),
                  resident(D_FF, D_MODEL), vec(D_MODEL)],
        out_specs=row(D_MODEL),
        out_shape=jax.ShapeDtypeStruct((n_tok, D_MODEL), F32),
        compiler_params=_cparams(("parallel",)),
        name="mix_mlp",
    )(ya, yb, yc, x2, ga, gb, gc, wo_bf, g, wu_bf, wd_bf, gf)


def _rope_tables(seq):
    pos = np.arange(seq, dtype=np.float32)
    inv_freq = (np.float32(ROPE_THETA) ** (-np.arange(0, ROT_DIM, 2, dtype=np.float32) / ROT_DIM))
    ang = (pos[:, None] * inv_freq[None, :]).astype(np.float32)
    cos, sin = np.cos(ang).astype(np.float32), np.sin(ang).astype(np.float32)
    rest = HEAD_DIM - ROT_DIM
    c_head = np.concatenate([cos, cos, np.ones((seq, rest), np.float32)], axis=1)
    s_head = np.concatenate([-sin, sin, np.zeros((seq, rest), np.float32)], axis=1)
    scale = np.float32(HEAD_DIM ** -0.5 * LOG2E)
    pair = lambda t: np.tile(t, (1, 2))
    c = np.concatenate([pair(c_head) * scale, pair(c_head)], axis=1)
    s = np.concatenate([pair(s_head) * scale, pair(s_head)], axis=1)
    return jnp.asarray(c), jnp.asarray(s)


def _hyena_position_tables(seq):
    t = np.linspace(0.0, 1.0, seq, dtype=np.float32)[:, None]
    w = (2.0 * math.pi * np.arange(seq, dtype=np.float32)[:, None] / seq).astype(np.float32)
    f = np.linspace(1e-4, HY_BANDS - 1, HY_BANDS, dtype=np.float32)[None, :]
    fw = (f * w).astype(np.float32)
    z = np.concatenate([t, np.cos(fw), -np.sin(fw)], axis=-1).astype(np.float32)
    deltas = np.abs(np.linspace(HY_MIN_DECAY, HY_MAX_DECAY, D_C, dtype=np.float32))
    decay = np.exp(-t * deltas[None, :]).astype(np.float32)
    decay_b = decay.copy()
    decay_b[0] = 0.0
    half = seq // 2
    zp = np.zeros((half, 2, HY_WIDTH), np.float32)
    zp[:, 0, :HY_EMB] = z[:half]
    zp[:, 1, :HY_EMB] = z[half:]
    dec4 = np.concatenate([decay[:half], decay_b[:half], decay[half:], decay_b[half:]], axis=1)
    return jnp.asarray(zp.reshape(half, 2 * HY_WIDTH)), jnp.asarray(dec4)


def _blockdiag2(a):
    z = jnp.zeros_like(a)
    return jnp.concatenate([jnp.concatenate([a, z], axis=1), jnp.concatenate([z, a], axis=1)], axis=0)


def _dft_tables(seq):
    n = 2 * seq
    nj = seq // DFT_BLK
    kk = np.arange(DFT_NBLK)
    ang = 2.0 * np.pi * np.outer(kk, np.arange(nj)) / DFT_NBLK
    eye = np.eye(DFT_SUB)
    ka = np.concatenate([np.kron(np.cos(ang), eye), np.kron(-np.sin(ang), eye)], axis=0)
    kai = np.concatenate([np.kron(np.cos(ang).T, eye), np.kron(-np.sin(ang).T, eye)], axis=1) / n
    m = np.arange(DFT_BLK)
    k = kk[:, None, None] + DFT_NBLK * np.arange(DFT_BLK)[None, :, None]
    ph = 2.0 * np.pi * ((k * m[None, None, :]) % n) / n
    gre, gim = np.cos(ph), -np.sin(ph)
    g = np.concatenate([np.concatenate([gre, -gim], axis=2), np.concatenate([gim, gre], axis=2)], axis=1)
    gi = np.transpose(g, (0, 2, 1))
    as_bf = lambda a: jnp.asarray(a.astype(np.float32)).astype(BF16)
    return as_bf(ka), as_bf(kai), as_bf(g), as_bf(gi)


def _lru_blockdiag(w):
    nb = w.shape[1] // 2
    w = w.reshape(2, nb, 2, w.shape[2], w.shape[3])
    z = jnp.zeros_like(w[:, :, 0])
    top = jnp.concatenate([w[:, :, 0], z], axis=-1)
    bot = jnp.concatenate([z, w[:, :, 1]], axis=-1)
    return jnp.concatenate([top, bot], axis=-2)


def kernel(x, norm_mix_g, w_in, conv_a_w, conv_a_b, lru_wa, lru_ba, lru_wx, lru_bx, lru_lambda,
           attn_sink, hy_conv_w, hy_conv_b, hy_w1, hy_b1, hy_freq, hy_w2, hy_b2, hy_w3, hy_bias,
           gnorm_a, gnorm_b, gnorm_c, w_out, norm_mlp_g, w_up, w_down, final_norm_g):
    batch, seq, _ = x.shape
    depth = w_in.shape[0]
    n_tok = batch * seq
    tm = 1024
    ng = D_A // LANES

    rc, rs = _rope_tables(seq)
    attn_bias = _attn_bias_table()
    zemb, dec4 = _hyena_position_tables(seq)
    ka, kai, gtab, gitab = _dft_tables(seq)

    w_in_bf, w_out_bf = w_in.astype(BF16), w_out.astype(BF16)
    w_up_bf, w_down_bf = w_up.astype(BF16), w_down.astype(BF16)

    xs = x.reshape(n_tok, D_MODEL)
    for i in range(depth):
        pa, pq, vt, pc = _in_proj(xs, norm_mix_g[i][None], w_in_bf, i, rc, rs, seq, 2 * tm)

        wa, wx = _lru_blockdiag(lru_wa[i]), _lru_blockdiag(lru_wx[i])
        wg = jnp.concatenate([wa[0], wx[0], wa[1], wx[1]], axis=-1).astype(BF16)
        tile = lambda v: v.reshape(ng, 1, LANES)
        bias = jnp.concatenate([tile(lru_ba[i][0]), tile(lru_bx[i][0]),
                                tile(lru_ba[i][1]), tile(lru_bx[i][1])], axis=-1)
        lam = jnp.concatenate([tile(lru_lambda[i][0]), tile(lru_lambda[i][1])], axis=-1)
        y_a = _lru(pa, conv_a_w[i], conv_a_b[i][None], wg, bias, lam, batch, seq)

        y_b = _attention(pq, vt, attn_sink[i], attn_bias, batch, seq)

        w1p = jnp.zeros((HY_WIDTH, HY_WIDTH), F32).at[:HY_EMB].set(hy_w1[i])
        pair = lambda v: jnp.concatenate([v, v])[None]
        filt = _hy_filter(zemb, _blockdiag2(w1p), pair(hy_b1[i]), pair(hy_freq[i]),
                          _blockdiag2(hy_w2[i]), pair(hy_b2[i]), _blockdiag2(hy_w3[i]), dec4)
        cw, cb = hy_conv_w[i], hy_conv_b[i][None]
        cdata = _hy_fwd_data(pc, cw, cb, ka, batch, seq)
        cfilt = _hy_fwd_filt(filt.reshape(2, seq // DFT_BLK, DFT_BLK, D_C), ka)
        dd = _hy_inner(cdata, cfilt, hy_bias[i][None], gtab, gitab)
        y_c = _hy_out(dd, pc, cw, cb, kai, batch, seq)

        xs = _mix_mlp(y_a, y_b, y_c, xs, gnorm_a[i][None], gnorm_b[i][None], gnorm_c[i][None],
                      w_out_bf, norm_mlp_g[i][None], w_up_bf, w_down_bf, final_norm_g[None], i, tm,
                      final_norm=(i == depth - 1))
    return xs.reshape(batch, seq, D_MODEL)
```

```python
import functools
import math

import numpy as np
import jax
import jax.numpy as jnp
from jax import lax
from jax.experimental import pallas as pl
from jax.experimental.pallas import tpu as pltpu

F32 = jnp.float32
BF16 = jnp.bfloat16

D_MODEL = 1024
D_A = 384
D_B = 384
D_C = 256
HEAD_DIM = 64
N_Q_HEADS = 6
N_KV_HEADS = 2
GROUP = N_Q_HEADS // N_KV_HEADS
D_KV = N_KV_HEADS * HEAD_DIM
D_QKV = D_B + 2 * D_KV
D_IN = 2 * D_A + D_QKV + 3 * D_C
C_LRU = 8.0
WINDOW = 128
BLOCK = 128
ROPE_THETA = 500000.0
ROT_DIM = HEAD_DIM // 4
HY_EMB = 33
HY_BANDS = (HY_EMB - 1) // 2
HY_WIDTH = 64
HY_TARGET = 1e-2
HY_MAX_DECAY = math.log(HY_TARGET) / 0.3
HY_MIN_DECAY = math.log(HY_TARGET) / 1.5
D_FF = 4 * D_MODEL
EPS = 1e-6
NEG = -1e30
LOG2E = math.log2(math.e)

LANES = 128
SUBLANES = 8
VMEM_LIMIT = 56 * 1024 * 1024

SCAN_SEGS = SUBLANES
DFT_BLK = 128
DFT_NBLK = 64
DFT_SUB = 8
BF16_ROWS = 16
IN_PROJ_CHUNKS = 2


def _cparams(sem):
    return pltpu.CompilerParams(dimension_semantics=sem, vmem_limit_bytes=VMEM_LIMIT)


def _rms(x, g):
    return x * lax.rsqrt(jnp.mean(x * x, axis=-1, keepdims=True) + EPS) * g


def _in_proj_kernel(x_ref, g_ref, w_ref, rc_ref, rs_ref, oa_ref, oq_ref, ov_ref, oc_ref):
    n = D_B + D_KV
    half = ROT_DIM // 2
    tm = x_ref.shape[0]
    for c in range(IN_PROJ_CHUNKS):
        r = slice(c * tm // IN_PROJ_CHUNKS, (c + 1) * tm // IN_PROJ_CHUNKS)
        h = _rms(x_ref[r, :], g_ref[...]).astype(BF16)
        oa_ref[r, :] = jnp.dot(h, w_ref[:, :2 * D_A], preferred_element_type=F32).astype(BF16)
        oc_ref[r, :] = jnp.dot(h, w_ref[:, 2 * D_A + D_QKV:], preferred_element_type=F32).astype(BF16)
        qkv = jnp.dot(h, w_ref[:, 2 * D_A:2 * D_A + D_QKV], preferred_element_type=F32)
        qk = qkv[:, :n]
        lane = lax.broadcasted_iota(jnp.int32, qk.shape, 1) % HEAD_DIM
        swapped = jnp.where(lane < half, pltpu.roll(qk, n - half, axis=1), pltpu.roll(qk, half, axis=1))
        wide = lambda t: jnp.concatenate([t[:, :2 * HEAD_DIM]] * (D_B // (2 * HEAD_DIM))
                                         + [t[:, 2 * HEAD_DIM:]], axis=1)
        oq_ref[r, :] = (qk * wide(rc_ref[r, :]) + swapped * wide(rs_ref[r, :])).astype(BF16)
        ov_ref[:, r] = qkv[:, n:].T.astype(BF16)


def _in_proj(x2, g, w_bf, layer, rc, rs, seq, tm):
    n_tok = x2.shape[0]
    nrb = seq // tm
    return pl.pallas_call(
        _in_proj_kernel,
        grid=(n_tok // tm,),
        in_specs=[
            pl.BlockSpec((tm, D_MODEL), lambda i: (i, 0)),
            pl.BlockSpec((1, D_MODEL), lambda i: (0, 0)),
            pl.BlockSpec((None, D_MODEL, D_IN), lambda i: (layer, 0, 0)),
            pl.BlockSpec((tm, 4 * HEAD_DIM), lambda i: (i % nrb, 0)),
            pl.BlockSpec((tm, 4 * HEAD_DIM), lambda i: (i % nrb, 0)),
        ],
        out_specs=[
            pl.BlockSpec((tm, 2 * D_A), lambda i: (i, 0)),
            pl.BlockSpec((tm, D_B + D_KV), lambda i: (i, 0)),
            pl.BlockSpec((D_KV, tm), lambda i: (0, i)),
            pl.BlockSpec((tm, 3 * D_C), lambda i: (i, 0)),
        ],
        out_shape=[
            jax.ShapeDtypeStruct((n_tok, 2 * D_A), BF16),
            jax.ShapeDtypeStruct((n_tok, D_B + D_KV), BF16),
            jax.ShapeDtypeStruct((D_KV, n_tok), BF16),
            jax.ShapeDtypeStruct((n_tok, 3 * D_C), BF16),
        ],
        compiler_params=_cparams(("parallel",)),
        name="in_proj",
    )(x2, g, w_bf, rc, rs)


def _dwconv_chunk(src_ref, ci, nchunk, rows, w, b, pad_left):
    halo = BF16_ROWS
    r0 = pl.multiple_of(ci * rows, rows)
    seq = nchunk * rows
    top = src_ref[pl.ds(pl.multiple_of(jnp.maximum(r0 - halo, 0), halo), halo), :]
    bot = src_ref[pl.ds(pl.multiple_of(jnp.minimum(r0 + rows, seq - halo), halo), halo), :]
    top = jnp.where(ci > 0, top, jnp.zeros_like(top))
    bot = jnp.where(ci < nchunk - 1, bot, jnp.zeros_like(bot))
    win = jnp.concatenate([top, src_ref[pl.ds(r0, rows), :], bot], axis=0).astype(F32)
    return _conv_taps(win, rows, w, b, pad_left, halo)


def _conv_taps(win, rows, w, b, pad_left, halo):
    total = rows + 2 * halo
    acc = None
    for k in range(w.shape[0]):
        shift = (pad_left - k) % total
        tap = win if shift == 0 else pltpu.roll(win, shift, axis=0)
        term = tap[halo:halo + rows] * w[k:k + 1, :]
        acc = term if acc is None else acc + term
    return acc + b


def _dwconv_block(src_ref, j, nblk, w, b, pad_left):
    r0 = j * DFT_BLK
    halo = 2 * SUBLANES
    zeros = jnp.zeros((halo, src_ref.shape[1]), src_ref.dtype)
    top = zeros if j == 0 else src_ref[r0 - halo:r0, :]
    bot = zeros if j == nblk - 1 else src_ref[r0 + DFT_BLK:r0 + DFT_BLK + halo, :]
    win = jnp.concatenate([top, src_ref[r0:r0 + DFT_BLK, :], bot], axis=0).astype(F32)
    return _conv_taps(win, DFT_BLK, w, b, pad_left, halo)


def _softplus(x):
    return jnp.maximum(x, 0.0) + jnp.log1p(jnp.exp(-jnp.abs(x)))


def _gelu_tanh(x):
    c = math.sqrt(2.0 / math.pi)
    return x * (0.5 * (1.0 + jnp.tanh(c * (x + 0.044715 * (x * x * x)))))


def _lru_kernel(u_ref, gate_ref, cw_ref, cb_ref, w_ref, bias_ref, lam_ref, o_ref,
                xci, gg, af, bf, ar, br, pfs, hfs, prs, hrs, g0, g1, *, seq, rows, unroll):
    seg = seq // SCAN_SEGS
    tiles = rows // SCAN_SEGS
    nchunk = seq // rows
    per_seg = seg // rows
    cw = cw_ref[...]
    cb = cb_ref[...]
    w = w_ref[0] * 0.5
    bias = bias_ref[0] * 0.5
    hnsp = (-0.5 * C_LRU) * _softplus(-lam_ref[0])

    def conv(ci, carry):
        r0 = pl.multiple_of(ci * rows, rows)
        dst = pl.ds((ci % per_seg) * (rows * SCAN_SEGS) + ci // per_seg, rows, stride=SCAN_SEGS)
        xci[dst, :] = _dwconv_chunk(u_ref, ci, nchunk, rows, cw, cb, 2)
        gg[dst, :] = _gelu_tanh(gate_ref[pl.ds(r0, rows), :].astype(F32))
        return carry

    lax.fori_loop(0, nchunk, conv, 0)

    grows = g0.shape[0]
    ngate = seq // grows

    def gate_matmul(ci, g_ref):
        r0 = pl.multiple_of(ci * grows, grows)
        g_ref[...] = jnp.dot(xci[pl.ds(r0, grows), :].astype(BF16), w, preferred_element_type=F32)

    def gate_math(ci, g_ref):
        r0 = pl.multiple_of(ci * grows, grows)
        xc = xci[pl.ds(r0, grows), :]
        t = jnp.tanh(g_ref[...] + bias)
        hxc = 0.5 * xc
        for d, (a_ref, b_ref) in enumerate(((af, bf), (ar, br))):
            tr = t[:, (2 * d) * LANES:(2 * d + 1) * LANES]
            ti = t[:, (2 * d + 1) * LANES:(2 * d + 2) * LANES]
            hn = hnsp[:, d * LANES:(d + 1) * LANES]
            log_a = tr * hn + hn
            a = jnp.exp(log_a)
            nem = (-1.0 - a * a) * jnp.tanh(log_a)
            a_ref[pl.ds(r0, grows), :] = a
            root = jnp.where(nem > 0.0, nem * lax.rsqrt(nem), 0.0)
            b_ref[pl.ds(r0, grows), :] = root * (ti * hxc + hxc)

    def gates(k, carry):
        gate_matmul(2 * k + 1, g1)
        gate_math(2 * k, g0)
        gate_matmul(jnp.minimum(2 * k + 2, ngate - 1), g0)
        gate_math(2 * k + 1, g1)
        return carry

    gate_matmul(0, g0)
    lax.fori_loop(0, ngate // 2, gates, 0)

    def scan(it, carry):
        span = unroll * SCAN_SEGS

        def run(p, h, a_ref, b_ref, p_out, h_out, base, order):
            rows = pl.ds(pl.multiple_of(base, span), span)
            a_blk, b_blk = a_ref[rows, :], b_ref[rows, :]
            tile = lambda blk, u: blk[u * SCAN_SEGS:(u + 1) * SCAN_SEGS]
            ps, hs = [None] * unroll, [None] * unroll
            for u0, u1 in zip(order[0::2], order[1::2]):
                a0, b0, a1, b1 = tile(a_blk, u0), tile(b_blk, u0), tile(a_blk, u1), tile(b_blk, u1)
                a01 = a1 * a0
                b01 = a1 * b0 + b1
                ps[u0], hs[u0] = a0 * p, a0 * h + b0
                p = a01 * p
                h = a01 * h + b01
                ps[u1], hs[u1] = p, h
            p_out[rows, :] = jnp.concatenate(ps, axis=0)
            h_out[rows, :] = jnp.concatenate(hs, axis=0)
            return p, h

        pf, hf, pr, hr = carry
        fwd = list(range(unroll))
        pf, hf = run(pf, hf, af, bf, pfs, hfs, it * span, fwd)
        pr, hr = run(pr, hr, ar, br, prs, hrs, (seg // unroll - 1 - it) * span, fwd[::-1])
        return pf, hf, pr, hr

    one = jnp.ones((SCAN_SEGS, LANES), F32)
    zero = jnp.zeros((SCAN_SEGS, LANES), F32)
    pf, hf, pr, hr = lax.fori_loop(0, seg // unroll, scan, (one, zero, one, zero))

    sub = lax.broadcasted_iota(jnp.int32, (SCAN_SEGS, LANES), 0)
    cf = zero
    cr = zero
    for _ in range(SCAN_SEGS - 1):
        cf = jnp.where(sub == 0, 0.0, pltpu.roll(hf + pf * cf, 1, axis=0))
        cr = jnp.where(sub == SCAN_SEGS - 1, 0.0, pltpu.roll(hr + pr * cr, SCAN_SEGS - 1, axis=0))

    def combine(ci, carry):
        r0 = pl.multiple_of(ci * rows, rows)
        sl = pl.ds(r0, rows)
        tile3 = lambda ref: ref[sl, :].reshape(tiles, SCAN_SEGS, LANES)
        h = (tile3(hfs) + tile3(pfs) * cf[None]) + (tile3(hrs) + tile3(prs) * cr[None])
        out = h.reshape(rows, LANES) * gg[sl, :]
        for k in range(tiles):
            o_ref[pl.ds(ci * tiles + k, SCAN_SEGS, stride=seg), :] = (
                out[k * SCAN_SEGS:(k + 1) * SCAN_SEGS])
        return carry

    lax.fori_loop(0, nchunk, combine, 0)


def _lru(pa, cw, cb, wg, bias, lam, batch, seq, rows=512, gate_rows=1024, unroll=16):
    n_tok = pa.shape[0]
    ng = D_A // LANES
    kern = functools.partial(_lru_kernel, seq=seq, rows=rows, unroll=unroll)
    return pl.pallas_call(
        kern,
        grid=(batch, ng),
        in_specs=[
            pl.BlockSpec((seq, LANES), lambda b, g: (b, g)),
            pl.BlockSpec((seq, LANES), lambda b, g: (b, ng + g)),
            pl.BlockSpec((cw.shape[0], LANES), lambda b, g: (0, g)),
            pl.BlockSpec((1, LANES), lambda b, g: (0, g)),
            pl.BlockSpec((1, LANES, 4 * LANES), lambda b, g: (g, 0, 0)),
            pl.BlockSpec((1, 1, 4 * LANES), lambda b, g: (g, 0, 0)),
            pl.BlockSpec((1, 1, 2 * LANES), lambda b, g: (g, 0, 0)),
        ],
        out_specs=pl.BlockSpec((seq, LANES), lambda b, g: (b, g)),
        out_shape=jax.ShapeDtypeStruct((n_tok, D_A), F32),
        scratch_shapes=[pltpu.VMEM((seq, LANES), F32) for _ in range(10)]
        + [pltpu.VMEM((gate_rows, 4 * LANES), F32) for _ in range(2)],
        compiler_params=_cparams(("parallel", "parallel")),
        name="rglru",
    )(pa, pa, cw, cb, wg, bias, lam)


_ATTN_STRAIGHT = (0, 2, 3, 5)
_ATTN_ROLLED = (1, 4)
_ONES_ROWS = 16


def _attn_kernel(sink_ref, bias_ref, q_ref, k_ref, vt_ref, o_ref, s_scr, p_scr, e_scr, *, seq):
    band = 3 * BLOCK
    nblk = seq // BLOCK
    lo_q = lax.broadcasted_iota(jnp.int32, (BLOCK, 2 * HEAD_DIM), 1) < HEAD_DIM
    ones = jnp.ones((_ONES_ROWS, band), BF16)
    nt = (((1,), (1,)), ((), ()))
    stack = _ATTN_STRAIGHT + _ATTN_ROLLED
    ns = len(_ATTN_STRAIGHT) * BLOCK

    def window(j):
        q0 = j * BLOCK
        return q0, pl.multiple_of(jnp.clip(q0 - BLOCK, 0, seq - band), BLOCK)

    def scores(j, slot):
        q0, k0 = window(j)
        rows = pl.ds(pl.multiple_of(q0, BLOCK), BLOCK)
        kb = k_ref[pl.ds(k0, band), :]
        kbs = pltpu.roll(kb, HEAD_DIM, axis=1)

        def own_half(h):
            qt = q_ref[rows, (h // 2) * 2 * HEAD_DIM:(h // 2 + 1) * 2 * HEAD_DIM]
            return jnp.where(lo_q, qt, 0.0) if h % 2 == 0 else jnp.where(lo_q, 0.0, qt)

        qa = jnp.concatenate([own_half(h) for h in _ATTN_STRAIGHT], axis=0)
        qb = jnp.concatenate([own_half(h) for h in _ATTN_ROLLED], axis=0)
        s_scr[slot, :, :ns] = lax.dot_general(kb, qa, nt, preferred_element_type=F32)
        s_scr[slot, :, ns:] = lax.dot_general(kbs, qb, nt, preferred_element_type=F32)

    def softmax(j, slot):
        q0, k0 = window(j)
        bias = bias_ref[(q0 - k0) // BLOCK]
        for h in range(N_Q_HEADS):
            src = stack.index(h) * BLOCK
            s = s_scr[slot, :, src:src + BLOCK] + bias
            sk = sink_ref[h] * LOG2E
            m = jnp.maximum(jnp.max(s, axis=0, keepdims=True), sk)
            p_scr[slot, :, h * BLOCK:(h + 1) * BLOCK] = jnp.exp2(s - m).astype(BF16)
            e_scr[slot, :, h * BLOCK:(h + 1) * BLOCK] = jnp.broadcast_to(
                jnp.exp2(sk - m), (SUBLANES, BLOCK))

    def values(j, slot):
        q0, k0 = window(j)
        vt = vt_ref[:, pl.ds(k0, band)]
        outs = []
        for kv in range(N_KV_HEADS):
            cols = slice(kv * GROUP * BLOCK, (kv + 1) * GROUP * BLOCK)
            lhs = jnp.concatenate([vt[kv * HEAD_DIM:(kv + 1) * HEAD_DIM], ones], axis=0)
            ov = jnp.dot(lhs, p_scr[slot, :, cols], preferred_element_type=F32)
            res = ov[:HEAD_DIM] / (ov[HEAD_DIM:HEAD_DIM + 1] + e_scr[slot, 0:1, cols])
            outs += [res[:, g * BLOCK:(g + 1) * BLOCK] for g in range(GROUP)]
        rows = pl.ds(pl.multiple_of(q0, BLOCK), BLOCK)
        o_ref[rows, :] = jnp.concatenate(outs, axis=0).T

    scores(0, 0)
    softmax(0, 0)
    scores(1, 1)

    def step(jj, carry):
        j = 2 * jj
        values(j, 0)
        softmax(j + 1, 1)
        scores(j + 2, 0)
        values(j + 1, 1)
        softmax(j + 2, 0)
        scores(j + 3, 1)
        return carry

    lax.fori_loop(0, (nblk - 2) // 2, step, 0)
    values(nblk - 2, 0)
    softmax(nblk - 1, 1)
    values(nblk - 1, 1)


def _attn_bias_table():
    ki = np.arange(3 * BLOCK)[None, :, None]
    qi = np.arange(BLOCK)[None, None, :]
    off = (np.arange(3) * BLOCK)[:, None, None]
    return jnp.asarray(np.where(np.abs(off + qi - ki) <= WINDOW, 0.0, NEG).astype(np.float32))


def _attention(pq, vt, sink, bias, batch, seq):
    n_tok = pq.shape[0]
    band = 3 * BLOCK
    kern = functools.partial(_attn_kernel, seq=seq)
    return pl.pallas_call(
        kern,
        grid=(batch,),
        in_specs=[
            pl.BlockSpec(memory_space=pltpu.SMEM),
            pl.BlockSpec(bias.shape, lambda b: (0, 0, 0)),
            pl.BlockSpec((seq, D_B), lambda b: (b, 0)),
            pl.BlockSpec((seq, D_KV), lambda b: (b, D_B // D_KV)),
            pl.BlockSpec((D_KV, seq), lambda b: (0, b)),
        ],
        out_specs=pl.BlockSpec((seq, D_B), lambda b: (b, 0)),
        out_shape=jax.ShapeDtypeStruct((n_tok, D_B), F32),
        scratch_shapes=[pltpu.VMEM((2, band, N_Q_HEADS * BLOCK), F32),
                        pltpu.VMEM((2, band, N_Q_HEADS * BLOCK), BF16),
                        pltpu.VMEM((2, SUBLANES, N_Q_HEADS * BLOCK), F32)],
        compiler_params=_cparams(("parallel",)),
        name="win_attn",
    )(sink, bias, pq, pq, vt)


def _hy_filter_kernel(z_ref, w1_ref, b1_ref, fr_ref, w2_ref, b2_ref, w3_ref, dec_ref, o_ref):
    hi = lax.Precision.HIGHEST
    fr = fr_ref[...]
    h = jnp.sin(fr * (jnp.dot(z_ref[...], w1_ref[...], preferred_element_type=F32, precision=hi)
                      + b1_ref[...]))
    h = jnp.sin(fr * (jnp.dot(h, w2_ref[...], preferred_element_type=F32, precision=hi) + b2_ref[...]))
    f = jnp.dot(h, w3_ref[...], preferred_element_type=F32, precision=hi) * dec_ref[...]
    for half in range(2):
        for d in range(2):
            c0 = (2 * half + d) * D_C
            o_ref[d, half] = f[:, c0:c0 + D_C]


def _hy_filter(zemb2, w1, b1, fr, w2, b2, w3, dec4, rows=512):
    n = zemb2.shape[0]
    full = lambda a: pl.BlockSpec(a.shape, lambda i: (0,) * a.ndim)
    return pl.pallas_call(
        _hy_filter_kernel,
        grid=(n // rows,),
        in_specs=[pl.BlockSpec((rows, zemb2.shape[1]), lambda i: (i, 0)),
                  full(w1), full(b1), full(fr), full(w2), full(b2), full(w3),
                  pl.BlockSpec((rows, 4 * D_C), lambda i: (i, 0))],
        out_specs=pl.BlockSpec((2, 2, rows, D_C), lambda i: (0, 0, i, 0)),
        out_shape=jax.ShapeDtypeStruct((2, 2, n, D_C), F32),
        compiler_params=_cparams(("parallel",)),
        name="hyena_filter",
    )(zemb2, w1, b1, fr, w2, b2, w3, dec4)


def _dft_outer_fwd(zs, ka_ref, o_ref):
    nj, _, lanes = zs.shape
    pair = BF16_ROWS // DFT_SUB
    for p in range(DFT_BLK // BF16_ROWS):
        parts = []
        for s in range(pair * p, pair * (p + 1)):
            xg = zs[:, s * DFT_SUB:(s + 1) * DFT_SUB, :].reshape(nj * DFT_SUB, lanes).astype(BF16)
            c = jnp.dot(ka_ref[...], xg, preferred_element_type=F32)
            parts.append(c.reshape(2, DFT_NBLK, DFT_SUB, lanes))
        o_ref[0, :, :, p * BF16_ROWS:(p + 1) * BF16_ROWS, :] = (
            jnp.concatenate(parts, axis=2).astype(BF16))


def _hy_fwd_data_kernel(x1_ref, v_ref, cw_ref, cb_ref, ka_ref, o_ref, zs, *, seq):
    nj = seq // DFT_BLK
    cw = cw_ref[...]
    cb = cb_ref[...]
    for j in range(nj):
        x1 = _dwconv_block(x1_ref, j, nj, cw[:, D_C:2 * D_C], cb[:, D_C:2 * D_C], 1)
        v = _dwconv_block(v_ref, j, nj, cw[:, 2 * D_C:], cb[:, 2 * D_C:], 1)
        zs[j] = v * x1
    _dft_outer_fwd(zs, ka_ref, o_ref)


def _hy_fwd_filt_kernel(f_ref, ka_ref, o_ref):
    _dft_outer_fwd(f_ref.at[0], ka_ref, o_ref)


def _hy_fwd_data(pc, cw, cb, ka, batch, seq):
    nj = seq // DFT_BLK
    kern = functools.partial(_hy_fwd_data_kernel, seq=seq)
    return pl.pallas_call(
        kern,
        grid=(batch,),
        in_specs=[
            pl.BlockSpec((seq, D_C), lambda b: (b, 1)),
            pl.BlockSpec((seq, D_C), lambda b: (b, 2)),
            pl.BlockSpec(cw.shape, lambda b: (0, 0)),
            pl.BlockSpec(cb.shape, lambda b: (0, 0)),
            pl.BlockSpec(ka.shape, lambda b: (0, 0)),
        ],
        out_specs=pl.BlockSpec((1, 2, DFT_NBLK, DFT_BLK, D_C), lambda b: (b, 0, 0, 0, 0)),
        out_shape=jax.ShapeDtypeStruct((batch, 2, DFT_NBLK, DFT_BLK, D_C), BF16),
        scratch_shapes=[pltpu.VMEM((nj, DFT_BLK, D_C), F32)],
        compiler_params=_cparams(("parallel",)),
        name="hyena_dft_outer",
    )(pc, pc, cw, cb, ka)


def _hy_fwd_filt(filt4, ka):
    ndir, nj = filt4.shape[0], filt4.shape[1]
    return pl.pallas_call(
        _hy_fwd_filt_kernel,
        grid=(ndir,),
        in_specs=[
            pl.BlockSpec((1, nj, DFT_BLK, D_C), lambda b: (b, 0, 0, 0)),
            pl.BlockSpec(ka.shape, lambda b: (0, 0)),
        ],
        out_specs=pl.BlockSpec((1, 2, DFT_NBLK, DFT_BLK, D_C), lambda b: (b, 0, 0, 0, 0)),
        out_shape=jax.ShapeDtypeStruct((ndir, 2, DFT_NBLK, DFT_BLK, D_C), BF16),
        compiler_params=_cparams(("parallel",)),
        name="hyena_dft_outer_filter",
    )(filt4, ka)


def _hy_inner_kernel(c_ref, f_ref, hbias_ref, g_ref, gi_ref, o_ref, *, batch, kper):
    hbias = hbias_ref[...]
    for q in range(kper):
        cols = [f_ref[d, :, q].reshape(2 * DFT_BLK, D_C) for d in range(2)]
        cols += [c_ref[b, :, q].reshape(2 * DFT_BLK, D_C) for b in range(batch)]
        x = jnp.dot(g_ref[q], jnp.concatenate(cols, axis=1), preferred_element_type=F32)
        hf, hb = x[:, :D_C], x[:, D_C:2 * D_C]
        hre = hf[:DFT_BLK] + hb[:DFT_BLK] + hbias
        him = hf[DFT_BLK:] - hb[DFT_BLK:]
        ys = []
        for b in range(batch):
            xb = x[:, (2 + b) * D_C:(3 + b) * D_C]
            xre, xim = xb[:DFT_BLK], xb[DFT_BLK:]
            ys.append(jnp.concatenate([xre * hre - xim * him, xre * him + xim * hre], axis=0))
        d = jnp.dot(gi_ref[q], jnp.concatenate(ys, axis=1).astype(BF16), preferred_element_type=F32)
        for b in range(batch):
            o_ref[b, :, q] = d[:, b * D_C:(b + 1) * D_C].reshape(2, DFT_BLK, D_C).astype(BF16)


def _hy_inner(cdata, cfilt, hbias, g, gi, kper=8):
    batch = cdata.shape[0]
    kern = functools.partial(_hy_inner_kernel, batch=batch, kper=kper)
    blk = lambda nb: pl.BlockSpec((nb, 2, kper, DFT_BLK, D_C), lambda k: (0, 0, k, 0, 0))
    return pl.pallas_call(
        kern,
        grid=(DFT_NBLK // kper,),
        in_specs=[blk(batch), blk(cfilt.shape[0]), pl.BlockSpec(hbias.shape, lambda k: (0, 0)),
                  pl.BlockSpec((kper, 2 * DFT_BLK, 2 * DFT_BLK), lambda k: (k, 0, 0)),
                  pl.BlockSpec((kper, 2 * DFT_BLK, 2 * DFT_BLK), lambda k: (k, 0, 0))],
        out_specs=blk(batch),
        out_shape=jax.ShapeDtypeStruct(cdata.shape, BF16),
        compiler_params=_cparams(("parallel",)),
        name="hyena_dft_inner",
    )(cdata, cfilt, hbias, g, gi)


def _hy_out_kernel(d_ref, x0_ref, cw_ref, cb_ref, kai_ref, o_ref, ys, *, seq):
    nj = seq // DFT_BLK
    pair = BF16_ROWS // DFT_SUB
    for p in range(DFT_BLK // BF16_ROWS):
        d = d_ref[0, :, :, p * BF16_ROWS:(p + 1) * BF16_ROWS, :].astype(F32)
        for s in range(pair):
            rhs = d[:, :, s * DFT_SUB:(s + 1) * DFT_SUB, :].reshape(2 * DFT_NBLK * DFT_SUB, D_C)
            y = jnp.dot(kai_ref[...], rhs.astype(BF16), preferred_element_type=F32)
            r0 = p * BF16_ROWS + s * DFT_SUB
            ys[:, r0:r0 + DFT_SUB, :] = y.reshape(nj, DFT_SUB, D_C)
    cw = cw_ref[...]
    cb = cb_ref[...]
    for j in range(nj):
        x0 = _dwconv_block(x0_ref, j, nj, cw[:, :D_C], cb[:, :D_C], 1)
        o_ref[j * DFT_BLK:(j + 1) * DFT_BLK, :] = ys[j] * x0


def _hy_out(dd, pc, cw, cb, kai, batch, seq):
    n_tok = pc.shape[0]
    nj = seq // DFT_BLK
    kern = functools.partial(_hy_out_kernel, seq=seq)
    whole = lambda a: pl.BlockSpec(a.shape, lambda b: (0, 0))
    return pl.pallas_call(
        kern,
        grid=(batch,),
        in_specs=[
            pl.BlockSpec((1, 2, DFT_NBLK, DFT_BLK, D_C), lambda b: (b, 0, 0, 0, 0)),
            pl.BlockSpec((seq, D_C), lambda b: (b, 0)),
            whole(cw), whole(cb), whole(kai),
        ],
        out_specs=pl.BlockSpec((seq, D_C), lambda b: (b, 0)),
        out_shape=jax.ShapeDtypeStruct((n_tok, D_C), F32),
        scratch_shapes=[pltpu.VMEM((nj, DFT_BLK, D_C), F32)],
        compiler_params=_cparams(("parallel",)),
        name="hyena_out",
    )(dd, pc, cw, cb, kai)


def _mix_mlp_kernel(ya_ref, yb_ref, yc_ref, x_ref, ga_ref, gb_ref, gc_ref, wo_ref, g_ref, wu_ref,
                    wd_ref, gf_ref, o_ref, *, ff_chunk, final_norm):
    y = jnp.concatenate([_rms(ya_ref[...], ga_ref[...]), _rms(yb_ref[...], gb_ref[...]),
                         _rms(yc_ref[...], gc_ref[...])], axis=-1).astype(BF16)
    x = x_ref[...] + jnp.dot(y, wo_ref[...], preferred_element_type=F32)
    h = _rms(x, g_ref[...]).astype(BF16)
    acc = x
    for c in range(D_FF // ff_chunk):
        sl = slice(c * ff_chunk, (c + 1) * ff_chunk)
        u = jnp.maximum(jnp.dot(h, wu_ref[:, sl], preferred_element_type=F32), 0.0)
        acc = acc + jnp.dot((u * u).astype(BF16), wd_ref[sl, :], preferred_element_type=F32)
    if final_norm:
        acc = _rms(acc, gf_ref[...])
    o_ref[...] = acc


def _mix_mlp(ya, yb, yc, x2, ga, gb, gc, wo_bf, g, wu_bf, wd_bf, gf, layer, tm, final_norm,
             ff_chunk=1024):
    n_tok = x2.shape[0]
    kern = functools.partial(_mix_mlp_kernel, ff_chunk=ff_chunk, final_norm=final_norm)
    row = lambda w: pl.BlockSpec((tm, w), lambda i: (i, 0))
    vec = lambda w: pl.BlockSpec((1, w), lambda i: (0, 0))
    resident = lambda r, c: pl.BlockSpec((None, r, c), lambda i: (layer, 0, 0),
                                         pipeline_mode=pl.Buffered(1))
    return pl.pallas_call(
        kern,
        grid=(n_tok // tm,),
        in_specs=[row(D_A), row(D_B), row(D_C), row(D_MODEL), vec(D_A), vec(D_B), vec(D_C),
                  resident(D_MODEL, D_MODEL), vec(D_MODEL), resident(D_MODEL, D_FF),
                  resident(D_FF, D_MODEL), vec(D_MODEL)],
        out_specs=row(D_MODEL),
        out_shape=jax.ShapeDtypeStruct((n_tok, D_MODEL), F32),
        compiler_params=_cparams(("parallel",)),
        name="mix_mlp",
    )(ya, yb, yc, x2, ga, gb, gc, wo_bf, g, wu_bf, wd_bf, gf)


def _rope_tables(seq):
    pos = np.arange(seq, dtype=np.float32)
    inv_freq = (np.float32(ROPE_THETA) ** (-np.arange(0, ROT_DIM, 2, dtype=np.float32) / ROT_DIM))
    ang = (pos[:, None] * inv_freq[None, :]).astype(np.float32)
    cos, sin = np.cos(ang).astype(np.float32), np.sin(ang).astype(np.float32)
    rest = HEAD_DIM - ROT_DIM
    c_head = np.concatenate([cos, cos, np.ones((seq, rest), np.float32)], axis=1)
    s_head = np.concatenate([-sin, sin, np.zeros((seq, rest), np.float32)], axis=1)
    scale = np.float32(HEAD_DIM ** -0.5 * LOG2E)
    pair = lambda t: np.tile(t, (1, 2))
    c = np.concatenate([pair(c_head) * scale, pair(c_head)], axis=1)
    s = np.concatenate([pair(s_head) * scale, pair(s_head)], axis=1)
    return jnp.asarray(c), jnp.asarray(s)


def _hyena_position_tables(seq):
    t = np.linspace(0.0, 1.0, seq, dtype=np.float32)[:, None]
    w = (2.0 * math.pi * np.arange(seq, dtype=np.float32)[:, None] / seq).astype(np.float32)
    f = np.linspace(1e-4, HY_BANDS - 1, HY_BANDS, dtype=np.float32)[None, :]
    fw = (f * w).astype(np.float32)
    z = np.concatenate([t, np.cos(fw), -np.sin(fw)], axis=-1).astype(np.float32)
    deltas = np.abs(np.linspace(HY_MIN_DECAY, HY_MAX_DECAY, D_C, dtype=np.float32))
    decay = np.exp(-t * deltas[None, :]).astype(np.float32)
    decay_b = decay.copy()
    decay_b[0] = 0.0
    half = seq // 2
    zp = np.zeros((half, 2, HY_WIDTH), np.float32)
    zp[:, 0, :HY_EMB] = z[:half]
    zp[:, 1, :HY_EMB] = z[half:]
    dec4 = np.concatenate([decay[:half], decay_b[:half], decay[half:], decay_b[half:]], axis=1)
    return jnp.asarray(zp.reshape(half, 2 * HY_WIDTH)), jnp.asarray(dec4)


def _blockdiag2(a):
    z = jnp.zeros_like(a)
    return jnp.concatenate([jnp.concatenate([a, z], axis=1), jnp.concatenate([z, a], axis=1)], axis=0)


def _dft_tables(seq):
    n = 2 * seq
    nj = seq // DFT_BLK
    kk = np.arange(DFT_NBLK)
    ang = 2.0 * np.pi * np.outer(kk, np.arange(nj)) / DFT_NBLK
    eye = np.eye(DFT_SUB)
    ka = np.concatenate([np.kron(np.cos(ang), eye), np.kron(-np.sin(ang), eye)], axis=0)
    kai = np.concatenate([np.kron(np.cos(ang).T, eye), np.kron(-np.sin(ang).T, eye)], axis=1) / n
    m = np.arange(DFT_BLK)
    k = kk[:, None, None] + DFT_NBLK * np.arange(DFT_BLK)[None, :, None]
    ph = 2.0 * np.pi * ((k * m[None, None, :]) % n) / n
    gre, gim = np.cos(ph), -np.sin(ph)
    g = np.concatenate([np.concatenate([gre, -gim], axis=2), np.concatenate([gim, gre], axis=2)], axis=1)
    gi = np.transpose(g, (0, 2, 1))
    as_bf = lambda a: jnp.asarray(a.astype(np.float32)).astype(BF16)
    return as_bf(ka), as_bf(kai), as_bf(g), as_bf(gi)


def _lru_blockdiag(w):
    nb = w.shape[1] // 2
    w = w.reshape(2, nb, 2, w.shape[2], w.shape[3])
    z = jnp.zeros_like(w[:, :, 0])
    top = jnp.concatenate([w[:, :, 0], z], axis=-1)
    bot = jnp.concatenate([z, w[:, :, 1]], axis=-1)
    return jnp.concatenate([top, bot], axis=-2)


def kernel(x, norm_mix_g, w_in, conv_a_w, conv_a_b, lru_wa, lru_ba, lru_wx, lru_bx, lru_lambda,
           attn_sink, hy_conv_w, hy_conv_b, hy_w1, hy_b1, hy_freq, hy_w2, hy_b2, hy_w3, hy_bias,
           gnorm_a, gnorm_b, gnorm_c, w_out, norm_mlp_g, w_up, w_down, final_norm_g):
    batch, seq, _ = x.shape
    depth = w_in.shape[0]
    n_tok = batch * seq
    tm = 1024
    ng = D_A // LANES

    rc, rs = _rope_tables(seq)
    attn_bias = _attn_bias_table()
    zemb, dec4 = _hyena_position_tables(seq)
    ka, kai, gtab, gitab = _dft_tables(seq)

    w_in_bf, w_out_bf = w_in.astype(BF16), w_out.astype(BF16)
    w_up_bf, w_down_bf = w_up.astype(BF16), w_down.astype(BF16)

    xs = x.reshape(n_tok, D_MODEL)
    for i in range(depth):
        pa, pq, vt, pc = _in_proj(xs, norm_mix_g[i][None], w_in_bf, i, rc, rs, seq, tm)

        wa, wx = _lru_blockdiag(lru_wa[i]), _lru_blockdiag(lru_wx[i])
        wg = jnp.concatenate([wa[0], wx[0], wa[1], wx[1]], axis=-1).astype(BF16)
        tile = lambda v: v.reshape(ng, 1, LANES)
        bias = jnp.concatenate([tile(lru_ba[i][0]), tile(lru_bx[i][0]),
                                tile(lru_ba[i][1]), tile(lru_bx[i][1])], axis=-1)
        lam = jnp.concatenate([tile(lru_lambda[i][0]), tile(lru_lambda[i][1])], axis=-1)
        y_a = _lru(pa, conv_a_w[i], conv_a_b[i][None], wg, bias, lam, batch, seq)

        y_b = _attention(pq, vt, attn_sink[i], attn_bias, batch, seq)

        w1p = jnp.zeros((HY_WIDTH, HY_WIDTH), F32).at[:HY_EMB].set(hy_w1[i])
        pair = lambda v: jnp.concatenate([v, v])[None]
        filt = _hy_filter(zemb, _blockdiag2(w1p), pair(hy_b1[i]), pair(hy_freq[i]),
                          _blockdiag2(hy_w2[i]), pair(hy_b2[i]), _blockdiag2(hy_w3[i]), dec4)
        cw, cb = hy_conv_w[i], hy_conv_b[i][None]
        cdata = _hy_fwd_data(pc, cw, cb, ka, batch, seq)
        cfilt = _hy_fwd_filt(filt.reshape(2, seq // DFT_BLK, DFT_BLK, D_C), ka)
        dd = _hy_inner(cdata, cfilt, hy_bias[i][None], gtab, gitab)
        y_c = _hy_out(dd, pc, cw, cb, kai, batch, seq)

        xs = _mix_mlp(y_a, y_b, y_c, xs, gnorm_a[i][None], gnorm_b[i][None], gnorm_c[i][None],
                      w_out_bf, norm_mlp_g[i][None], w_up_bf, w_down_bf, final_norm_g[None], i, tm,
                      final_norm=(i == depth - 1))
    return xs.reshape(batch, seq, D_MODEL)
```
